```python
import jax, jax.numpy as jnp
from jax import lax
import numpy as np

D_MODEL = 1024
BATCH = 2
SEQ = 8192
DEPTH = 2

HEAD_DIM = 64
ATTN_WIDTH = D_MODEL // 2
ATTN_HEADS = ATTN_WIDTH // HEAD_DIM
DILATED_PATTERNS = ((128, 1), (512, 4), (2048, 16))
ROPE_THETA = 10000.0
POOL_WINDOWS = (2, 4, 8, 16)
POOL_WIDTH = D_MODEL // 4
POOL_GROUP = POOL_WIDTH // len(POOL_WINDOWS)
MEM_WIDTH = D_MODEL // 4
MEM_HEADS = 4
MEM_HEAD_DIM = MEM_WIDTH // MEM_HEADS
N_MEM = 256
MIX_WIDTH = ATTN_WIDTH + POOL_WIDTH + MEM_WIDTH
IN_COLS = 3 * ATTN_WIDTH + POOL_WIDTH + MEM_WIDTH
N_EXPERTS = 32
TOP_K = 4
D_EXPERT = D_MODEL
SWIGLU_ALPHA = 1.702
SWIGLU_LIMIT = 7.0
MOE_BLOCK = 256
NORM_EPS = 1e-5
NEG_INF = -1e30

kernel_name = 'hybrid_dilated_pool_memory_moe_encoder'


def rms_norm(x, g):
    x32 = x.astype(jnp.float32)
    y = x32 * lax.rsqrt(jnp.mean(x32 * x32, axis=-1, keepdims=True) + NORM_EPS)
    return (y * g.astype(jnp.float32)).astype(x.dtype)


def rope_tables(seq):
    half = HEAD_DIM // 2
    inv_freq = ROPE_THETA ** (-jnp.arange(half, dtype=jnp.float32) / half)
    ang = jnp.arange(seq, dtype=jnp.float32)[:, None] * inv_freq[None, :]
    return jnp.cos(ang), jnp.sin(ang)


def apply_rope(t, cos, sin):
    half = t.shape[-1] // 2
    t32 = t.astype(jnp.float32)
    t1, t2 = t32[..., :half], t32[..., half:]
    c, s = cos[None, :, None, :], sin[None, :, None, :]
    return jnp.concatenate([t1 * c - t2 * s, t1 * s + t2 * c], axis=-1).astype(t.dtype)


def dilated_window_attention(q, k, v, window, dilation):
    B, S, H, hd = q.shape
    half = (window // 2) // dilation
    L = S // dilation
    nb = -(-L // half)
    Lp = nb * half

    def split(t):
        return t.reshape(B, L, dilation, H, hd).transpose(0, 2, 3, 1, 4).astype(jnp.float32)

    qs, ks, vs = split(q), split(k), split(v)
    qb = jnp.pad(qs, ((0, 0), (0, 0), (0, 0), (0, Lp - L), (0, 0))).reshape(B, dilation, H, nb, half, hd)
    kv_pad = ((0, 0), (0, 0), (0, 0), (half, Lp - L + half), (0, 0))
    ks = jnp.pad(ks, kv_pad).reshape(B, dilation, H, nb + 2, half, hd)
    vs = jnp.pad(vs, kv_pad).reshape(B, dilation, H, nb + 2, half, hd)
    kb = jnp.concatenate([ks[:, :, :, 0:nb], ks[:, :, :, 1:nb + 1], ks[:, :, :, 2:nb + 2]], axis=4)
    vb = jnp.concatenate([vs[:, :, :, 0:nb], vs[:, :, :, 1:nb + 1], vs[:, :, :, 2:nb + 2]], axis=4)

    scores = jnp.einsum('bdhnqe,bdhnke->bdhnqk', qb, kb) * (hd ** -0.5)
    r = jnp.arange(half)[:, None]
    c = jnp.arange(3 * half)[None, :]
    rel = c - half - r
    key_idx = jnp.arange(nb)[:, None, None] * half + c[None] - half
    valid = (jnp.abs(rel)[None] <= half) & (key_idx >= 0) & (key_idx < L)
    scores = jnp.where(valid, scores, NEG_INF)
    m = jnp.max(scores, axis=-1, keepdims=True)
    p = jnp.exp(scores - m)
    den = jnp.sum(p, axis=-1, keepdims=True)
    out = jnp.einsum('bdhnqk,bdhnke->bdhnqe', p, vb) / den
    lse = (m + jnp.log(den))[..., 0]

    out = out.reshape(B, dilation, H, Lp, hd)[:, :, :, :L].transpose(0, 3, 1, 2, 4).reshape(B, S, H, hd)
    lse = lse.reshape(B, dilation, H, Lp)[..., :L].transpose(0, 3, 1, 2).reshape(B, S, H)
    return out, lse


def dilated_mixture(q, k, v):
    outs, lses = [], []
    for window, dilation in DILATED_PATTERNS:
        o, l = dilated_window_attention(q, k, v, window, dilation)
        outs.append(o)
        lses.append(l)
    wts = jax.nn.softmax(jnp.stack(lses, axis=0), axis=0)
    return jnp.sum(wts[..., None] * jnp.stack(outs, axis=0), axis=0)


def multiscale_pool(u, pool_w, pool_scale):
    B, S, C = u.shape
    u32 = u.astype(jnp.float32)
    cs = jnp.pad(jnp.cumsum(u32, axis=1), ((0, 0), (1, 0), (0, 0)))
    pos = jnp.arange(S)
    diffs = []
    for g, w in enumerate(POOL_WINDOWS):
        left, right = w // 2, w - 1 - w // 2
        lo = jnp.maximum(pos - left, 0)
        hi = jnp.minimum(pos + right, S - 1) + 1
        seg = cs[..., g * POOL_GROUP:(g + 1) * POOL_GROUP]
        mean = (seg[:, hi] - seg[:, lo]) / (hi - lo).astype(jnp.float32)[None, :, None]
        diffs.append(mean - u32[..., g * POOL_GROUP:(g + 1) * POOL_GROUP])
    d = jnp.stack(diffs, axis=2)
    y = jnp.einsum('bsgc,gcd->bsgd', d, pool_w.astype(jnp.float32)).reshape(B, S, C)
    return (y * pool_scale.astype(jnp.float32)).astype(u.dtype)


def memory_attention(q, mem, mem_norm_g, w_mem_kv):
    B, S, H, hd = q.shape
    n_mem = mem.shape[1]
    kv = rms_norm(mem, mem_norm_g) @ w_mem_kv
    k = kv[..., :MEM_WIDTH].reshape(B, n_mem, H, hd).astype(jnp.float32)
    v = kv[..., MEM_WIDTH:].reshape(B, n_mem, H, hd).astype(jnp.float32)
    s = jnp.einsum('bshe,bmhe->bhsm', q.astype(jnp.float32), k) * (hd ** -0.5)
    p = jax.nn.softmax(s, axis=-1)
    o = jnp.einsum('bhsm,bmhe->bshe', p, v)
    return o.reshape(B, S, H * hd).astype(q.dtype)


def parallel_mixers(h, mem, w_in, pool_w, pool_scale, mem_norm_g, w_mem_kv, grp_norm_g, w_out, cos, sin):
    B, S, _ = h.shape
    a, p = ATTN_WIDTH, POOL_WIDTH
    proj = h @ w_in
    q_a = apply_rope(proj[..., 0:a].reshape(B, S, ATTN_HEADS, HEAD_DIM), cos, sin)
    k_a = apply_rope(proj[..., a:2 * a].reshape(B, S, ATTN_HEADS, HEAD_DIM), cos, sin)
    v_a = proj[..., 2 * a:3 * a].reshape(B, S, ATTN_HEADS, HEAD_DIM)
    u_p = proj[..., 3 * a:3 * a + p]
    q_m = proj[..., 3 * a + p:].reshape(B, S, MEM_HEADS, MEM_HEAD_DIM)

    y_a = dilated_mixture(q_a, k_a, v_a).reshape(B, S, a).astype(h.dtype)
    y_p = multiscale_pool(u_p, pool_w, pool_scale)
    y_m = memory_attention(q_m, mem, mem_norm_g, w_mem_kv)
    y = jnp.concatenate([rms_norm(y_a, grp_norm_g[:a]),
                         rms_norm(y_p, grp_norm_g[a:a + p]),
                         rms_norm(y_m, grp_norm_g[a + p:])], axis=-1)
    return y @ w_out


def moe_ffn(h, router_w, router_b, w_gu, b_gu, w_down, b_down):
    B, S, D = h.shape
    T = B * S
    xt = h.reshape(T, D)
    logits = (xt @ router_w + router_b).astype(jnp.float32)
    top_val, top_idx = lax.top_k(logits, TOP_K)
    gates = jax.nn.softmax(top_val, axis=-1)

    n_assign = T * TOP_K
    flat_e = top_idx.reshape(-1)
    flat_tok = jnp.arange(n_assign, dtype=jnp.int32) // TOP_K
    order = jnp.argsort(flat_e, stable=True)
    e_sorted = flat_e[order]
    tok_sorted = flat_tok[order]
    gate_sorted = gates.reshape(-1)[order]

    counts = jnp.bincount(flat_e, length=N_EXPERTS)
    padded = (counts + MOE_BLOCK - 1) // MOE_BLOCK * MOE_BLOCK
    pad_end = jnp.cumsum(padded)
    pad_start = pad_end - padded
    start = jnp.cumsum(counts) - counts
    dest = pad_start[e_sorted] + jnp.arange(n_assign, dtype=jnp.int32) - start[e_sorted]
    n_blocks = -(-n_assign // MOE_BLOCK) + N_EXPERTS
    slot_tok = jnp.zeros((n_blocks * MOE_BLOCK,), jnp.int32).at[dest].set(tok_sorted)
    block_e = jnp.minimum(jnp.searchsorted(pad_end, jnp.arange(n_blocks) * MOE_BLOCK, side='right'), N_EXPERTS - 1)

    def expert_block(args):
        tok, e = args
        gu = xt[tok] @ w_gu[e] + b_gu[e]
        gate = jnp.minimum(gu[:, :D_EXPERT], SWIGLU_LIMIT)
        up = jnp.clip(gu[:, D_EXPERT:], -SWIGLU_LIMIT, SWIGLU_LIMIT)
        act = (up + 1) * gate * jax.nn.sigmoid(SWIGLU_ALPHA * gate)
        return act @ w_down[e] + b_down[e]

    y_slots = lax.map(expert_block, (slot_tok.reshape(n_blocks, MOE_BLOCK), block_e)).reshape(-1, D)
    y = y_slots[dest] * gate_sorted[:, None].astype(y_slots.dtype)
    return jax.ops.segment_sum(y, tok_sorted, num_segments=T).reshape(B, S, D)


def setup_inputs(seed: int = 0) -> dict:
    key = jax.random.key(seed)
    ks = jax.random.split(key, 18)
    f32 = jnp.float32

    def nrm(k, shape, scale):
        return jax.random.normal(k, shape, f32) * scale

    def gain(k, shape):
        return 1.0 + 0.02 * jax.random.normal(k, shape, f32)

    return {
        'x': nrm(ks[0], (BATCH, SEQ, D_MODEL), 1.0),
        'mem': nrm(ks[1], (BATCH, N_MEM, D_MODEL), 1.0),
        'norm1_g': gain(ks[2], (DEPTH, D_MODEL)),
        'w_in': nrm(ks[3], (DEPTH, D_MODEL, IN_COLS), D_MODEL ** -0.5),
        'pool_w': nrm(ks[4], (DEPTH, len(POOL_WINDOWS), POOL_GROUP, POOL_GROUP), POOL_GROUP ** -0.5),
        'pool_scale': gain(ks[5], (DEPTH, POOL_WIDTH)),
        'mem_norm_g': gain(ks[6], (DEPTH, D_MODEL)),
        'w_mem_kv': nrm(ks[7], (DEPTH, D_MODEL, 2 * MEM_WIDTH), D_MODEL ** -0.5),
        'grp_norm_g': gain(ks[8], (DEPTH, MIX_WIDTH)),
        'w_out': nrm(ks[9], (DEPTH, MIX_WIDTH, D_MODEL), MIX_WIDTH ** -0.5),
        'norm2_g': gain(ks[10], (DEPTH, D_MODEL)),
        'router_w': nrm(ks[11], (DEPTH, D_MODEL, N_EXPERTS), D_MODEL ** -0.5),
        'router_b': nrm(ks[12], (DEPTH, N_EXPERTS), 0.01),
        'w_gu': nrm(ks[13], (DEPTH, N_EXPERTS, D_MODEL, 2 * D_EXPERT), D_MODEL ** -0.5),
        'b_gu': nrm(ks[14], (DEPTH, N_EXPERTS, 2 * D_EXPERT), 0.01),
        'w_down': nrm(ks[15], (DEPTH, N_EXPERTS, D_EXPERT, D_MODEL), D_EXPERT ** -0.5),
        'b_down': nrm(ks[16], (DEPTH, N_EXPERTS, D_MODEL), 0.01),
        'final_g': gain(ks[17], (D_MODEL,)),
    }


def reference(x, mem, norm1_g, w_in, pool_w, pool_scale, mem_norm_g, w_mem_kv, grp_norm_g, w_out,
              norm2_g, router_w, router_b, w_gu, b_gu, w_down, b_down, final_g):
    cos, sin = rope_tables(x.shape[1])
    for l in range(DEPTH):
        h = rms_norm(x, norm1_g[l])
        x = x + parallel_mixers(h, mem, w_in[l], pool_w[l], pool_scale[l], mem_norm_g[l], w_mem_kv[l],
                                grp_norm_g[l], w_out[l], cos, sin)
        h = rms_norm(x, norm2_g[l])
        x = x + moe_ffn(h, router_w[l], router_b[l], w_gu[l], b_gu[l], w_down[l], b_down[l])
    return rms_norm(x, final_g)
```

```python
import functools

import jax
import jax.numpy as jnp
from jax import lax
from jax.experimental import pallas as pl
from jax.experimental.pallas import tpu as pltpu

D_MODEL = 1024
HEAD_DIM = 64
ATTN_WIDTH = 512
DILATIONS = (1, 4, 16)
BAND = 64
ROPE_THETA = 10000.0
POOL_WINDOWS = (2, 4, 8, 16)
POOL_WIDTH = 256
POOL_GROUP = 64
POOL_HALO = 8
MEM_WIDTH = 256
N_EXPERTS = 32
TOP_K = 4
D_EXPERT = 1024
SWIGLU_ALPHA = 1.702
SWIGLU_LIMIT = 7.0
MOE_BLOCK = 256
NORM_EPS = 1e-5
NEG_INF = -1e30
LANES = 128

F32 = jnp.float32
BF16 = jnp.bfloat16
VMEM_LIMIT = 56 * 1024 * 1024

_NT = (((1,), (1,)), ((), ()))


def _params(sem, vmem=VMEM_LIMIT):
    return pltpu.CompilerParams(dimension_semantics=sem, vmem_limit_bytes=vmem)


def _rms(x, g):
    return x * lax.rsqrt(jnp.mean(x * x, axis=-1, keepdims=True) + NORM_EPS) * g


def _in_proj_kernel(x_ref, g_ref, w_ref, cos_ref, sin_ref, q_ref, k_ref, v_ref, u_ref, qm_ref):
    h = _rms(x_ref[...], g_ref[...]).astype(BF16)
    proj = jnp.dot(h, w_ref[...], preferred_element_type=F32)
    cos = cos_ref[...]
    sin = sin_ref[...]
    lane = lax.broadcasted_iota(jnp.int32, cos.shape, 1)
    first_half = (lane % HEAD_DIM) < (HEAD_DIM // 2)
    scale = HEAD_DIM ** -0.5

    def rope(t):
        partner = jnp.where(first_half, pltpu.roll(t, LANES - HEAD_DIM // 2, 1),
                            pltpu.roll(t, HEAD_DIM // 2, 1))
        return t * cos + partner * sin

    a = ATTN_WIDTH
    for c in range(a // LANES):
        sl = slice(c * LANES, (c + 1) * LANES)
        q_ref[:, sl] = (rope(proj[:, c * LANES:(c + 1) * LANES]) * scale).astype(BF16)
        k_ref[:, sl] = rope(proj[:, a + c * LANES:a + (c + 1) * LANES]).astype(BF16)
    v_ref[...] = proj[:, 2 * a:3 * a].astype(BF16)
    u_ref[...] = proj[:, 3 * a:3 * a + POOL_WIDTH]
    qm_ref[...] = (proj[:, 3 * a + POOL_WIDTH:] * scale).astype(BF16)


def _in_proj(x2, g, w_bf, cos, sin, seq):
    t, d = x2.shape
    tm = 512
    n_seq_tiles = seq // tm
    cols = w_bf.shape[1]
    row = lambda i: (i, 0)
    fixed = lambda i: (0, 0)
    return pl.pallas_call(
        _in_proj_kernel,
        grid=(t // tm,),
        in_specs=[pl.BlockSpec((tm, d), row),
                  pl.BlockSpec((1, d), fixed),
                  pl.BlockSpec((d, cols), fixed),
                  pl.BlockSpec((tm, LANES), lambda i: (i % n_seq_tiles, 0)),
                  pl.BlockSpec((tm, LANES), lambda i: (i % n_seq_tiles, 0))],
        out_specs=[pl.BlockSpec((tm, ATTN_WIDTH), row),
                   pl.BlockSpec((tm, ATTN_WIDTH), row),
                   pl.BlockSpec((tm, ATTN_WIDTH), row),
                   pl.BlockSpec((tm, POOL_WIDTH), row),
                   pl.BlockSpec((tm, MEM_WIDTH), row)],
        out_shape=[jax.ShapeDtypeStruct((t, ATTN_WIDTH), BF16),
                   jax.ShapeDtypeStruct((t, ATTN_WIDTH), BF16),
                   jax.ShapeDtypeStruct((t, ATTN_WIDTH), BF16),
                   jax.ShapeDtypeStruct((t, POOL_WIDTH), F32),
                   jax.ShapeDtypeStruct((t, MEM_WIDTH), BF16)],
        compiler_params=_params(("parallel",)),
        name="in_proj",
    )(x2, g, w_bf, cos, sin)


def _head_pair_attention(q, k, v, valid):
    lane = lax.broadcasted_iota(jnp.int32, q.shape, 1)
    outs, lses = [], []
    for half in range(2):
        mine = (lane // HEAD_DIM) == half
        s = lax.dot_general(jnp.where(mine, q, jnp.zeros_like(q)), k, _NT,
                            preferred_element_type=F32)
        if valid is not None:
            s = jnp.where(valid, s, NEG_INF)
        m = jnp.max(s, axis=-1, keepdims=True)
        p = jnp.exp(s - m)
        den = jnp.sum(p, axis=-1, keepdims=True)
        pv = jnp.dot(p.astype(BF16), v, preferred_element_type=F32)
        outs.append(pv / den)
        lses.append(m + jnp.log(den))
    return jnp.where((lane // HEAD_DIM) == 0, outs[0], outs[1]), lses


def _attn_kernel(q_ref, kp_ref, kc_ref, kn_ref, vp_ref, vc_ref, vn_ref, o_ref, lse_ref, *, tq, length):
    j = pl.program_id(2)
    nk = tq + 2 * BAND
    row = lax.broadcasted_iota(jnp.int32, (tq, nk), 0)
    col = lax.broadcasted_iota(jnp.int32, (tq, nk), 1)
    key = j * tq - BAND + col
    valid = (jnp.abs(col - BAND - row) <= BAND) & (key >= 0) & (key < length)
    lane = lax.broadcasted_iota(jnp.int32, (tq, LANES), 1)
    lse_tile = jnp.zeros((tq, LANES), F32)
    for c in range(ATTN_WIDTH // LANES):
        sl = slice(c * LANES, (c + 1) * LANES)
        k = jnp.concatenate([kp_ref[0, :, sl], kc_ref[0, :, sl], kn_ref[0, :, sl]], axis=0)
        v = jnp.concatenate([vp_ref[0, :, sl], vc_ref[0, :, sl], vn_ref[0, :, sl]], axis=0)
        o, lses = _head_pair_attention(q_ref[0, :, sl], k, v, valid)
        o_ref[0, :, sl] = o
        for half in range(2):
            lse_tile = jnp.where(lane == 2 * c + half, lses[half], lse_tile)
    lse_ref[0] = lse_tile


def _dilated_attention(q, k, v, batch, seq, dilation):
    length = seq // dilation
    tq = 128
    per = tq // BAND
    n_band_blocks = length // BAND
    view = lambda a: a.reshape(batch, length, dilation * a.shape[-1])
    cur = lambda b, r, j: (b, j, r)
    prev = lambda b, r, j: (b, jnp.maximum(j * per - 1, 0), r)
    nxt = lambda b, r, j: (b, jnp.minimum((j + 1) * per, n_band_blocks - 1), r)
    big = lambda imap: pl.BlockSpec((1, tq, ATTN_WIDTH), imap)
    halo = lambda imap: pl.BlockSpec((1, BAND, ATTN_WIDTH), imap)
    o, lse = pl.pallas_call(
        functools.partial(_attn_kernel, tq=tq, length=length),
        grid=(batch, dilation, length // tq),
        in_specs=[big(cur), halo(prev), big(cur), halo(nxt), halo(prev), big(cur), halo(nxt)],
        out_specs=[pl.BlockSpec((1, tq, ATTN_WIDTH), cur), pl.BlockSpec((1, tq, LANES), cur)],
        out_shape=[jax.ShapeDtypeStruct((batch, length, dilation * ATTN_WIDTH), F32),
                   jax.ShapeDtypeStruct((batch, length, dilation * LANES), F32)],
        compiler_params=_params(("parallel", "parallel", "parallel")),
        name=f"dilated_attn_d{dilation}",
    )(view(q), view(k), view(k), view(k), view(v), view(v), view(v))
    return o.reshape(batch * seq, ATTN_WIDTH), lse.reshape(batch * seq, LANES)


def _mem_kv_kernel(m_ref, g_ref, w_ref, kv_ref):
    h = _rms(m_ref[...], g_ref[...]).astype(BF16)
    kv_ref[...] = jnp.dot(h, w_ref[...], preferred_element_type=F32).astype(BF16)


def _mem_kv(mem2, g, w_bf):
    n, d = mem2.shape
    cols = w_bf.shape[1]
    tm = 256
    return pl.pallas_call(
        _mem_kv_kernel,
        grid=(n // tm,),
        in_specs=[pl.BlockSpec((tm, d), lambda i: (i, 0)),
                  pl.BlockSpec((1, d), lambda i: (0, 0)),
                  pl.BlockSpec((d, cols), lambda i: (0, 0))],
        out_specs=pl.BlockSpec((tm, cols), lambda i: (i, 0)),
        out_shape=jax.ShapeDtypeStruct((n, cols), BF16),
        compiler_params=_params(("parallel",)),
        name="mem_kv",
    )(mem2, g, w_bf)


def _mix_out_kernel(o1_ref, o2_ref, o3_ref, l1_ref, l2_ref, l3_ref, up_ref, u_ref, un_ref, qm_ref, kv_ref,
                    x_ref, pw_ref, ps_ref, gg_ref, wo_ref, n2_ref, rw_ref, rb_ref,
                    x1_ref, h2_ref, idx_ref, gate_ref, *, tm, seq):
    i = pl.program_id(0)
    tiles_per_seq = seq // tm
    pos0 = (i % tiles_per_seq) * tm

    l1, l2, l3 = l1_ref[...], l2_ref[...], l3_ref[...]
    lm = jnp.maximum(jnp.maximum(l1, l2), l3)
    e1, e2, e3 = jnp.exp(l1 - lm), jnp.exp(l2 - lm), jnp.exp(l3 - lm)
    es = e1 + e2 + e3
    w1, w2, w3 = e1 / es, e2 / es, e3 / es
    ya = []
    for c in range(ATTN_WIDTH // LANES):
        sl = slice(c * LANES, (c + 1) * LANES)
        lane = lax.broadcasted_iota(jnp.int32, (tm, LANES), 1)
        lo = lane < HEAD_DIM

        def per_head(w):
            return jnp.where(lo, w[:, 2 * c:2 * c + 1], w[:, 2 * c + 1:2 * c + 2])

        ya.append(per_head(w1) * o1_ref[:, sl] + per_head(w2) * o2_ref[:, sl] + per_head(w3) * o3_ref[:, sl])
    ya = jnp.concatenate(ya, axis=1)

    u = u_ref[...]
    before = jnp.where(pos0 > 0, up_ref[...], 0.0)
    after = jnp.where(pos0 + tm < seq, un_ref[...], 0.0)
    ext = jnp.concatenate([before, u, after], axis=0)
    n_ext = tm + 2 * POOL_HALO
    sums = []
    acc = ext
    shift = 1
    for w in POOL_WINDOWS:
        if w == 2:
            acc = pltpu.roll(ext, 1, 0) + ext
        else:
            acc = pltpu.roll(acc, shift, 0) + pltpu.roll(acc, n_ext - shift, 0)
            shift *= 2
        sums.append(acc[POOL_HALO:POOL_HALO + tm])
    pos = pos0 + lax.broadcasted_iota(jnp.int32, (tm, POOL_WIDTH), 0)
    grp = lax.broadcasted_iota(jnp.int32, (tm, POOL_WIDTH), 1) // POOL_GROUP
    mean = jnp.zeros((tm, POOL_WIDTH), F32)
    for g, w in enumerate(POOL_WINDOWS):
        cnt = jnp.minimum(pos + (w - 1 - w // 2), seq - 1) + 1 - jnp.maximum(pos - w // 2, 0)
        mean = jnp.where(grp == g, sums[g] / cnt.astype(F32), mean)
    d = (mean - u).astype(BF16)
    yp = jnp.dot(d, pw_ref[...], preferred_element_type=F32) * ps_ref[...]

    ym = []
    for c in range(MEM_WIDTH // LANES):
        sl = slice(c * LANES, (c + 1) * LANES)
        o, _ = _head_pair_attention(qm_ref[:, sl], kv_ref[0, :, sl],
                                    kv_ref[0, :, MEM_WIDTH + c * LANES:MEM_WIDTH + (c + 1) * LANES], None)
        ym.append(o)
    ym = jnp.concatenate(ym, axis=1)

    gg = gg_ref[...]
    a, p = ATTN_WIDTH, POOL_WIDTH
    y = jnp.concatenate([_rms(ya, gg[:, :a]), _rms(yp, gg[:, a:a + p]), _rms(ym, gg[:, a + p:])], axis=1)
    x1 = x_ref[...] + jnp.dot(y.astype(BF16), wo_ref[...], preferred_element_type=F32)
    x1_ref[...] = x1

    h2 = _rms(x1, n2_ref[...])
    h2_ref[...] = h2
    logits = jnp.dot(h2, rw_ref[...], preferred_element_type=F32,
                     precision=lax.Precision.HIGHEST) + rb_ref[...]
    eidx = lax.broadcasted_iota(jnp.int32, logits.shape, 1)
    kcol = lax.broadcasted_iota(jnp.int32, (tm, TOP_K), 1)
    vals = jnp.zeros((tm, TOP_K), F32)
    idxs = jnp.zeros((tm, TOP_K), jnp.int32)
    work = logits
    for kk in range(TOP_K):
        best = jnp.max(work, axis=-1, keepdims=True)
        arg = jnp.min(jnp.where(work == best, eidx, N_EXPERTS), axis=-1, keepdims=True)
        vals = jnp.where(kcol == kk, best, vals)
        idxs = jnp.where(kcol == kk, arg, idxs)
        work = jnp.where(eidx == arg, -jnp.inf, work)
    ex = jnp.exp(vals - vals[:, 0:1])
    gate_ref[...] = ex / jnp.sum(ex, axis=-1, keepdims=True)
    idx_ref[...] = idxs


def _mix_out(o, lse, u, qm, kv, x2, pw_bd, ps, gg, wo_bf, n2, rw, rb, batch, seq):
    t, d = x2.shape
    tm = 256
    hp = tm // POOL_HALO
    n_halo = t // POOL_HALO
    tiles_per_seq = seq // tm
    row = lambda i: (i, 0)
    fixed = lambda i: (0, 0)
    rowspec = lambda w: pl.BlockSpec((tm, w), row)
    return pl.pallas_call(
        functools.partial(_mix_out_kernel, tm=tm, seq=seq),
        grid=(t // tm,),
        in_specs=[rowspec(ATTN_WIDTH)] * 3 + [rowspec(LANES)] * 3 + [
            pl.BlockSpec((POOL_HALO, POOL_WIDTH), lambda i: (jnp.maximum(i * hp - 1, 0), 0)),
            rowspec(POOL_WIDTH),
            pl.BlockSpec((POOL_HALO, POOL_WIDTH), lambda i: (jnp.minimum((i + 1) * hp, n_halo - 1), 0)),
            rowspec(MEM_WIDTH),
            pl.BlockSpec((1, kv.shape[1], kv.shape[2]), lambda i: (i // tiles_per_seq, 0, 0)),
            rowspec(d),
            pl.BlockSpec(pw_bd.shape, fixed),
            pl.BlockSpec(ps.shape, fixed),
            pl.BlockSpec(gg.shape, fixed),
            pl.BlockSpec(wo_bf.shape, fixed),
            pl.BlockSpec(n2.shape, fixed),
            pl.BlockSpec(rw.shape, fixed),
            pl.BlockSpec(rb.shape, fixed)],
        out_specs=[rowspec(d), rowspec(d), rowspec(TOP_K), rowspec(TOP_K)],
        out_shape=[jax.ShapeDtypeStruct((t, d), F32),
                   jax.ShapeDtypeStruct((t, d), F32),
                   jax.ShapeDtypeStruct((t, TOP_K), jnp.int32),
                   jax.ShapeDtypeStruct((t, TOP_K), F32)],
        compiler_params=_params(("parallel",)),
        name="mix_out_router",
    )(o[0], o[1], o[2], lse[0], lse[1], lse[2], u, u, u, qm, kv, x2, pw_bd, ps, gg, wo_bf, n2, rw, rb)


def _gather_rows_kernel(tok_ref, h_hbm, xs_ref, buf, sem, *, rows):
    base = pl.program_id(0) * rows

    def copy(r):
        return pltpu.make_async_copy(h_hbm.at[pl.ds(tok_ref[base + r], 1), :], buf.at[pl.ds(r, 1), :], sem)

    def start(r, c):
        copy(r).start()
        return c

    def wait(r, c):
        copy(r).wait()
        return c

    lax.fori_loop(0, rows, start, 0, unroll=8)
    lax.fori_loop(0, rows, wait, 0, unroll=8)
    xs_ref[...] = buf[...].astype(BF16)


def _gather_rows(slot_tok, h2):
    n_slots = slot_tok.shape[0]
    d = h2.shape[1]
    rows = MOE_BLOCK
    return pl.pallas_call(
        functools.partial(_gather_rows_kernel, rows=rows),
        grid_spec=pltpu.PrefetchScalarGridSpec(
            num_scalar_prefetch=1,
            grid=(n_slots // rows,),
            in_specs=[pl.BlockSpec(memory_space=pl.ANY)],
            out_specs=pl.BlockSpec((rows, d), lambda b, tok: (b, 0)),
            scratch_shapes=[pltpu.VMEM((rows, d), F32), pltpu.SemaphoreType.DMA(())]),
        out_shape=jax.ShapeDtypeStruct((n_slots, d), BF16),
        compiler_params=_params(("arbitrary",)),
        name="moe_dispatch_gather",
    )(slot_tok, h2)


def _expert_kernel(be_ref, nb_ref, xs_ref, wgu_ref, bgu_ref, wdn_ref, bdn_ref, y_ref, wgu_bf, wdn_bf):
    b = pl.program_id(0)
    e = be_ref[b]
    changed = (b == 0) | (e != be_ref[jnp.maximum(b - 1, 0)])

    @pl.when(changed)
    def _():
        wgu_bf[...] = wgu_ref[0].astype(BF16)
        wdn_bf[...] = wdn_ref[0].astype(BF16)

    @pl.when(b < nb_ref[0])
    def _():
        gu = jnp.dot(xs_ref[...], wgu_bf[...], preferred_element_type=F32) + bgu_ref[0]
        gate = jnp.minimum(gu[:, :D_EXPERT], SWIGLU_LIMIT)
        up = jnp.clip(gu[:, D_EXPERT:], -SWIGLU_LIMIT, SWIGLU_LIMIT)
        act = (up + 1.0) * gate * jax.nn.sigmoid(SWIGLU_ALPHA * gate)
        y_ref[...] = jnp.dot(act.astype(BF16), wdn_bf[...], preferred_element_type=F32) + bdn_ref[0]

    @pl.when(b >= nb_ref[0])
    def _():
        y_ref[...] = jnp.zeros_like(y_ref)


def _experts(block_e, n_used, xs, w_gu, b_gu, w_down, b_down):
    n_slots, d = xs.shape
    n_blocks = n_slots // MOE_BLOCK
    de2 = w_gu.shape[2]
    by_expert = lambda b, be, nb: (be[b], 0, 0)
    by_block = lambda b, be, nb: (b, 0)
    return pl.pallas_call(
        _expert_kernel,
        grid_spec=pltpu.PrefetchScalarGridSpec(
            num_scalar_prefetch=2,
            grid=(n_blocks,),
            in_specs=[pl.BlockSpec((MOE_BLOCK, d), by_block),
                      pl.BlockSpec((1, d, de2), by_expert),
                      pl.BlockSpec((1, 1, de2), by_expert),
                      pl.BlockSpec((1, D_EXPERT, d), by_expert),
                      pl.BlockSpec((1, 1, d), by_expert)],
            out_specs=pl.BlockSpec((MOE_BLOCK, d), by_block),
            scratch_shapes=[pltpu.VMEM((d, de2), BF16), pltpu.VMEM((D_EXPERT, d), BF16)]),
        out_shape=jax.ShapeDtypeStruct((n_slots, d), F32),
        compiler_params=_params(("arbitrary",)),
        name="moe_experts",
    )(block_e, n_used, xs, w_gu, b_gu.reshape(N_EXPERTS, 1, de2), w_down, b_down.reshape(N_EXPERTS, 1, d))


def _combine_kernel(dest_ref, y_hbm, x1_ref, gate_ref, g_ref, out_ref, buf, sem, *, tm, final):
    base = pl.program_id(0) * tm * TOP_K

    def copy(a):
        r = a // TOP_K
        kk = a % TOP_K
        return pltpu.make_async_copy(y_hbm.at[pl.ds(dest_ref[base + a], 1), :],
                                     buf.at[kk, pl.ds(r, 1), :], sem)

    def start(a, c):
        copy(a).start()
        return c

    def wait(a, c):
        copy(a).wait()
        return c

    lax.fori_loop(0, tm * TOP_K, start, 0, unroll=8)
    lax.fori_loop(0, tm * TOP_K, wait, 0, unroll=8)
    gates = gate_ref[...]
    x = x1_ref[...]
    for kk in range(TOP_K):
        x = x + buf[kk] * gates[:, kk:kk + 1]
    out_ref[...] = _rms(x, g_ref[...]) if final else x


def _combine(dest, y_slots, x1, gates, final_g, final):
    t, d = x1.shape
    tm = 128
    return pl.pallas_call(
        functools.partial(_combine_kernel, tm=tm, final=final),
        grid_spec=pltpu.PrefetchScalarGridSpec(
            num_scalar_prefetch=1,
            grid=(t // tm,),
            in_specs=[pl.BlockSpec(memory_space=pl.ANY),
                      pl.BlockSpec((tm, d), lambda i, dest: (i, 0)),
                      pl.BlockSpec((tm, TOP_K), lambda i, dest: (i, 0)),
                      pl.BlockSpec((1, d), lambda i, dest: (0, 0))],
            out_specs=pl.BlockSpec((tm, d), lambda i, dest: (i, 0)),
            scratch_shapes=[pltpu.VMEM((TOP_K, tm, d), F32), pltpu.SemaphoreType.DMA(())]),
        out_shape=jax.ShapeDtypeStruct((t, d), F32),
        compiler_params=_params(("arbitrary",)),
        name="moe_combine",
    )(dest, y_slots, x1, gates, final_g)


def _routing(top_idx):
    t = top_idx.shape[0]
    n_assign = t * TOP_K
    flat_e = top_idx.reshape(-1)
    onehot = (flat_e[:, None] == jnp.arange(N_EXPERTS, dtype=jnp.int32)[None, :]).astype(jnp.int32)
    running = jnp.cumsum(onehot, axis=0)
    counts = running[-1]
    rank = jnp.sum((running - onehot) * onehot, axis=1)
    padded = (counts + MOE_BLOCK - 1) // MOE_BLOCK * MOE_BLOCK
    pad_end = jnp.cumsum(padded)
    pad_start = pad_end - padded
    dest = (pad_start[flat_e] + rank).astype(jnp.int32)
    n_blocks = n_assign // MOE_BLOCK + N_EXPERTS
    flat_tok = jnp.arange(n_assign, dtype=jnp.int32) // TOP_K
    slot_tok = jnp.zeros((n_blocks * MOE_BLOCK,), jnp.int32).at[dest].set(flat_tok)
    n_used = (pad_end[-1] // MOE_BLOCK).astype(jnp.int32)
    blk = jnp.minimum(jnp.arange(n_blocks, dtype=jnp.int32), n_used - 1) * MOE_BLOCK
    block_e = jnp.minimum(jnp.searchsorted(pad_end, blk, side='right'), N_EXPERTS - 1).astype(jnp.int32)
    return dest, slot_tok, block_e, n_used.reshape(1)


def _rope_tables(seq):
    half = HEAD_DIM // 2
    inv_freq = ROPE_THETA ** (-jnp.arange(half, dtype=F32) / half)
    ang = jnp.arange(seq, dtype=F32)[:, None] * inv_freq[None, :]
    cos, sin = jnp.cos(ang), jnp.sin(ang)
    reps = LANES // HEAD_DIM
    cos_l = jnp.tile(jnp.concatenate([cos, cos], axis=1), (1, reps))
    sin_l = jnp.tile(jnp.concatenate([-sin, sin], axis=1), (1, reps))
    return cos_l, sin_l


def kernel(x, mem, norm1_g, w_in, pool_w, pool_scale, mem_norm_g, w_mem_kv, grp_norm_g, w_out, norm2_g,
           router_w, router_b, w_gu, b_gu, w_down, b_down, final_g):
    batch, seq, d = x.shape
    depth = w_in.shape[0]
    n_mem = mem.shape[1]
    cos_l, sin_l = _rope_tables(seq)
    x2 = x.reshape(batch * seq, d)
    mem2 = mem.reshape(batch * n_mem, d)
    row = lambda a: a.reshape(1, -1)
    for l in range(depth):
        q, k, v, u, qm = _in_proj(x2, row(norm1_g[l]), w_in[l].astype(BF16), cos_l, sin_l, seq)
        outs, lses = [], []
        for dil in DILATIONS:
            o, lse = _dilated_attention(q, k, v, batch, seq, dil)
            outs.append(o)
            lses.append(lse)
        kv = _mem_kv(mem2, row(mem_norm_g[l]), w_mem_kv[l].astype(BF16)).reshape(batch, n_mem, 2 * MEM_WIDTH)
        pw_bd = jax.scipy.linalg.block_diag(*[pool_w[l, g] for g in range(len(POOL_WINDOWS))]).astype(BF16)
        x1, h2, top_idx, gates = _mix_out(outs, lses, u, qm, kv, x2, pw_bd, row(pool_scale[l]), row(grp_norm_g[l]),
                                          w_out[l].astype(BF16), row(norm2_g[l]), router_w[l], row(router_b[l]),
                                          batch, seq)
        dest, slot_tok, block_e, n_used = _routing(top_idx)
        xs = _gather_rows(slot_tok, h2)
        y_slots = _experts(block_e, n_used, xs, w_gu[l], b_gu[l], w_down[l], b_down[l])
        x2 = _combine(dest, y_slots, x1, gates, row(final_g), final=(l == depth - 1))
    return x2.reshape(batch, seq, d)
```

```python
import functools

import jax
import jax.numpy as jnp
from jax import lax
from jax.experimental import pallas as pl
from jax.experimental.pallas import tpu as pltpu

D_MODEL = 1024
HEAD_DIM = 64
ATTN_WIDTH = 512
DILATIONS = (1, 4, 16)
BAND = 64
ROPE_THETA = 10000.0
POOL_WINDOWS = (2, 4, 8, 16)
POOL_WIDTH = 256
POOL_GROUP = 64
POOL_HALO = 8
MEM_WIDTH = 256
N_EXPERTS = 32
TOP_K = 4
D_EXPERT = 1024
SWIGLU_ALPHA = 1.702
SWIGLU_LIMIT = 7.0
MOE_BLOCK = 256
NORM_EPS = 1e-5
NEG_INF = -1e30
LANES = 128
SUBLANES = 8
ROW_TILE = D_MODEL // LANES

F32 = jnp.float32
BF16 = jnp.bfloat16
VMEM_LIMIT = 56 * 1024 * 1024

_NT = (((1,), (1,)), ((), ()))


def _params(sem, vmem=VMEM_LIMIT):
    return pltpu.CompilerParams(dimension_semantics=sem, vmem_limit_bytes=vmem)


def _rms(x, g):
    return x * lax.rsqrt(jnp.mean(x * x, axis=-1, keepdims=True) + NORM_EPS) * g


def _to_row_tiles(ref, val):
    m = val.shape[0]
    for s in range(ROW_TILE):
        ref[pl.ds(s, m, stride=ROW_TILE), :] = val[:, s * LANES:(s + 1) * LANES]


def _from_row_tiles(ref, start, m):
    return jnp.concatenate([ref[pl.ds(start * ROW_TILE + s, m, stride=ROW_TILE), :] for s in range(ROW_TILE)],
                           axis=1)


def _in_proj_kernel(x_ref, g_ref, w_ref, cos_ref, sin_ref, q1, k1, v1, q4, k4, v4, q16, k16, v16, u_ref, qm_ref,
                    qkv, *, tm):
    h = _rms(x_ref[...], g_ref[...]).astype(BF16)
    proj = jnp.dot(h, w_ref[...], preferred_element_type=F32)
    cos = cos_ref[...]
    sin = sin_ref[...]
    lane = lax.broadcasted_iota(jnp.int32, cos.shape, 1)
    first_half = (lane % HEAD_DIM) < (HEAD_DIM // 2)
    scale = HEAD_DIM ** -0.5

    def rope(t):
        partner = jnp.where(first_half, pltpu.roll(t, LANES - HEAD_DIM // 2, 1),
                            pltpu.roll(t, HEAD_DIM // 2, 1))
        return t * cos + partner * sin

    a = ATTN_WIDTH
    groups = a // LANES
    for c in range(groups):
        qkv[c] = rope(proj[:, c * LANES:(c + 1) * LANES]) * scale
        qkv[groups + c] = rope(proj[:, a + c * LANES:a + (c + 1) * LANES])
        qkv[2 * groups + c] = proj[:, 2 * a + c * LANES:2 * a + (c + 1) * LANES]
    u_ref[...] = proj[:, 3 * a:3 * a + POOL_WIDTH]
    qm_ref[...] = (proj[:, 3 * a + POOL_WIDTH:] * scale).astype(BF16)

    for dil, refs in ((1, (q1, k1, v1)), (4, (q4, k4, v4)), (16, (q16, k16, v16))):
        for r in range(dil):
            for n, ref in enumerate(refs):
                for c in range(groups):
                    rows = qkv[n * groups + c, pl.ds(r, tm // dil, stride=dil), :] if dil > 1 else qkv[n * groups + c]
                    ref[0, r, :, c * LANES:(c + 1) * LANES] = rows.astype(BF16)


def _in_proj(x2, g, w_bf, cos, sin, batch, seq):
    t, d = x2.shape
    tm = 512
    tiles_per_seq = seq // tm
    cols = w_bf.shape[1]
    row = lambda i: (i, 0)
    fixed = lambda i: (0, 0)
    cls = lambda i: (i // tiles_per_seq, 0, i % tiles_per_seq, 0)
    cls_specs, cls_shapes = [], []
    for dil in DILATIONS:
        for _ in range(3):
            cls_specs.append(pl.BlockSpec((1, dil, tm // dil, ATTN_WIDTH), cls))
            cls_shapes.append(jax.ShapeDtypeStruct((batch, dil, seq // dil, ATTN_WIDTH), BF16))
    outs = pl.pallas_call(
        functools.partial(_in_proj_kernel, tm=tm),
        grid=(t // tm,),
        in_specs=[pl.BlockSpec((tm, d), row),
                  pl.BlockSpec((1, d), fixed),
                  pl.BlockSpec((d, cols), fixed),
                  pl.BlockSpec((tm, LANES), lambda i: (i % tiles_per_seq, 0)),
                  pl.BlockSpec((tm, LANES), lambda i: (i % tiles_per_seq, 0))],
        out_specs=cls_specs + [pl.BlockSpec((tm, POOL_WIDTH), row), pl.BlockSpec((tm, MEM_WIDTH), row)],
        out_shape=cls_shapes + [jax.ShapeDtypeStruct((t, POOL_WIDTH), F32),
                                jax.ShapeDtypeStruct((t, MEM_WIDTH), BF16)],
        scratch_shapes=[pltpu.VMEM((3 * ATTN_WIDTH // LANES, tm, LANES), F32)],
        compiler_params=_params(("parallel",)),
        name="in_proj",
    )(x2, g, w_bf, cos, sin)
    qkv = [outs[3 * n:3 * n + 3] for n in range(len(DILATIONS))]
    return qkv, outs[-2], outs[-1]


def _head_pair_attention(q, k, v, valid):
    lane = lax.broadcasted_iota(jnp.int32, q.shape, 1)
    outs, lses = [], []
    for half in range(2):
        mine = (lane // HEAD_DIM) == half
        s = lax.dot_general(jnp.where(mine, q, jnp.zeros_like(q)), k, _NT,
                            preferred_element_type=F32)
        if valid is not None:
            s = jnp.where(valid, s, NEG_INF)
        m = jnp.max(s, axis=-1, keepdims=True)
        p = jnp.exp(s - m)
        den = jnp.sum(p, axis=-1, keepdims=True)
        pv = jnp.dot(p.astype(BF16), v, preferred_element_type=F32)
        outs.append(pv / den)
        lses.append(m + jnp.log(den))
    return jnp.where((lane // HEAD_DIM) == 0, outs[0], outs[1]), lses


def _attn_kernel(q_ref, kp_ref, kc_ref, kn_ref, vp_ref, vc_ref, vn_ref, o_ref, lse_ref, *, tq, length):
    j = pl.program_id(2)
    nk = tq + 2 * BAND
    row = lax.broadcasted_iota(jnp.int32, (tq, nk), 0)
    col = lax.broadcasted_iota(jnp.int32, (tq, nk), 1)
    key = j * tq - BAND + col
    valid = (jnp.abs(col - BAND - row) <= BAND) & (key >= 0) & (key < length)
    lane = lax.broadcasted_iota(jnp.int32, (tq, LANES), 1)
    lse_tile = jnp.zeros((tq, LANES), F32)
    for c in range(ATTN_WIDTH // LANES):
        sl = slice(c * LANES, (c + 1) * LANES)
        k = jnp.concatenate([kp_ref[0, 0, :, sl], kc_ref[0, 0, :, sl], kn_ref[0, 0, :, sl]], axis=0)
        v = jnp.concatenate([vp_ref[0, 0, :, sl], vc_ref[0, 0, :, sl], vn_ref[0, 0, :, sl]], axis=0)
        o, lses = _head_pair_attention(q_ref[0, 0, :, sl], k, v, valid)
        o_ref[0, 0, :, sl] = o
        for half in range(2):
            lse_tile = jnp.where(lane == 2 * c + half, lses[half], lse_tile)
    lse_ref[0, 0] = lse_tile


def _dilated_attention(q, k, v):
    batch, dilation, length, _ = q.shape
    tq = 128
    per = tq // BAND
    n_band_blocks = length // BAND
    cur = lambda b, r, j: (b, r, j, 0)
    prev = lambda b, r, j: (b, r, jnp.maximum(j * per - 1, 0), 0)
    nxt = lambda b, r, j: (b, r, jnp.minimum((j + 1) * per, n_band_blocks - 1), 0)
    big = lambda imap: pl.BlockSpec((1, 1, tq, ATTN_WIDTH), imap)
    halo = lambda imap: pl.BlockSpec((1, 1, BAND, ATTN_WIDTH), imap)
    return pl.pallas_call(
        functools.partial(_attn_kernel, tq=tq, length=length),
        grid=(batch, dilation, length // tq),
        in_specs=[big(cur), halo(prev), big(cur), halo(nxt), halo(prev), big(cur), halo(nxt)],
        out_specs=[pl.BlockSpec((1, 1, tq, ATTN_WIDTH), cur), pl.BlockSpec((1, 1, tq, LANES), cur)],
        out_shape=[jax.ShapeDtypeStruct((batch, dilation, length, ATTN_WIDTH), F32),
                   jax.ShapeDtypeStruct((batch, dilation, length, LANES), F32)],
        compiler_params=_params(("parallel", "parallel", "parallel")),
        name=f"dilated_attn_d{dilation}",
    )(q, k, k, k, v, v, v)


def _mem_kv_kernel(m_ref, g_ref, w_ref, kv_ref):
    h = _rms(m_ref[...], g_ref[...]).astype(BF16)
    kv_ref[...] = jnp.dot(h, w_ref[...], preferred_element_type=F32).astype(BF16)


def _mem_kv(mem2, g, w_bf):
    n, d = mem2.shape
    cols = w_bf.shape[1]
    tm = 256
    return pl.pallas_call(
        _mem_kv_kernel,
        grid=(n // tm,),
        in_specs=[pl.BlockSpec((tm, d), lambda i: (i, 0)),
                  pl.BlockSpec((1, d), lambda i: (0, 0)),
                  pl.BlockSpec((d, cols), lambda i: (0, 0))],
        out_specs=pl.BlockSpec((tm, cols), lambda i: (i, 0)),
        out_shape=jax.ShapeDtypeStruct((n, cols), BF16),
        compiler_params=_params(("parallel",)),
        name="mem_kv",
    )(mem2, g, w_bf)


def _mix_out_kernel(o1_ref, o4_ref, o16_ref, l1_ref, l4_ref, l16_ref, up_ref, u_ref, un_ref, qm_ref, kv_ref,
                    x_ref, pw_ref, ps_ref, gg_ref, wo_ref, n2_ref, rwh_ref, rwl_ref, rb_ref,
                    x1_ref, h2_ref, idx_ref, gate_ref, rank_ref, cnt_ref,
                    o4_s, o16_s, l4_s, l16_s, carry, *, tm, seq):
    i = pl.program_id(0)
    tiles_per_seq = seq // tm
    pos0 = (i % tiles_per_seq) * tm

    for dil, src, dst in ((4, o4_ref, o4_s), (16, o16_ref, o16_s), (4, l4_ref, l4_s), (16, l16_ref, l16_s)):
        for r in range(dil):
            for c in range(dst.shape[0]):
                dst[c, pl.ds(r, tm // dil, stride=dil), :] = src[0, r, :, c * LANES:(c + 1) * LANES]

    l1, l2, l3 = l1_ref[0, 0], l4_s[0], l16_s[0]
    lm = jnp.maximum(jnp.maximum(l1, l2), l3)
    e1, e2, e3 = jnp.exp(l1 - lm), jnp.exp(l2 - lm), jnp.exp(l3 - lm)
    es = e1 + e2 + e3
    w1, w2, w3 = e1 / es, e2 / es, e3 / es
    ya = []
    for c in range(ATTN_WIDTH // LANES):
        sl = slice(c * LANES, (c + 1) * LANES)
        lane = lax.broadcasted_iota(jnp.int32, (tm, LANES), 1)
        lo = lane < HEAD_DIM

        def per_head(w):
            return jnp.where(lo, w[:, 2 * c:2 * c + 1], w[:, 2 * c + 1:2 * c + 2])

        ya.append(per_head(w1) * o1_ref[0, 0, :, sl] + per_head(w2) * o4_s[c] + per_head(w3) * o16_s[c])
    ya = jnp.concatenate(ya, axis=1)

    u = u_ref[...]
    before = jnp.where(pos0 > 0, up_ref[...], 0.0)
    after = jnp.where(pos0 + tm < seq, un_ref[...], 0.0)
    ext = jnp.concatenate([before, u, after], axis=0)
    n_ext = tm + 2 * POOL_HALO
    sums = []
    acc = ext
    shift = 1
    for w in POOL_WINDOWS:
        if w == 2:
            acc = pltpu.roll(ext, 1, 0) + ext
        else:
            acc = pltpu.roll(acc, shift, 0) + pltpu.roll(acc, n_ext - shift, 0)
            shift *= 2
        sums.append(acc[POOL_HALO:POOL_HALO + tm])
    pos = pos0 + lax.broadcasted_iota(jnp.int32, (tm, POOL_WIDTH), 0)
    grp = lax.broadcasted_iota(jnp.int32, (tm, POOL_WIDTH), 1) // POOL_GROUP
    mean = jnp.zeros((tm, POOL_WIDTH), F32)
    for g, w in enumerate(POOL_WINDOWS):
        cnt = jnp.minimum(pos + (w - 1 - w // 2), seq - 1) + 1 - jnp.maximum(pos - w // 2, 0)
        mean = jnp.where(grp == g, sums[g] / cnt.astype(F32), mean)
    d = (mean - u).astype(BF16)
    yp = jnp.dot(d, pw_ref[...], preferred_element_type=F32) * ps_ref[...]

    ym = []
    for c in range(MEM_WIDTH // LANES):
        sl = slice(c * LANES, (c + 1) * LANES)
        o, _ = _head_pair_attention(qm_ref[:, sl], kv_ref[0, :, sl],
                                    kv_ref[0, :, MEM_WIDTH + c * LANES:MEM_WIDTH + (c + 1) * LANES], None)
        ym.append(o)
    ym = jnp.concatenate(ym, axis=1)

    gg = gg_ref[...]
    a, p = ATTN_WIDTH, POOL_WIDTH
    y = jnp.concatenate([_rms(ya, gg[:, :a]), _rms(yp, gg[:, a:a + p]), _rms(ym, gg[:, a + p:])], axis=1)
    x1 = x_ref[...] + jnp.dot(y.astype(BF16), wo_ref[...], preferred_element_type=F32)
    x1_ref[...] = x1

    h2 = _rms(x1, n2_ref[...])
    _to_row_tiles(h2_ref, h2)

    hi = h2.astype(BF16)
    lo = (h2 - hi.astype(F32)).astype(BF16)
    logits = (lax.dot_general(rwh_ref[...], hi, _NT, preferred_element_type=F32)
              + lax.dot_general(rwh_ref[...], lo, _NT, preferred_element_type=F32)
              + lax.dot_general(rwl_ref[...], hi, _NT, preferred_element_type=F32)) + rb_ref[...]
    eidx = lax.broadcasted_iota(jnp.int32, logits.shape, 0)
    krow = lax.broadcasted_iota(jnp.int32, (TOP_K, tm), 0)
    vals = jnp.zeros((TOP_K, tm), F32)
    idxs = jnp.zeros((TOP_K, tm), jnp.int32)
    work = logits
    args = []
    for kk in range(TOP_K):
        best = jnp.max(work, axis=0, keepdims=True)
        arg = jnp.min(jnp.where(work == best, eidx, N_EXPERTS), axis=0, keepdims=True)
        vals = jnp.where(krow == kk, best, vals)
        idxs = jnp.where(krow == kk, arg, idxs)
        work = jnp.where(eidx == arg, -jnp.inf, work)
        args.append(arg)
    ex = jnp.exp(vals - vals[0:1])
    gate_ref[...] = ex / jnp.sum(ex, axis=0, keepdims=True)
    idx_ref[...] = idxs

    @pl.when(i == 0)
    def _():
        carry[...] = jnp.zeros_like(carry)

    chosen = (work == -jnp.inf).astype(BF16)
    earlier = (lax.broadcasted_iota(jnp.int32, (tm, tm), 0)
               < lax.broadcasted_iota(jnp.int32, (tm, tm), 1)).astype(BF16)
    before_me = jnp.dot(chosen, earlier, preferred_element_type=F32) + carry[...]
    ranks = jnp.zeros((TOP_K, tm), F32)
    for kk in range(TOP_K):
        ranks = jnp.where(krow == kk, jnp.sum(jnp.where(eidx == args[kk], before_me, 0.0), axis=0, keepdims=True),
                          ranks)
    rank_ref[...] = ranks.astype(jnp.int32)
    carry[...] = carry[...] + jnp.sum(chosen.astype(F32), axis=1, keepdims=True)
    cnt_ref[...] = carry[...].astype(jnp.int32)


def _mix_out(attn, u, qm, kv, x2, pw_bd, ps, gg, wo_bf, n2, rw_hi, rw_lo, rb, batch, seq):
    t, d = x2.shape
    tm = 256
    hp = tm // POOL_HALO
    n_halo = t // POOL_HALO
    tiles_per_seq = seq // tm
    row = lambda i: (i, 0)
    fixed = lambda i: (0, 0)
    cls = lambda i: (i // tiles_per_seq, 0, i % tiles_per_seq, 0)
    rowspec = lambda w: pl.BlockSpec((tm, w), row)
    clsspec = lambda dil, w: pl.BlockSpec((1, dil, tm // dil, w), cls)
    tok_cols = lambda i: (0, i)
    (o1, l1), (o4, l4), (o16, l16) = attn
    return pl.pallas_call(
        functools.partial(_mix_out_kernel, tm=tm, seq=seq),
        grid=(t // tm,),
        in_specs=[clsspec(1, ATTN_WIDTH), clsspec(4, ATTN_WIDTH), clsspec(16, ATTN_WIDTH),
                  clsspec(1, LANES), clsspec(4, LANES), clsspec(16, LANES),
                  pl.BlockSpec((POOL_HALO, POOL_WIDTH), lambda i: (jnp.maximum(i * hp - 1, 0), 0)),
                  rowspec(POOL_WIDTH),
                  pl.BlockSpec((POOL_HALO, POOL_WIDTH), lambda i: (jnp.minimum((i + 1) * hp, n_halo - 1), 0)),
                  rowspec(MEM_WIDTH),
                  pl.BlockSpec((1, kv.shape[1], kv.shape[2]), lambda i: (i // tiles_per_seq, 0, 0)),
                  rowspec(d),
                  pl.BlockSpec(pw_bd.shape, fixed),
                  pl.BlockSpec(ps.shape, fixed),
                  pl.BlockSpec(gg.shape, fixed),
                  pl.BlockSpec(wo_bf.shape, fixed),
                  pl.BlockSpec(n2.shape, fixed),
                  pl.BlockSpec(rw_hi.shape, fixed),
                  pl.BlockSpec(rw_lo.shape, fixed),
                  pl.BlockSpec(rb.shape, fixed)],
        out_specs=[rowspec(d),
                   pl.BlockSpec((tm * ROW_TILE, LANES), row),
                   pl.BlockSpec((TOP_K, tm), tok_cols),
                   pl.BlockSpec((TOP_K, tm), tok_cols),
                   pl.BlockSpec((TOP_K, tm), tok_cols),
                   pl.BlockSpec((N_EXPERTS, 1), fixed)],
        out_shape=[jax.ShapeDtypeStruct((t, d), F32),
                   jax.ShapeDtypeStruct((t * ROW_TILE, LANES), F32),
                   jax.ShapeDtypeStruct((TOP_K, t), jnp.int32),
                   jax.ShapeDtypeStruct((TOP_K, t), F32),
                   jax.ShapeDtypeStruct((TOP_K, t), jnp.int32),
                   jax.ShapeDtypeStruct((N_EXPERTS, 1), jnp.int32)],
        scratch_shapes=[pltpu.VMEM((ATTN_WIDTH // LANES, tm, LANES), F32),
                        pltpu.VMEM((ATTN_WIDTH // LANES, tm, LANES), F32),
                        pltpu.VMEM((1, tm, LANES), F32), pltpu.VMEM((1, tm, LANES), F32),
                        pltpu.VMEM((N_EXPERTS, 1), F32)],
        compiler_params=_params(("arbitrary",)),
        name="mix_out_router",
    )(o1, o4, o16, l1, l4, l16, u, u, u, qm, kv, x2, pw_bd, ps, gg, wo_bf, n2, rw_hi, rw_lo, rb)


def _dispatch_kernel(dest_ref, pend_ref, padded_ref, nb_ref, h_hbm, xs_hbm, zeros, sem, *, tm, n_tok, n_blocks):
    i = pl.program_id(0)
    blk = MOE_BLOCK * ROW_TILE

    @pl.when(i == 0)
    def _():
        zeros[...] = jnp.zeros_like(zeros)

        def clear(block):
            return pltpu.make_async_copy(zeros, xs_hbm.at[pl.ds(pl.multiple_of(block * blk, blk), blk), :], sem)

        def for_each_cleared_block(fn):
            for e in range(N_EXPERTS):
                @pl.when(padded_ref[e] > 0)
                def _():
                    fn(clear(pend_ref[e] // MOE_BLOCK - 1))

                @pl.when(nb_ref[0] + e < n_blocks)
                def _():
                    fn(clear(nb_ref[0] + e))

        for_each_cleared_block(lambda c: c.start())
        for_each_cleared_block(lambda c: c.wait())

    def start(r, c):
        tok = i * tm + r
        src = h_hbm.at[pl.ds(pl.multiple_of(tok * ROW_TILE, ROW_TILE), ROW_TILE), :]
        for kk in range(TOP_K):
            slot = dest_ref[kk * n_tok + tok]
            pltpu.make_async_copy(src, xs_hbm.at[pl.ds(pl.multiple_of(slot * ROW_TILE, ROW_TILE), ROW_TILE), :],
                                  sem).start()
        return c

    lax.fori_loop(0, tm, start, 0, unroll=4)
    n = tm * TOP_K * ROW_TILE
    pltpu.make_async_copy(h_hbm.at[pl.ds(0, n), :], xs_hbm.at[pl.ds(0, n), :], sem).wait()


def _dispatch(dest, pad_end, padded, n_used, h2_tiles, n_slots):
    n_tok = h2_tiles.shape[0] // ROW_TILE
    tm = 512
    return pl.pallas_call(
        functools.partial(_dispatch_kernel, tm=tm, n_tok=n_tok, n_blocks=n_slots // MOE_BLOCK),
        grid_spec=pltpu.PrefetchScalarGridSpec(
            num_scalar_prefetch=4,
            grid=(n_tok // tm,),
            in_specs=[pl.BlockSpec(memory_space=pl.ANY)],
            out_specs=pl.BlockSpec(memory_space=pl.ANY),
            scratch_shapes=[pltpu.VMEM((MOE_BLOCK * ROW_TILE, LANES), F32), pltpu.SemaphoreType.DMA(())]),
        out_shape=jax.ShapeDtypeStruct((n_slots * ROW_TILE, LANES), F32),
        compiler_params=_params(("arbitrary",)),
        name="moe_dispatch",
    )(dest, pad_end, padded, n_used, h2_tiles)


def _expert_kernel(be_ref, nb_ref, xs_ref, wgu_ref, bgu_ref, wdn_ref, bdn_ref, y_ref, wgu_bf, wdn_bf):
    b = pl.program_id(0)
    e = be_ref[b]
    changed = (b == 0) | (e != be_ref[jnp.maximum(b - 1, 0)])

    @pl.when(changed)
    def _():
        wgu_bf[...] = wgu_ref[0, 0].astype(BF16)
        wdn_bf[...] = wdn_ref[0, 0].astype(BF16)

    @pl.when(b < nb_ref[0])
    def _():
        x = _from_row_tiles(xs_ref, 0, MOE_BLOCK).astype(BF16)
        gu = jnp.dot(x, wgu_bf[...], preferred_element_type=F32) + bgu_ref[0, 0]
        gate = jnp.minimum(gu[:, :D_EXPERT], SWIGLU_LIMIT)
        up = jnp.clip(gu[:, D_EXPERT:], -SWIGLU_LIMIT, SWIGLU_LIMIT)
        act = (up + 1.0) * gate * jax.nn.sigmoid(SWIGLU_ALPHA * gate)
        y = jnp.dot(act.astype(BF16), wdn_bf[...], preferred_element_type=F32) + bdn_ref[0, 0]
        _to_row_tiles(y_ref, y)

    @pl.when(b >= nb_ref[0])
    def _():
        y_ref[...] = jnp.zeros_like(y_ref)


def _experts(layer, block_e, n_used, xs_tiles, w_gu, b_gu, w_down, b_down):
    n_slots = xs_tiles.shape[0] // ROW_TILE
    n_blocks = n_slots // MOE_BLOCK
    d = w_gu.shape[2]
    de2 = w_gu.shape[3]
    by_expert = lambda b, be, nb: (layer, be[b], 0, 0)
    by_block = lambda b, be, nb: (b, 0)
    used_block = lambda b, be, nb: (jnp.minimum(b, nb[0] - 1), 0)
    return pl.pallas_call(
        _expert_kernel,
        grid_spec=pltpu.PrefetchScalarGridSpec(
            num_scalar_prefetch=2,
            grid=(n_blocks,),
            in_specs=[pl.BlockSpec((MOE_BLOCK * ROW_TILE, LANES), used_block),
                      pl.BlockSpec((1, 1, d, de2), by_expert),
                      pl.BlockSpec((1, 1, 1, de2), by_expert),
                      pl.BlockSpec((1, 1, D_EXPERT, d), by_expert),
                      pl.BlockSpec((1, 1, 1, d), by_expert)],
            out_specs=pl.BlockSpec((MOE_BLOCK * ROW_TILE, LANES), by_block),
            scratch_shapes=[pltpu.VMEM((d, de2), BF16), pltpu.VMEM((D_EXPERT, d), BF16)]),
        out_shape=jax.ShapeDtypeStruct((n_slots * ROW_TILE, LANES), F32),
        compiler_params=_params(("arbitrary",)),
        name="moe_experts",
    )(block_e, n_used, xs_tiles, w_gu, b_gu.reshape(b_gu.shape[0], N_EXPERTS, 1, de2), w_down,
      b_down.reshape(b_down.shape[0], N_EXPERTS, 1, d))


def _combine_kernel(dest_ref, y_hbm, x1_ref, gate_ref, g_ref, out_ref, buf, sem, *, tm, n_tok, final):
    i = pl.program_id(0)

    def start(r, c):
        tok = i * tm + r
        for kk in range(TOP_K):
            slot = dest_ref[kk * n_tok + tok]
            pltpu.make_async_copy(y_hbm.at[pl.ds(pl.multiple_of(slot * ROW_TILE, ROW_TILE), ROW_TILE), :],
                                  buf.at[pl.ds(pl.multiple_of((kk * tm + r) * ROW_TILE, ROW_TILE), ROW_TILE), :],
                                  sem).start()
        return c

    lax.fori_loop(0, tm, start, 0, unroll=4)
    pltpu.make_async_copy(y_hbm.at[pl.ds(0, TOP_K * tm * ROW_TILE), :], buf, sem).wait()
    gates = gate_ref[...]
    x = x1_ref[...]
    for kk in range(TOP_K):
        x = x + _from_row_tiles(buf, kk * tm, tm) * gates[:, kk:kk + 1]
    out_ref[...] = _rms(x, g_ref[...]) if final else x


def _combine(dest, y_tiles, x1, gates_tk, final_g, final):
    t, d = x1.shape
    tm = 256
    return pl.pallas_call(
        functools.partial(_combine_kernel, tm=tm, n_tok=t, final=final),
        grid_spec=pltpu.PrefetchScalarGridSpec(
            num_scalar_prefetch=1,
            grid=(t // tm,),
            in_specs=[pl.BlockSpec(memory_space=pl.ANY),
                      pl.BlockSpec((tm, d), lambda i, dest: (i, 0)),
                      pl.BlockSpec((tm, TOP_K), lambda i, dest: (i, 0)),
                      pl.BlockSpec((1, d), lambda i, dest: (0, 0))],
            out_specs=pl.BlockSpec((tm, d), lambda i, dest: (i, 0)),
            scratch_shapes=[pltpu.VMEM((TOP_K * tm * ROW_TILE, LANES), F32), pltpu.SemaphoreType.DMA(())]),
        out_shape=jax.ShapeDtypeStruct((t, d), F32),
        compiler_params=_params(("arbitrary",)),
        name="moe_combine",
    )(dest, y_tiles, x1, gates_tk, final_g)


def _slot_layout(top_idx, rank, counts, n_blocks):
    counts = counts.reshape(-1)
    padded = (counts + MOE_BLOCK - 1) // MOE_BLOCK * MOE_BLOCK
    pad_end = jnp.cumsum(padded).astype(jnp.int32)
    pad_start = pad_end - padded
    experts = jnp.arange(N_EXPERTS, dtype=jnp.int32)
    start_of = jnp.sum(jnp.where(top_idx[..., None] == experts, pad_start, 0), axis=-1)
    dest = (start_of + rank).reshape(-1).astype(jnp.int32)
    n_used = pad_end[-1] // MOE_BLOCK
    blk = jnp.minimum(jnp.arange(n_blocks, dtype=jnp.int32), n_used - 1) * MOE_BLOCK
    block_e = jnp.minimum(jnp.sum((blk[:, None] >= pad_end[None, :]).astype(jnp.int32), axis=1), N_EXPERTS - 1)
    return dest, pad_end, padded.astype(jnp.int32), block_e.astype(jnp.int32), n_used.reshape(1).astype(jnp.int32)


def _rope_tables(seq):
    half = HEAD_DIM // 2
    inv_freq = ROPE_THETA ** (-jnp.arange(half, dtype=F32) / half)
    ang = jnp.arange(seq, dtype=F32)[:, None] * inv_freq[None, :]
    cos, sin = jnp.cos(ang), jnp.sin(ang)
    reps = LANES // HEAD_DIM
    cos_l = jnp.tile(jnp.concatenate([cos, cos], axis=1), (1, reps))
    sin_l = jnp.tile(jnp.concatenate([-sin, sin], axis=1), (1, reps))
    return cos_l, sin_l


def kernel(x, mem, norm1_g, w_in, pool_w, pool_scale, mem_norm_g, w_mem_kv, grp_norm_g, w_out, norm2_g,
           router_w, router_b, w_gu, b_gu, w_down, b_down, final_g):
    batch, seq, d = x.shape
    depth = w_in.shape[0]
    n_mem = mem.shape[1]
    t = batch * seq
    n_blocks = t * TOP_K // MOE_BLOCK + N_EXPERTS
    cos_l, sin_l = _rope_tables(seq)
    x2 = x.reshape(t, d)
    mem2 = mem.reshape(batch * n_mem, d)
    row = lambda a: a.reshape(1, -1)
    for l in range(depth):
        qkv, u, qm = _in_proj(x2, row(norm1_g[l]), w_in[l].astype(BF16), cos_l, sin_l, batch, seq)
        attn = [_dilated_attention(*qkv[n]) for n in range(len(DILATIONS))]
        kv = _mem_kv(mem2, row(mem_norm_g[l]), w_mem_kv[l].astype(BF16)).reshape(batch, n_mem, 2 * MEM_WIDTH)
        pw_bd = jax.scipy.linalg.block_diag(*[pool_w[l, g] for g in range(len(POOL_WINDOWS))]).astype(BF16)
        rw_t = router_w[l].T
        rw_hi = rw_t.astype(BF16)
        rw_lo = (rw_t - rw_hi.astype(F32)).astype(BF16)
        x1, h2_tiles, top_idx, gates, rank, counts = _mix_out(
            attn, u, qm, kv, x2, pw_bd, row(pool_scale[l]), row(grp_norm_g[l]), w_out[l].astype(BF16),
            row(norm2_g[l]), rw_hi, rw_lo, router_b[l].reshape(-1, 1), batch, seq)
        dest, pad_end, padded, block_e, n_used = _slot_layout(top_idx, rank, counts, n_blocks)
        xs_tiles = _dispatch(dest, pad_end, padded, n_used, h2_tiles, n_blocks * MOE_BLOCK)
        y_tiles = _experts(l, block_e, n_used, xs_tiles, w_gu, b_gu, w_down, b_down)
        x2 = _combine(dest, y_tiles, x1, gates.T, row(final_g), final=(l == depth - 1))
    return x2.reshape(batch, seq, d)
```

```python
import functools

import jax
import jax.numpy as jnp
from jax import lax
from jax.experimental import pallas as pl
from jax.experimental.pallas import tpu as pltpu

D_MODEL = 1024
HEAD_DIM = 64
ATTN_WIDTH = 512
DILATIONS = (1, 4, 16)
BAND = 64
ROPE_THETA = 10000.0
POOL_WINDOWS = (2, 4, 8, 16)
POOL_WIDTH = 256
POOL_GROUP = 64
POOL_HALO = 8
MEM_WIDTH = 256
N_EXPERTS = 32
TOP_K = 4
D_EXPERT = 1024
SWIGLU_ALPHA = 1.702
SWIGLU_LIMIT = 7.0
MOE_BLOCK = 256
NORM_EPS = 1e-5
NEG_INF = -1e30
LANES = 128
SUBLANES = 8
ROW_TILE = D_MODEL // LANES

F32 = jnp.float32
BF16 = jnp.bfloat16
VMEM_LIMIT = 56 * 1024 * 1024

_NT = (((1,), (1,)), ((), ()))


def _params(sem, vmem=VMEM_LIMIT):
    return pltpu.CompilerParams(dimension_semantics=sem, vmem_limit_bytes=vmem)


def _rms(x, g):
    return x * lax.rsqrt(jnp.mean(x * x, axis=-1, keepdims=True) + NORM_EPS) * g


def _to_row_tiles(ref, val):
    m = val.shape[0]
    for s in range(ROW_TILE):
        ref[pl.ds(s, m, stride=ROW_TILE), :] = val[:, s * LANES:(s + 1) * LANES]


def _from_row_tiles(ref, start, m):
    return jnp.concatenate([ref[pl.ds(start * ROW_TILE + s, m, stride=ROW_TILE), :] for s in range(ROW_TILE)],
                           axis=1)


def _in_proj_kernel(x_ref, g_ref, w_ref, cos_ref, sin_ref, q1, k1, v1, q4, k4, v4, q16, k16, v16, u_ref, qm_ref,
                    qkv, *, tm):
    h = _rms(x_ref[...], g_ref[...]).astype(BF16)
    proj = jnp.dot(h, w_ref[...], preferred_element_type=F32)
    cos = cos_ref[...]
    sin = sin_ref[...]
    lane = lax.broadcasted_iota(jnp.int32, cos.shape, 1)
    first_half = (lane % HEAD_DIM) < (HEAD_DIM // 2)
    scale = HEAD_DIM ** -0.5

    def rope(t):
        partner = jnp.where(first_half, pltpu.roll(t, LANES - HEAD_DIM // 2, 1),
                            pltpu.roll(t, HEAD_DIM // 2, 1))
        return t * cos + partner * sin

    a = ATTN_WIDTH
    groups = a // LANES
    for c in range(groups):
        qkv[c] = rope(proj[:, c * LANES:(c + 1) * LANES]) * scale
        qkv[groups + c] = rope(proj[:, a + c * LANES:a + (c + 1) * LANES])
        qkv[2 * groups + c] = proj[:, 2 * a + c * LANES:2 * a + (c + 1) * LANES]
    u_ref[...] = proj[:, 3 * a:3 * a + POOL_WIDTH]
    qm_ref[...] = (proj[:, 3 * a + POOL_WIDTH:] * scale).astype(BF16)

    for dil, refs in ((1, (q1, k1, v1)), (4, (q4, k4, v4)), (16, (q16, k16, v16))):
        for r in range(dil):
            for n, ref in enumerate(refs):
                for c in range(groups):
                    rows = qkv[n * groups + c, pl.ds(r, tm // dil, stride=dil), :] if dil > 1 else qkv[n * groups + c]
                    ref[0, r, :, c * LANES:(c + 1) * LANES] = rows.astype(BF16)


def _in_proj(x2, g, w_bf, cos, sin, batch, seq):
    t, d = x2.shape
    tm = 512
    tiles_per_seq = seq // tm
    cols = w_bf.shape[1]
    row = lambda i: (i, 0)
    fixed = lambda i: (0, 0)
    cls = lambda i: (i // tiles_per_seq, 0, i % tiles_per_seq, 0)
    cls_specs, cls_shapes = [], []
    for dil in DILATIONS:
        for _ in range(3):
            cls_specs.append(pl.BlockSpec((1, dil, tm // dil, ATTN_WIDTH), cls))
            cls_shapes.append(jax.ShapeDtypeStruct((batch, dil, seq // dil, ATTN_WIDTH), BF16))
    outs = pl.pallas_call(
        functools.partial(_in_proj_kernel, tm=tm),
        grid=(t // tm,),
        in_specs=[pl.BlockSpec((tm, d), row),
                  pl.BlockSpec((1, d), fixed),
                  pl.BlockSpec((d, cols), fixed),
                  pl.BlockSpec((tm, LANES), lambda i: (i % tiles_per_seq, 0)),
                  pl.BlockSpec((tm, LANES), lambda i: (i % tiles_per_seq, 0))],
        out_specs=cls_specs + [pl.BlockSpec((tm, POOL_WIDTH), row), pl.BlockSpec((tm, MEM_WIDTH), row)],
        out_shape=cls_shapes + [jax.ShapeDtypeStruct((t, POOL_WIDTH), F32),
                                jax.ShapeDtypeStruct((t, MEM_WIDTH), BF16)],
        scratch_shapes=[pltpu.VMEM((3 * ATTN_WIDTH // LANES, tm, LANES), F32)],
        compiler_params=_params(("parallel",)),
        name="in_proj",
    )(x2, g, w_bf, cos, sin)
    qkv = [outs[3 * n:3 * n + 3] for n in range(len(DILATIONS))]
    return qkv, outs[-2], outs[-1]


def _head_pair_attention(q, k, v, valid):
    lane = lax.broadcasted_iota(jnp.int32, q.shape, 1)
    outs, lses = [], []
    for half in range(2):
        mine = (lane // HEAD_DIM) == half
        s = lax.dot_general(jnp.where(mine, q, jnp.zeros_like(q)), k, _NT,
                            preferred_element_type=F32)
        if valid is not None:
            s = jnp.where(valid, s, NEG_INF)
        m = jnp.max(s, axis=-1, keepdims=True)
        p = jnp.exp(s - m)
        den = jnp.sum(p, axis=-1, keepdims=True)
        pv = jnp.dot(p.astype(BF16), v, preferred_element_type=F32)
        outs.append(pv / den)
        lses.append(m + jnp.log(den))
    return jnp.where((lane // HEAD_DIM) == 0, outs[0], outs[1]), lses


def _attn_kernel(q_ref, kp_ref, kc_ref, kn_ref, vp_ref, vc_ref, vn_ref, o_ref, lse_ref, *, tq, sub, length):
    j = pl.program_id(2)
    nk = sub + 2 * BAND
    row = lax.broadcasted_iota(jnp.int32, (sub, nk), 0)
    col = lax.broadcasted_iota(jnp.int32, (sub, nk), 1)
    in_band = jnp.abs(col - BAND - row) <= BAND
    lane = lax.broadcasted_iota(jnp.int32, (sub, LANES), 1)
    for c in range(ATTN_WIDTH // LANES):
        sl = slice(c * LANES, (c + 1) * LANES)
        k = jnp.concatenate([kp_ref[0, 0, :, sl], kc_ref[0, 0, :, sl], kn_ref[0, 0, :, sl]], axis=0)
        v = jnp.concatenate([vp_ref[0, 0, :, sl], vc_ref[0, 0, :, sl], vn_ref[0, 0, :, sl]], axis=0)
        for s in range(tq // sub):
            rows = slice(s * sub, (s + 1) * sub)
            key = j * tq + s * sub - BAND + col
            valid = in_band & (key >= 0) & (key < length)
            o, lses = _head_pair_attention(q_ref[0, 0, rows, sl], k[s * sub:s * sub + nk], v[s * sub:s * sub + nk],
                                           valid)
            o_ref[0, 0, rows, sl] = o
            lse_tile = jnp.where(lane == 2 * c, lses[0], lses[1])
            if c == 0:
                lse_ref[0, 0, rows, :] = lse_tile
            else:
                lse_ref[0, 0, rows, :] = jnp.where((lane // 2) == c, lse_tile, lse_ref[0, 0, rows, :])


def _dilated_attention(q, k, v):
    batch, dilation, length, _ = q.shape
    tq = 512
    per = tq // BAND
    n_band_blocks = length // BAND
    cur = lambda b, r, j: (b, r, j, 0)
    prev = lambda b, r, j: (b, r, jnp.maximum(j * per - 1, 0), 0)
    nxt = lambda b, r, j: (b, r, jnp.minimum((j + 1) * per, n_band_blocks - 1), 0)
    big = lambda imap: pl.BlockSpec((1, 1, tq, ATTN_WIDTH), imap)
    halo = lambda imap: pl.BlockSpec((1, 1, BAND, ATTN_WIDTH), imap)
    return pl.pallas_call(
        functools.partial(_attn_kernel, tq=tq, sub=128, length=length),
        grid=(batch, dilation, length // tq),
        in_specs=[big(cur), halo(prev), big(cur), halo(nxt), halo(prev), big(cur), halo(nxt)],
        out_specs=[pl.BlockSpec((1, 1, tq, ATTN_WIDTH), cur), pl.BlockSpec((1, 1, tq, LANES), cur)],
        out_shape=[jax.ShapeDtypeStruct((batch, dilation, length, ATTN_WIDTH), F32),
                   jax.ShapeDtypeStruct((batch, dilation, length, LANES), F32)],
        compiler_params=_params(("parallel", "parallel", "parallel")),
        name=f"dilated_attn_d{dilation}",
    )(q, k, k, k, v, v, v)


def _mem_kv_kernel(m_ref, g_ref, w_ref, kv_ref):
    h = _rms(m_ref[...], g_ref[...]).astype(BF16)
    kv_ref[...] = jnp.dot(h, w_ref[...], preferred_element_type=F32).astype(BF16)


def _mem_kv(mem2, g, w_bf):
    n, d = mem2.shape
    cols = w_bf.shape[1]
    tm = 256
    return pl.pallas_call(
        _mem_kv_kernel,
        grid=(n // tm,),
        in_specs=[pl.BlockSpec((tm, d), lambda i: (i, 0)),
                  pl.BlockSpec((1, d), lambda i: (0, 0)),
                  pl.BlockSpec((d, cols), lambda i: (0, 0))],
        out_specs=pl.BlockSpec((tm, cols), lambda i: (i, 0)),
        out_shape=jax.ShapeDtypeStruct((n, cols), BF16),
        compiler_params=_params(("parallel",)),
        name="mem_kv",
    )(mem2, g, w_bf)


def _mix_out_kernel(o1_ref, o4_ref, o16_ref, l1_ref, l4_ref, l16_ref, up_ref, u_ref, un_ref, qm_ref, kv_ref,
                    x_ref, pw_ref, ps_ref, gg_ref, wo_ref, n2_ref, rwh_ref, rwl_ref, rb_ref,
                    x1_ref, h2_ref, idx_ref, gate_ref, rank_ref, cnt_ref, base_ref,
                    o4_s, o16_s, l4_s, l16_s, carry, *, tm, seq):
    i = pl.program_id(0)
    tiles_per_seq = seq // tm
    pos0 = (i % tiles_per_seq) * tm

    for dil, src, dst in ((4, o4_ref, o4_s), (16, o16_ref, o16_s), (4, l4_ref, l4_s), (16, l16_ref, l16_s)):
        for r in range(dil):
            for c in range(dst.shape[0]):
                dst[c, pl.ds(r, tm // dil, stride=dil), :] = src[0, r, :, c * LANES:(c + 1) * LANES]

    l1, l2, l3 = l1_ref[0, 0], l4_s[0], l16_s[0]
    lm = jnp.maximum(jnp.maximum(l1, l2), l3)
    e1, e2, e3 = jnp.exp(l1 - lm), jnp.exp(l2 - lm), jnp.exp(l3 - lm)
    es = e1 + e2 + e3
    w1, w2, w3 = e1 / es, e2 / es, e3 / es
    ya = []
    for c in range(ATTN_WIDTH // LANES):
        sl = slice(c * LANES, (c + 1) * LANES)
        lane = lax.broadcasted_iota(jnp.int32, (tm, LANES), 1)
        lo = lane < HEAD_DIM

        def per_head(w):
            return jnp.where(lo, w[:, 2 * c:2 * c + 1], w[:, 2 * c + 1:2 * c + 2])

        ya.append(per_head(w1) * o1_ref[0, 0, :, sl] + per_head(w2) * o4_s[c] + per_head(w3) * o16_s[c])
    ya = jnp.concatenate(ya, axis=1)

    u = u_ref[...]
    before = jnp.where(pos0 > 0, up_ref[...], 0.0)
    after = jnp.where(pos0 + tm < seq, un_ref[...], 0.0)
    ext = jnp.concatenate([before, u, after], axis=0)
    n_ext = tm + 2 * POOL_HALO
    sums = []
    acc = ext
    shift = 1
    for w in POOL_WINDOWS:
        if w == 2:
            acc = pltpu.roll(ext, 1, 0) + ext
        else:
            acc = pltpu.roll(acc, shift, 0) + pltpu.roll(acc, n_ext - shift, 0)
            shift *= 2
        sums.append(acc[POOL_HALO:POOL_HALO + tm])
    pos = pos0 + lax.broadcasted_iota(jnp.int32, (tm, POOL_WIDTH), 0)
    grp = lax.broadcasted_iota(jnp.int32, (tm, POOL_WIDTH), 1) // POOL_GROUP
    mean = jnp.zeros((tm, POOL_WIDTH), F32)
    for g, w in enumerate(POOL_WINDOWS):
        cnt = jnp.minimum(pos + (w - 1 - w // 2), seq - 1) + 1 - jnp.maximum(pos - w // 2, 0)
        mean = jnp.where(grp == g, sums[g] / cnt.astype(F32), mean)
    d = (mean - u).astype(BF16)
    yp = jnp.dot(d, pw_ref[...], preferred_element_type=F32) * ps_ref[...]

    ym = []
    for c in range(MEM_WIDTH // LANES):
        sl = slice(c * LANES, (c + 1) * LANES)
        o, _ = _head_pair_attention(qm_ref[:, sl], kv_ref[0, :, sl],
                                    kv_ref[0, :, MEM_WIDTH + c * LANES:MEM_WIDTH + (c + 1) * LANES], None)
        ym.append(o)
    ym = jnp.concatenate(ym, axis=1)

    gg = gg_ref[...]
    a, p = ATTN_WIDTH, POOL_WIDTH
    y = jnp.concatenate([_rms(ya, gg[:, :a]), _rms(yp, gg[:, a:a + p]), _rms(ym, gg[:, a + p:])], axis=1)
    x1 = x_ref[...] + jnp.dot(y.astype(BF16), wo_ref[...], preferred_element_type=F32)
    x1_ref[...] = x1

    h2 = _rms(x1, n2_ref[...])
    _to_row_tiles(h2_ref, h2)

    hi = h2.astype(BF16)
    lo = (h2 - hi.astype(F32)).astype(BF16)
    logits = (lax.dot_general(rwh_ref[...], hi, _NT, preferred_element_type=F32)
              + lax.dot_general(rwh_ref[...], lo, _NT, preferred_element_type=F32)
              + lax.dot_general(rwl_ref[...], hi, _NT, preferred_element_type=F32)) + rb_ref[...]
    eidx = lax.broadcasted_iota(jnp.int32, logits.shape, 0)
    krow = lax.broadcasted_iota(jnp.int32, (TOP_K, tm), 0)
    vals = jnp.zeros((TOP_K, tm), F32)
    idxs = jnp.zeros((TOP_K, tm), jnp.int32)
    work = logits
    args = []
    for kk in range(TOP_K):
        best = jnp.max(work, axis=0, keepdims=True)
        arg = jnp.min(jnp.where(work == best, eidx, N_EXPERTS), axis=0, keepdims=True)
        vals = jnp.where(krow == kk, best, vals)
        idxs = jnp.where(krow == kk, arg, idxs)
        work = jnp.where(eidx == arg, -jnp.inf, work)
        args.append(arg)
    ex = jnp.exp(vals - vals[0:1])
    gate_ref[...] = ex / jnp.sum(ex, axis=0, keepdims=True)
    idx_ref[...] = idxs

    @pl.when(i == 0)
    def _():
        carry[...] = jnp.zeros_like(carry)

    chosen = (work == -jnp.inf).astype(BF16)
    earlier = (lax.broadcasted_iota(jnp.int32, (tm, tm), 0)
               < lax.broadcasted_iota(jnp.int32, (tm, tm), 1)).astype(BF16)
    before_me = jnp.dot(chosen, earlier, preferred_element_type=F32) + carry[...]
    ranks = jnp.zeros((TOP_K, tm), F32)
    for kk in range(TOP_K):
        ranks = jnp.where(krow == kk, jnp.sum(jnp.where(eidx == args[kk], before_me, 0.0), axis=0, keepdims=True),
                          ranks)
    rank_ref[...] = ranks.astype(jnp.int32)
    base_ref[0] = carry[...].astype(jnp.int32)
    carry[...] = carry[...] + jnp.sum(chosen.astype(F32), axis=1, keepdims=True)
    cnt_ref[...] = carry[...].astype(jnp.int32)


def _mix_out(attn, u, qm, kv, x2, pw_bd, ps, gg, wo_bf, n2, rw_hi, rw_lo, rb, batch, seq):
    t, d = x2.shape
    tm = 256
    hp = tm // POOL_HALO
    n_halo = t // POOL_HALO
    tiles_per_seq = seq // tm
    row = lambda i: (i, 0)
    fixed = lambda i: (0, 0)
    cls = lambda i: (i // tiles_per_seq, 0, i % tiles_per_seq, 0)
    rowspec = lambda w: pl.BlockSpec((tm, w), row)
    clsspec = lambda dil, w: pl.BlockSpec((1, dil, tm // dil, w), cls)
    tok_cols = lambda i: (0, i)
    (o1, l1), (o4, l4), (o16, l16) = attn
    return pl.pallas_call(
        functools.partial(_mix_out_kernel, tm=tm, seq=seq),
        grid=(t // tm,),
        in_specs=[clsspec(1, ATTN_WIDTH), clsspec(4, ATTN_WIDTH), clsspec(16, ATTN_WIDTH),
                  clsspec(1, LANES), clsspec(4, LANES), clsspec(16, LANES),
                  pl.BlockSpec((POOL_HALO, POOL_WIDTH), lambda i: (jnp.maximum(i * hp - 1, 0), 0)),
                  rowspec(POOL_WIDTH),
                  pl.BlockSpec((POOL_HALO, POOL_WIDTH), lambda i: (jnp.minimum((i + 1) * hp, n_halo - 1), 0)),
                  rowspec(MEM_WIDTH),
                  pl.BlockSpec((1, kv.shape[1], kv.shape[2]), lambda i: (i // tiles_per_seq, 0, 0)),
                  rowspec(d),
                  pl.BlockSpec(pw_bd.shape, fixed),
                  pl.BlockSpec(ps.shape, fixed),
                  pl.BlockSpec(gg.shape, fixed),
                  pl.BlockSpec(wo_bf.shape, fixed),
                  pl.BlockSpec(n2.shape, fixed),
                  pl.BlockSpec(rw_hi.shape, fixed),
                  pl.BlockSpec(rw_lo.shape, fixed),
                  pl.BlockSpec(rb.shape, fixed)],
        out_specs=[rowspec(d),
                   pl.BlockSpec((tm * ROW_TILE, LANES), row),
                   pl.BlockSpec((TOP_K, tm), tok_cols),
                   pl.BlockSpec((TOP_K, tm), tok_cols),
                   pl.BlockSpec((TOP_K, tm), tok_cols),
                   pl.BlockSpec((N_EXPERTS, 1), fixed),
                   pl.BlockSpec((1, N_EXPERTS, 1), lambda i: (i, 0, 0))],
        out_shape=[jax.ShapeDtypeStruct((t, d), F32),
                   jax.ShapeDtypeStruct((t * ROW_TILE, LANES), F32),
                   jax.ShapeDtypeStruct((TOP_K, t), jnp.int32),
                   jax.ShapeDtypeStruct((TOP_K, t), F32),
                   jax.ShapeDtypeStruct((TOP_K, t), jnp.int32),
                   jax.ShapeDtypeStruct((N_EXPERTS, 1), jnp.int32),
                   jax.ShapeDtypeStruct((t // tm, N_EXPERTS, 1), jnp.int32)],
        scratch_shapes=[pltpu.VMEM((ATTN_WIDTH // LANES, tm, LANES), F32),
                        pltpu.VMEM((ATTN_WIDTH // LANES, tm, LANES), F32),
                        pltpu.VMEM((1, tm, LANES), F32), pltpu.VMEM((1, tm, LANES), F32),
                        pltpu.VMEM((N_EXPERTS, 1), F32)],
        compiler_params=_params(("arbitrary",)),
        name="mix_out_router",
    )(o1, o4, o16, l1, l4, l16, u, u, u, qm, kv, x2, pw_bd, ps, gg, wo_bf, n2, rw_hi, rw_lo, rb)


def _dispatch_kernel(dest_ref, pend_ref, padded_ref, nb_ref, h_ref, xs_hbm, zeros, sem, *, tm, n_tok, n_blocks):
    i = pl.program_id(0)
    blk = MOE_BLOCK * ROW_TILE

    @pl.when(i == 0)
    def _():
        zeros[...] = jnp.zeros_like(zeros)

        def clear(block):
            return pltpu.make_async_copy(zeros, xs_hbm.at[pl.ds(pl.multiple_of(block * blk, blk), blk), :], sem)

        def for_each_cleared_block(fn):
            for e in range(N_EXPERTS):
                @pl.when(padded_ref[e] > 0)
                def _():
                    fn(clear(pend_ref[e] // MOE_BLOCK - 1))

                @pl.when(nb_ref[0] + e < n_blocks)
                def _():
                    fn(clear(nb_ref[0] + e))

        for_each_cleared_block(lambda c: c.start())
        for_each_cleared_block(lambda c: c.wait())

    def start(r, c):
        src = h_ref.at[pl.ds(pl.multiple_of(r * ROW_TILE, ROW_TILE), ROW_TILE), :]
        for kk in range(TOP_K):
            slot = dest_ref[kk * n_tok + i * tm + r]
            pltpu.make_async_copy(src, xs_hbm.at[pl.ds(pl.multiple_of(slot * ROW_TILE, ROW_TILE), ROW_TILE), :],
                                  sem).start()
        return c

    lax.fori_loop(0, tm, start, 0, unroll=4)
    n = tm * TOP_K * ROW_TILE
    pltpu.make_async_copy(xs_hbm.at[pl.ds(0, n), :], xs_hbm.at[pl.ds(0, n), :], sem).wait()


def _dispatch(dest, pad_end, padded, n_used, h2_tiles, n_slots):
    n_tok = h2_tiles.shape[0] // ROW_TILE
    tm = 512
    return pl.pallas_call(
        functools.partial(_dispatch_kernel, tm=tm, n_tok=n_tok, n_blocks=n_slots // MOE_BLOCK),
        grid_spec=pltpu.PrefetchScalarGridSpec(
            num_scalar_prefetch=4,
            grid=(n_tok // tm,),
            in_specs=[pl.BlockSpec((tm * ROW_TILE, LANES), lambda i, *_: (i, 0))],
            out_specs=pl.BlockSpec(memory_space=pl.ANY),
            scratch_shapes=[pltpu.VMEM((MOE_BLOCK * ROW_TILE, LANES), F32), pltpu.SemaphoreType.DMA(())]),
        out_shape=jax.ShapeDtypeStruct((n_slots * ROW_TILE, LANES), F32),
        compiler_params=_params(("arbitrary",)),
        name="moe_dispatch",
    )(dest, pad_end, padded, n_used, h2_tiles)


def _expert_kernel(be_ref, nb_ref, xs_ref, wgu_ref, bgu_ref, wdn_ref, bdn_ref, y_ref, wgu_bf, wdn_bf):
    b = pl.program_id(0)
    e = be_ref[b]
    changed = (b == 0) | (e != be_ref[jnp.maximum(b - 1, 0)])

    @pl.when(changed)
    def _():
        wgu_bf[...] = wgu_ref[0, 0].astype(BF16)
        wdn_bf[...] = wdn_ref[0, 0].astype(BF16)

    @pl.when(b < nb_ref[0])
    def _():
        x = _from_row_tiles(xs_ref, 0, MOE_BLOCK).astype(BF16)
        gu = jnp.dot(x, wgu_bf[...], preferred_element_type=F32) + bgu_ref[0, 0]
        gate = jnp.minimum(gu[:, :D_EXPERT], SWIGLU_LIMIT)
        up = jnp.clip(gu[:, D_EXPERT:], -SWIGLU_LIMIT, SWIGLU_LIMIT)
        act = (up + 1.0) * gate * jax.nn.sigmoid(SWIGLU_ALPHA * gate)
        y = jnp.dot(act.astype(BF16), wdn_bf[...], preferred_element_type=F32) + bdn_ref[0, 0]
        y_ref[...] = y.astype(BF16)

    @pl.when(b >= nb_ref[0])
    def _():
        y_ref[...] = jnp.zeros_like(y_ref)


def _experts(layer, block_e, n_used, xs_tiles, w_gu, b_gu, w_down, b_down):
    n_slots = xs_tiles.shape[0] // ROW_TILE
    n_blocks = n_slots // MOE_BLOCK
    d = w_gu.shape[2]
    de2 = w_gu.shape[3]
    by_expert = lambda b, be, nb: (layer, be[b], 0, 0)
    by_block = lambda b, be, nb: (b, 0)
    used_block = lambda b, be, nb: (jnp.minimum(b, nb[0] - 1), 0)
    return pl.pallas_call(
        _expert_kernel,
        grid_spec=pltpu.PrefetchScalarGridSpec(
            num_scalar_prefetch=2,
            grid=(n_blocks,),
            in_specs=[pl.BlockSpec((MOE_BLOCK * ROW_TILE, LANES), used_block),
                      pl.BlockSpec((1, 1, d, de2), by_expert),
                      pl.BlockSpec((1, 1, 1, de2), by_expert),
                      pl.BlockSpec((1, 1, D_EXPERT, d), by_expert),
                      pl.BlockSpec((1, 1, 1, d), by_expert)],
            out_specs=pl.BlockSpec((MOE_BLOCK, d), by_block),
            scratch_shapes=[pltpu.VMEM((d, de2), BF16), pltpu.VMEM((D_EXPERT, d), BF16)]),
        out_shape=jax.ShapeDtypeStruct((n_slots, d), BF16),
        compiler_params=_params(("arbitrary",)),
        name="moe_experts",
    )(block_e, n_used, xs_tiles, w_gu, b_gu.reshape(b_gu.shape[0], N_EXPERTS, 1, de2), w_down,
      b_down.reshape(b_down.shape[0], N_EXPERTS, 1, d))


STRIP = 64
STRIP_ALIGN = 16
STRIP_USE = STRIP - STRIP_ALIGN


def _combine_kernel(start_ref, cnt_ref, y_hbm, x1_ref, idx_ref, dest_ref, gate_ref, g_ref, out_ref, ybuf, sems,
                    *, tm, n_slots, n_tiles, final):
    i = pl.program_id(0)
    slot = i % 2

    def strip_start(tile, e, p):
        first = start_ref[tile * N_EXPERTS + e] + p * STRIP_USE
        return jnp.minimum(first // STRIP_ALIGN * STRIP_ALIGN, n_slots - STRIP)

    def strips(tile, p, buf, fn):
        for e in range(N_EXPERTS):
            a = pl.multiple_of(strip_start(tile, e, p), STRIP_ALIGN)
            fn(pltpu.make_async_copy(y_hbm.at[pl.ds(a, STRIP), :], ybuf.at[buf, pl.ds(e * STRIP, STRIP), :],
                                     sems.at[buf]))

    @pl.when(i == 0)
    def _():
        strips(0, 0, 0, lambda c: c.start())

    @pl.when(i + 1 < n_tiles)
    def _():
        strips(i + 1, 0, 1 - slot, lambda c: c.start())

    idx = idx_ref[...]
    dest = dest_ref[...].astype(F32)
    gates = gate_ref[...]
    elane = lax.broadcasted_iota(jnp.int32, (1, N_EXPERTS), 1)
    erow = lax.broadcasted_iota(jnp.int32, (tm, N_EXPERTS), 1)
    col_id = lax.broadcasted_iota(jnp.int32, (tm, N_EXPERTS * STRIP), 1)

    def contribution(p):
        first = jnp.zeros((1, N_EXPERTS), F32)
        fetched = jnp.zeros((1, N_EXPERTS), F32)
        for e in range(N_EXPERTS):
            first = jnp.where(elane == e, (start_ref[i * N_EXPERTS + e] + p * STRIP_USE).astype(F32), first)
            fetched = jnp.where(elane == e, strip_start(i, e, p).astype(F32), fetched)
        g = jnp.zeros((tm, N_EXPERTS * STRIP), F32)
        for kk in range(TOP_K):
            mine = erow == idx[:, kk:kk + 1]
            rel = dest[:, kk:kk + 1] - jnp.sum(jnp.where(mine, first, 0.0), axis=-1, keepdims=True)
            off = dest[:, kk:kk + 1] - jnp.sum(jnp.where(mine, fetched, 0.0), axis=-1, keepdims=True)
            col = idx[:, kk:kk + 1] * STRIP + off.astype(jnp.int32)
            hit = (col_id == col) & (rel >= 0) & (rel < STRIP_USE)
            g = g + jnp.where(hit, gates[:, kk:kk + 1], 0.0)
        return jnp.dot(g.astype(BF16), ybuf[slot], preferred_element_type=F32)

    strips(i, 0, slot, lambda c: c.wait())
    x = x1_ref[...] + contribution(0)

    most = cnt_ref[i * N_EXPERTS]
    for e in range(1, N_EXPERTS):
        most = jnp.maximum(most, cnt_ref[i * N_EXPERTS + e])

    def more(p, x):
        strips(i, p, slot, lambda c: c.start())
        strips(i, p, slot, lambda c: c.wait())
        return x + contribution(p)

    x = lax.fori_loop(1, (most + STRIP_USE - 1) // STRIP_USE, more, x)
    out_ref[...] = _rms(x, g_ref[...]) if final else x


def _combine(seg_start, seg_cnt, y_slots, x1, idx_tk, dest_tk, gates_tk, final_g, final):
    t, d = x1.shape
    tm = 256
    n_tiles = t // tm
    tok = lambda i, *_: (i, 0)
    return pl.pallas_call(
        functools.partial(_combine_kernel, tm=tm, n_slots=y_slots.shape[0], n_tiles=n_tiles, final=final),
        grid_spec=pltpu.PrefetchScalarGridSpec(
            num_scalar_prefetch=2,
            grid=(n_tiles,),
            in_specs=[pl.BlockSpec(memory_space=pl.ANY),
                      pl.BlockSpec((tm, d), tok),
                      pl.BlockSpec((tm, TOP_K), tok),
                      pl.BlockSpec((tm, TOP_K), tok),
                      pl.BlockSpec((tm, TOP_K), tok),
                      pl.BlockSpec((1, d), lambda i, *_: (0, 0))],
            out_specs=pl.BlockSpec((tm, d), tok),
            scratch_shapes=[pltpu.VMEM((2, N_EXPERTS * STRIP, d), BF16), pltpu.SemaphoreType.DMA((2,))]),
        out_shape=jax.ShapeDtypeStruct((t, d), F32),
        compiler_params=_params(("arbitrary",)),
        name="moe_combine",
    )(seg_start, seg_cnt, y_slots, x1, idx_tk, dest_tk, gates_tk, final_g)


def _slot_layout(top_idx, rank, counts, tile_base, n_blocks):
    counts = counts.reshape(-1)
    padded = (counts + MOE_BLOCK - 1) // MOE_BLOCK * MOE_BLOCK
    pad_end = jnp.cumsum(padded).astype(jnp.int32)
    pad_start = pad_end - padded
    experts = jnp.arange(N_EXPERTS, dtype=jnp.int32)
    start_of = jnp.sum(jnp.where(top_idx[..., None] == experts, pad_start, 0), axis=-1)
    dest = (start_of + rank).astype(jnp.int32)
    n_used = pad_end[-1] // MOE_BLOCK
    blk = jnp.minimum(jnp.arange(n_blocks, dtype=jnp.int32), n_used - 1) * MOE_BLOCK
    block_e = jnp.minimum(jnp.sum((blk[:, None] >= pad_end[None, :]).astype(jnp.int32), axis=1), N_EXPERTS - 1)
    base = tile_base.reshape(-1, N_EXPERTS)
    seg_start = (pad_start[None, :] + base).reshape(-1).astype(jnp.int32)
    seg_cnt = (jnp.concatenate([base[1:], counts[None, :]], axis=0) - base).reshape(-1).astype(jnp.int32)
    return (dest, pad_end, padded.astype(jnp.int32), block_e.astype(jnp.int32), n_used.reshape(1).astype(jnp.int32),
            seg_start, seg_cnt)


def _rope_tables(seq):
    half = HEAD_DIM // 2
    inv_freq = ROPE_THETA ** (-jnp.arange(half, dtype=F32) / half)
    ang = jnp.arange(seq, dtype=F32)[:, None] * inv_freq[None, :]
    cos, sin = jnp.cos(ang), jnp.sin(ang)
    reps = LANES // HEAD_DIM
    cos_l = jnp.tile(jnp.concatenate([cos, cos], axis=1), (1, reps))
    sin_l = jnp.tile(jnp.concatenate([-sin, sin], axis=1), (1, reps))
    return cos_l, sin_l


def kernel(x, mem, norm1_g, w_in, pool_w, pool_scale, mem_norm_g, w_mem_kv, grp_norm_g, w_out, norm2_g,
           router_w, router_b, w_gu, b_gu, w_down, b_down, final_g):
    batch, seq, d = x.shape
    depth = w_in.shape[0]
    n_mem = mem.shape[1]
    t = batch * seq
    n_blocks = t * TOP_K // MOE_BLOCK + N_EXPERTS
    cos_l, sin_l = _rope_tables(seq)
    x2 = x.reshape(t, d)
    mem2 = mem.reshape(batch * n_mem, d)
    row = lambda a: a.reshape(1, -1)
    for l in range(depth):
        qkv, u, qm = _in_proj(x2, row(norm1_g[l]), w_in[l].astype(BF16), cos_l, sin_l, batch, seq)
        attn = [_dilated_attention(*qkv[n]) for n in range(len(DILATIONS))]
        kv = _mem_kv(mem2, row(mem_norm_g[l]), w_mem_kv[l].astype(BF16)).reshape(batch, n_mem, 2 * MEM_WIDTH)
        pw_bd = jax.scipy.linalg.block_diag(*[pool_w[l, g] for g in range(len(POOL_WINDOWS))]).astype(BF16)
        rw_t = router_w[l].T
        rw_hi = rw_t.astype(BF16)
        rw_lo = (rw_t - rw_hi.astype(F32)).astype(BF16)
        x1, h2_tiles, top_idx, gates, rank, counts, tile_base = _mix_out(
            attn, u, qm, kv, x2, pw_bd, row(pool_scale[l]), row(grp_norm_g[l]), w_out[l].astype(BF16),
            row(norm2_g[l]), rw_hi, rw_lo, router_b[l].reshape(-1, 1), batch, seq)
        dest, pad_end, padded, block_e, n_used, seg_start, seg_cnt = _slot_layout(top_idx, rank, counts, tile_base,
                                                                                  n_blocks)
        xs_tiles = _dispatch(dest.reshape(-1), pad_end, padded, n_used, h2_tiles, n_blocks * MOE_BLOCK)
        y_slots = _experts(l, block_e, n_used, xs_tiles, w_gu, b_gu, w_down, b_down)
        x2 = _combine(seg_start, seg_cnt, y_slots, x1, top_idx.T, dest.T, gates.T, row(final_g),
                      final=(l == depth - 1))
    return x2.reshape(batch, seq, d)
```

```python
import functools

import jax
import jax.numpy as jnp
from jax import lax
from jax.experimental import pallas as pl
from jax.experimental.pallas import tpu as pltpu

D_MODEL = 1024
HEAD_DIM = 64
ATTN_WIDTH = 512
DILATIONS = (1, 4, 16)
BAND = 64
ROPE_THETA = 10000.0
POOL_WINDOWS = (2, 4, 8, 16)
POOL_WIDTH = 256
POOL_GROUP = 64
POOL_HALO = 8
MEM_WIDTH = 256
N_EXPERTS = 32
TOP_K = 4
D_EXPERT = 1024
SWIGLU_ALPHA = 1.702
SWIGLU_LIMIT = 7.0
MOE_BLOCK = 256
ROUTER_TILE = 256
NORM_EPS = 1e-5
NEG_INF = -1e30
LANES = 128
SUBLANES = 8
ROW_TILE = D_MODEL // LANES

F32 = jnp.float32
BF16 = jnp.bfloat16
VMEM_LIMIT = 56 * 1024 * 1024

_NT = (((1,), (1,)), ((), ()))


def _params(sem, vmem=VMEM_LIMIT):
    return pltpu.CompilerParams(dimension_semantics=sem, vmem_limit_bytes=vmem)


def _rms(x, g):
    return x * lax.rsqrt(jnp.mean(x * x, axis=-1, keepdims=True) + NORM_EPS) * g


def _to_row_tiles(ref, val):
    m = val.shape[0]
    for s in range(ROW_TILE):
        ref[pl.ds(s, m, stride=ROW_TILE), :] = val[:, s * LANES:(s + 1) * LANES]


def _from_row_tiles(ref, start, m):
    return jnp.concatenate([ref[pl.ds(start * ROW_TILE + s, m, stride=ROW_TILE), :] for s in range(ROW_TILE)],
                           axis=1)


def _in_proj_kernel(x_ref, g_ref, w_ref, cos_ref, sin_ref, q1, k1, v1, q4, k4, v4, q16, k16, v16, u_ref, qm_ref,
                    qkv, *, tm):
    h = _rms(x_ref[...], g_ref[...]).astype(BF16)
    proj = jnp.dot(h, w_ref[...], preferred_element_type=F32)
    cos = cos_ref[...]
    sin = sin_ref[...]
    lane = lax.broadcasted_iota(jnp.int32, cos.shape, 1)
    first_half = (lane % HEAD_DIM) < (HEAD_DIM // 2)
    scale = HEAD_DIM ** -0.5

    def rope(t):
        partner = jnp.where(first_half, pltpu.roll(t, LANES - HEAD_DIM // 2, 1),
                            pltpu.roll(t, HEAD_DIM // 2, 1))
        return t * cos + partner * sin

    a = ATTN_WIDTH
    groups = a // LANES
    for c in range(groups):
        qkv[c] = rope(proj[:, c * LANES:(c + 1) * LANES]) * scale
        qkv[groups + c] = rope(proj[:, a + c * LANES:a + (c + 1) * LANES])
        qkv[2 * groups + c] = proj[:, 2 * a + c * LANES:2 * a + (c + 1) * LANES]
    u_ref[...] = proj[:, 3 * a:3 * a + POOL_WIDTH]
    qm_ref[...] = (proj[:, 3 * a + POOL_WIDTH:] * scale).astype(BF16)

    for dil, refs in ((1, (q1, k1, v1)), (4, (q4, k4, v4)), (16, (q16, k16, v16))):
        for r in range(dil):
            for n, ref in enumerate(refs):
                for c in range(groups):
                    rows = qkv[n * groups + c, pl.ds(r, tm // dil, stride=dil), :] if dil > 1 else qkv[n * groups + c]
                    ref[0, r, :, c * LANES:(c + 1) * LANES] = rows.astype(BF16)


def _in_proj(x2, g, w_bf, cos, sin, batch, seq):
    t, d = x2.shape
    tm = 512
    tiles_per_seq = seq // tm
    cols = w_bf.shape[1]
    row = lambda i: (i, 0)
    fixed = lambda i: (0, 0)
    cls = lambda i: (i // tiles_per_seq, 0, i % tiles_per_seq, 0)
    cls_specs, cls_shapes = [], []
    for dil in DILATIONS:
        for _ in range(3):
            cls_specs.append(pl.BlockSpec((1, dil, tm // dil, ATTN_WIDTH), cls))
            cls_shapes.append(jax.ShapeDtypeStruct((batch, dil, seq // dil, ATTN_WIDTH), BF16))
    outs = pl.pallas_call(
        functools.partial(_in_proj_kernel, tm=tm),
        grid=(t // tm,),
        in_specs=[pl.BlockSpec((tm, d), row),
                  pl.BlockSpec((1, d), fixed),
                  pl.BlockSpec((d, cols), fixed),
                  pl.BlockSpec((tm, LANES), lambda i: (i % tiles_per_seq, 0)),
                  pl.BlockSpec((tm, LANES), lambda i: (i % tiles_per_seq, 0))],
        out_specs=cls_specs + [pl.BlockSpec((tm, POOL_WIDTH), row), pl.BlockSpec((tm, MEM_WIDTH), row)],
        out_shape=cls_shapes + [jax.ShapeDtypeStruct((t, POOL_WIDTH), F32),
                                jax.ShapeDtypeStruct((t, MEM_WIDTH), BF16)],
        scratch_shapes=[pltpu.VMEM((3 * ATTN_WIDTH // LANES, tm, LANES), F32)],
        compiler_params=_params(("parallel",)),
        name="in_proj",
    )(x2, g, w_bf, cos, sin)
    qkv = [outs[3 * n:3 * n + 3] for n in range(len(DILATIONS))]
    return qkv, outs[-2], outs[-1]


def _head_pair_attention(q, k, v, valid):
    lane = lax.broadcasted_iota(jnp.int32, q.shape, 1)
    outs, lses = [], []
    for half in range(2):
        mine = (lane // HEAD_DIM) == half
        s = lax.dot_general(jnp.where(mine, q, jnp.zeros_like(q)), k, _NT,
                            preferred_element_type=F32)
        if valid is not None:
            s = jnp.where(valid, s, NEG_INF)
        m = jnp.max(s, axis=-1, keepdims=True)
        p = jnp.exp(s - m)
        den = jnp.sum(p, axis=-1, keepdims=True)
        pv = jnp.dot(p.astype(BF16), v, preferred_element_type=F32)
        outs.append(pv / den)
        lses.append(m + jnp.log(den))
    return jnp.where((lane // HEAD_DIM) == 0, outs[0], outs[1]), lses


def _attn_kernel(q_ref, kp_ref, kc_ref, kn_ref, vp_ref, vc_ref, vn_ref, o_ref, lse_ref, *, tq, sub, length):
    j = pl.program_id(2)
    nk = sub + 2 * BAND
    row = lax.broadcasted_iota(jnp.int32, (sub, nk), 0)
    col = lax.broadcasted_iota(jnp.int32, (sub, nk), 1)
    in_band = jnp.abs(col - BAND - row) <= BAND
    lane = lax.broadcasted_iota(jnp.int32, (sub, LANES), 1)
    for c in range(ATTN_WIDTH // LANES):
        sl = slice(c * LANES, (c + 1) * LANES)
        k = jnp.concatenate([kp_ref[0, 0, :, sl], kc_ref[0, 0, :, sl], kn_ref[0, 0, :, sl]], axis=0)
        v = jnp.concatenate([vp_ref[0, 0, :, sl], vc_ref[0, 0, :, sl], vn_ref[0, 0, :, sl]], axis=0)
        for s in range(tq // sub):
            rows = slice(s * sub, (s + 1) * sub)
            key = j * tq + s * sub - BAND + col
            valid = in_band & (key >= 0) & (key < length)
            o, lses = _head_pair_attention(q_ref[0, 0, rows, sl], k[s * sub:s * sub + nk], v[s * sub:s * sub + nk],
                                           valid)
            o_ref[0, 0, rows, sl] = o
            lse_tile = jnp.where(lane == 2 * c, lses[0], lses[1])
            if c == 0:
                lse_ref[0, 0, rows, :] = lse_tile
            else:
                lse_ref[0, 0, rows, :] = jnp.where((lane // 2) == c, lse_tile, lse_ref[0, 0, rows, :])


def _dilated_attention(q, k, v):
    batch, dilation, length, _ = q.shape
    tq = 512
    per = tq // BAND
    n_band_blocks = length // BAND
    cur = lambda b, r, j: (b, r, j, 0)
    prev = lambda b, r, j: (b, r, jnp.maximum(j * per - 1, 0), 0)
    nxt = lambda b, r, j: (b, r, jnp.minimum((j + 1) * per, n_band_blocks - 1), 0)
    big = lambda imap: pl.BlockSpec((1, 1, tq, ATTN_WIDTH), imap)
    halo = lambda imap: pl.BlockSpec((1, 1, BAND, ATTN_WIDTH), imap)
    return pl.pallas_call(
        functools.partial(_attn_kernel, tq=tq, sub=128, length=length),
        grid=(batch, dilation, length // tq),
        in_specs=[big(cur), halo(prev), big(cur), halo(nxt), halo(prev), big(cur), halo(nxt)],
        out_specs=[pl.BlockSpec((1, 1, tq, ATTN_WIDTH), cur), pl.BlockSpec((1, 1, tq, LANES), cur)],
        out_shape=[jax.ShapeDtypeStruct((batch, dilation, length, ATTN_WIDTH), F32),
                   jax.ShapeDtypeStruct((batch, dilation, length, LANES), F32)],
        compiler_params=_params(("parallel", "parallel", "parallel")),
        name=f"dilated_attn_d{dilation}",
    )(q, k, k, k, v, v, v)


def _mem_kv_kernel(m_ref, g_ref, w_ref, kv_ref):
    h = _rms(m_ref[...], g_ref[...]).astype(BF16)
    kv_ref[...] = jnp.dot(h, w_ref[...], preferred_element_type=F32).astype(BF16)


def _mem_kv(mem2, g, w_bf):
    n, d = mem2.shape
    cols = w_bf.shape[1]
    tm = 256
    return pl.pallas_call(
        _mem_kv_kernel,
        grid=(n // tm,),
        in_specs=[pl.BlockSpec((tm, d), lambda i: (i, 0)),
                  pl.BlockSpec((1, d), lambda i: (0, 0)),
                  pl.BlockSpec((d, cols), lambda i: (0, 0))],
        out_specs=pl.BlockSpec((tm, cols), lambda i: (i, 0)),
        out_shape=jax.ShapeDtypeStruct((n, cols), BF16),
        compiler_params=_params(("parallel",)),
        name="mem_kv",
    )(mem2, g, w_bf)


def _mix_out_kernel(o1_ref, o4_ref, o16_ref, l1_ref, l4_ref, l16_ref, up_ref, u_ref, un_ref, qm_ref, kv_ref,
                    x_ref, pw_ref, ps_ref, gg_ref, wo_ref, n2_ref, rwh_ref, rwl_ref, rb_ref,
                    x1_ref, h2_ref, idx_ref, gate_ref, rank_ref, cnt_ref, base_ref,
                    o4_s, o16_s, l4_s, l16_s, carry, *, tm, seq):
    i = pl.program_id(0)
    tiles_per_seq = seq // tm
    pos0 = (i % tiles_per_seq) * tm

    for dil, src, dst in ((4, o4_ref, o4_s), (16, o16_ref, o16_s), (4, l4_ref, l4_s), (16, l16_ref, l16_s)):
        for r in range(dil):
            for c in range(dst.shape[0]):
                dst[c, pl.ds(r, tm // dil, stride=dil), :] = src[0, r, :, c * LANES:(c + 1) * LANES]

    l1, l2, l3 = l1_ref[0, 0], l4_s[0], l16_s[0]
    lm = jnp.maximum(jnp.maximum(l1, l2), l3)
    e1, e2, e3 = jnp.exp(l1 - lm), jnp.exp(l2 - lm), jnp.exp(l3 - lm)
    es = e1 + e2 + e3
    w1, w2, w3 = e1 / es, e2 / es, e3 / es
    ya = []
    for c in range(ATTN_WIDTH // LANES):
        sl = slice(c * LANES, (c + 1) * LANES)
        lane = lax.broadcasted_iota(jnp.int32, (tm, LANES), 1)
        lo = lane < HEAD_DIM

        def per_head(w):
            return jnp.where(lo, w[:, 2 * c:2 * c + 1], w[:, 2 * c + 1:2 * c + 2])

        ya.append(per_head(w1) * o1_ref[0, 0, :, sl] + per_head(w2) * o4_s[c] + per_head(w3) * o16_s[c])
    ya = jnp.concatenate(ya, axis=1)

    u = u_ref[...]
    before = jnp.where(pos0 > 0, up_ref[...], 0.0)
    after = jnp.where(pos0 + tm < seq, un_ref[...], 0.0)
    ext = jnp.concatenate([before, u, after], axis=0)
    n_ext = tm + 2 * POOL_HALO
    sums = []
    acc = ext
    shift = 1
    for w in POOL_WINDOWS:
        if w == 2:
            acc = pltpu.roll(ext, 1, 0) + ext
        else:
            acc = pltpu.roll(acc, shift, 0) + pltpu.roll(acc, n_ext - shift, 0)
            shift *= 2
        sums.append(acc[POOL_HALO:POOL_HALO + tm])
    pos = pos0 + lax.broadcasted_iota(jnp.int32, (tm, POOL_WIDTH), 0)
    grp = lax.broadcasted_iota(jnp.int32, (tm, POOL_WIDTH), 1) // POOL_GROUP
    mean = jnp.zeros((tm, POOL_WIDTH), F32)
    for g, w in enumerate(POOL_WINDOWS):
        cnt = jnp.minimum(pos + (w - 1 - w // 2), seq - 1) + 1 - jnp.maximum(pos - w // 2, 0)
        mean = jnp.where(grp == g, sums[g] / cnt.astype(F32), mean)
    d = (mean - u).astype(BF16)
    yp = jnp.dot(d, pw_ref[...], preferred_element_type=F32) * ps_ref[...]

    ym = []
    for c in range(MEM_WIDTH // LANES):
        sl = slice(c * LANES, (c + 1) * LANES)
        o, _ = _head_pair_attention(qm_ref[:, sl], kv_ref[0, :, sl],
                                    kv_ref[0, :, MEM_WIDTH + c * LANES:MEM_WIDTH + (c + 1) * LANES], None)
        ym.append(o)
    ym = jnp.concatenate(ym, axis=1)

    gg = gg_ref[...]
    a, p = ATTN_WIDTH, POOL_WIDTH
    y = jnp.concatenate([_rms(ya, gg[:, :a]), _rms(yp, gg[:, a:a + p]), _rms(ym, gg[:, a + p:])], axis=1)
    x1 = x_ref[...] + jnp.dot(y.astype(BF16), wo_ref[...], preferred_element_type=F32)
    x1_ref[...] = x1

    h2 = _rms(x1, n2_ref[...])
    _to_row_tiles(h2_ref, h2)

    hi = h2.astype(BF16)
    lo = (h2 - hi.astype(F32)).astype(BF16)
    logits = (lax.dot_general(rwh_ref[...], hi, _NT, preferred_element_type=F32)
              + lax.dot_general(rwh_ref[...], lo, _NT, preferred_element_type=F32)
              + lax.dot_general(rwl_ref[...], hi, _NT, preferred_element_type=F32)) + rb_ref[...]
    eidx = lax.broadcasted_iota(jnp.int32, logits.shape, 0)
    krow = lax.broadcasted_iota(jnp.int32, (TOP_K, tm), 0)
    vals = jnp.zeros((TOP_K, tm), F32)
    idxs = jnp.zeros((TOP_K, tm), jnp.int32)
    work = logits
    args = []
    for kk in range(TOP_K):
        best = jnp.max(work, axis=0, keepdims=True)
        arg = jnp.min(jnp.where(work == best, eidx, N_EXPERTS), axis=0, keepdims=True)
        vals = jnp.where(krow == kk, best, vals)
        idxs = jnp.where(krow == kk, arg, idxs)
        work = jnp.where(eidx == arg, -jnp.inf, work)
        args.append(arg)
    ex = jnp.exp(vals - vals[0:1])
    gate_ref[...] = ex / jnp.sum(ex, axis=0, keepdims=True)
    idx_ref[...] = idxs

    @pl.when(i == 0)
    def _():
        carry[...] = jnp.zeros_like(carry)

    chosen = (work == -jnp.inf).astype(BF16)
    earlier = (lax.broadcasted_iota(jnp.int32, (tm, tm), 0)
               < lax.broadcasted_iota(jnp.int32, (tm, tm), 1)).astype(BF16)
    before_me = jnp.dot(chosen, earlier, preferred_element_type=F32) + carry[...]
    ranks = jnp.zeros((TOP_K, tm), F32)
    for kk in range(TOP_K):
        ranks = jnp.where(krow == kk, jnp.sum(jnp.where(eidx == args[kk], before_me, 0.0), axis=0, keepdims=True),
                          ranks)
    rank_ref[...] = ranks.astype(jnp.int32)
    base_ref[0] = carry[...].astype(jnp.int32)
    carry[...] = carry[...] + jnp.sum(chosen.astype(F32), axis=1, keepdims=True)
    cnt_ref[...] = carry[...].astype(jnp.int32)


def _mix_out(attn, u, qm, kv, x2, pw_bd, ps, gg, wo_bf, n2, rw_hi, rw_lo, rb, batch, seq):
    t, d = x2.shape
    tm = ROUTER_TILE
    hp = tm // POOL_HALO
    n_halo = t // POOL_HALO
    tiles_per_seq = seq // tm
    row = lambda i: (i, 0)
    fixed = lambda i: (0, 0)
    cls = lambda i: (i // tiles_per_seq, 0, i % tiles_per_seq, 0)
    rowspec = lambda w: pl.BlockSpec((tm, w), row)
    clsspec = lambda dil, w: pl.BlockSpec((1, dil, tm // dil, w), cls)
    tok_cols = lambda i: (0, i)
    (o1, l1), (o4, l4), (o16, l16) = attn
    return pl.pallas_call(
        functools.partial(_mix_out_kernel, tm=tm, seq=seq),
        grid=(t // tm,),
        in_specs=[clsspec(1, ATTN_WIDTH), clsspec(4, ATTN_WIDTH), clsspec(16, ATTN_WIDTH),
                  clsspec(1, LANES), clsspec(4, LANES), clsspec(16, LANES),
                  pl.BlockSpec((POOL_HALO, POOL_WIDTH), lambda i: (jnp.maximum(i * hp - 1, 0), 0)),
                  rowspec(POOL_WIDTH),
                  pl.BlockSpec((POOL_HALO, POOL_WIDTH), lambda i: (jnp.minimum((i + 1) * hp, n_halo - 1), 0)),
                  rowspec(MEM_WIDTH),
                  pl.BlockSpec((1, kv.shape[1], kv.shape[2]), lambda i: (i // tiles_per_seq, 0, 0)),
                  rowspec(d),
                  pl.BlockSpec(pw_bd.shape, fixed),
                  pl.BlockSpec(ps.shape, fixed),
                  pl.BlockSpec(gg.shape, fixed),
                  pl.BlockSpec(wo_bf.shape, fixed),
                  pl.BlockSpec(n2.shape, fixed),
                  pl.BlockSpec(rw_hi.shape, fixed),
                  pl.BlockSpec(rw_lo.shape, fixed),
                  pl.BlockSpec(rb.shape, fixed)],
        out_specs=[rowspec(d),
                   pl.BlockSpec((tm * ROW_TILE, LANES), row),
                   pl.BlockSpec((TOP_K, tm), tok_cols),
                   pl.BlockSpec((TOP_K, tm), tok_cols),
                   pl.BlockSpec((TOP_K, tm), tok_cols),
                   pl.BlockSpec((N_EXPERTS, 1), fixed),
                   pl.BlockSpec((1, N_EXPERTS, 1), lambda i: (i, 0, 0))],
        out_shape=[jax.ShapeDtypeStruct((t, d), F32),
                   jax.ShapeDtypeStruct((t * ROW_TILE, LANES), F32),
                   jax.ShapeDtypeStruct((TOP_K, t), jnp.int32),
                   jax.ShapeDtypeStruct((TOP_K, t), F32),
                   jax.ShapeDtypeStruct((TOP_K, t), jnp.int32),
                   jax.ShapeDtypeStruct((N_EXPERTS, 1), jnp.int32),
                   jax.ShapeDtypeStruct((t // tm, N_EXPERTS, 1), jnp.int32)],
        scratch_shapes=[pltpu.VMEM((ATTN_WIDTH // LANES, tm, LANES), F32),
                        pltpu.VMEM((ATTN_WIDTH // LANES, tm, LANES), F32),
                        pltpu.VMEM((1, tm, LANES), F32), pltpu.VMEM((1, tm, LANES), F32),
                        pltpu.VMEM((N_EXPERTS, 1), F32)],
        compiler_params=_params(("arbitrary",)),
        name="mix_out_router",
    )(o1, o4, o16, l1, l4, l16, u, u, u, qm, kv, x2, pw_bd, ps, gg, wo_bf, n2, rw_hi, rw_lo, rb)


def _dispatch_kernel(dest_ref, pend_ref, padded_ref, nb_ref, h_ref, xs_hbm, zeros, sem, *, tm, n_tok, n_blocks):
    i = pl.program_id(0)
    blk = MOE_BLOCK * ROW_TILE

    @pl.when(i == 0)
    def _():
        zeros[...] = jnp.zeros_like(zeros)

        def clear(block):
            return pltpu.make_async_copy(zeros, xs_hbm.at[pl.ds(pl.multiple_of(block * blk, blk), blk), :], sem)

        def for_each_cleared_block(fn):
            for e in range(N_EXPERTS):
                @pl.when(padded_ref[e] > 0)
                def _():
                    fn(clear(pend_ref[e] // MOE_BLOCK - 1))

                @pl.when(nb_ref[0] + e < n_blocks)
                def _():
                    fn(clear(nb_ref[0] + e))

        for_each_cleared_block(lambda c: c.start())
        for_each_cleared_block(lambda c: c.wait())

    def start(r, c):
        src = h_ref.at[pl.ds(pl.multiple_of(r * ROW_TILE, ROW_TILE), ROW_TILE), :]
        for kk in range(TOP_K):
            slot = dest_ref[kk * n_tok + i * tm + r]
            pltpu.make_async_copy(src, xs_hbm.at[pl.ds(pl.multiple_of(slot * ROW_TILE, ROW_TILE), ROW_TILE), :],
                                  sem).start()
        return c

    lax.fori_loop(0, tm, start, 0, unroll=4)
    n = tm * TOP_K * ROW_TILE
    pltpu.make_async_copy(xs_hbm.at[pl.ds(0, n), :], xs_hbm.at[pl.ds(0, n), :], sem).wait()


def _dispatch(dest, pad_end, padded, n_used, h2_tiles, n_slots):
    n_tok = h2_tiles.shape[0] // ROW_TILE
    tm = 512
    return pl.pallas_call(
        functools.partial(_dispatch_kernel, tm=tm, n_tok=n_tok, n_blocks=n_slots // MOE_BLOCK),
        grid_spec=pltpu.PrefetchScalarGridSpec(
            num_scalar_prefetch=4,
            grid=(n_tok // tm,),
            in_specs=[pl.BlockSpec((tm * ROW_TILE, LANES), lambda i, *_: (i, 0))],
            out_specs=pl.BlockSpec(memory_space=pl.ANY),
            scratch_shapes=[pltpu.VMEM((MOE_BLOCK * ROW_TILE, LANES), F32), pltpu.SemaphoreType.DMA(())]),
        out_shape=jax.ShapeDtypeStruct((n_slots * ROW_TILE, LANES), F32),
        compiler_params=_params(("arbitrary",)),
        name="moe_dispatch",
    )(dest, pad_end, padded, n_used, h2_tiles)


def _expert_kernel(be_ref, nb_ref, xs_ref, wgu_ref, bgu_ref, wdn_ref, bdn_ref, y_ref, wgu_bf, wdn_bf):
    b = pl.program_id(0)
    e = be_ref[b]
    changed = (b == 0) | (e != be_ref[jnp.maximum(b - 1, 0)])

    @pl.when(changed)
    def _():
        wgu_bf[...] = wgu_ref[0, 0].astype(BF16)
        wdn_bf[...] = wdn_ref[0, 0].astype(BF16)

    @pl.when(b < nb_ref[0])
    def _():
        x = _from_row_tiles(xs_ref, 0, MOE_BLOCK).astype(BF16)
        gu = jnp.dot(x, wgu_bf[...], preferred_element_type=F32) + bgu_ref[0, 0]
        gate = jnp.minimum(gu[:, :D_EXPERT], SWIGLU_LIMIT)
        up = jnp.clip(gu[:, D_EXPERT:], -SWIGLU_LIMIT, SWIGLU_LIMIT)
        act = (up + 1.0) * gate * jax.nn.sigmoid(SWIGLU_ALPHA * gate)
        y = jnp.dot(act.astype(BF16), wdn_bf[...], preferred_element_type=F32) + bdn_ref[0, 0]
        y_ref[...] = y.astype(BF16)

    @pl.when(b >= nb_ref[0])
    def _():
        y_ref[...] = jnp.zeros_like(y_ref)


def _experts(layer, block_e, n_used, xs_tiles, w_gu, b_gu, w_down, b_down):
    n_slots = xs_tiles.shape[0] // ROW_TILE
    n_blocks = n_slots // MOE_BLOCK
    d = w_gu.shape[2]
    de2 = w_gu.shape[3]
    by_expert = lambda b, be, nb: (layer, be[b], 0, 0)
    by_block = lambda b, be, nb: (b, 0)
    used_block = lambda b, be, nb: (jnp.minimum(b, nb[0] - 1), 0)
    return pl.pallas_call(
        _expert_kernel,
        grid_spec=pltpu.PrefetchScalarGridSpec(
            num_scalar_prefetch=2,
            grid=(n_blocks,),
            in_specs=[pl.BlockSpec((MOE_BLOCK * ROW_TILE, LANES), used_block),
                      pl.BlockSpec((1, 1, d, de2), by_expert),
                      pl.BlockSpec((1, 1, 1, de2), by_expert),
                      pl.BlockSpec((1, 1, D_EXPERT, d), by_expert),
                      pl.BlockSpec((1, 1, 1, d), by_expert)],
            out_specs=pl.BlockSpec((MOE_BLOCK, d), by_block),
            scratch_shapes=[pltpu.VMEM((d, de2), BF16), pltpu.VMEM((D_EXPERT, d), BF16)]),
        out_shape=jax.ShapeDtypeStruct((n_slots, d), BF16),
        compiler_params=_params(("arbitrary",)),
        name="moe_experts",
    )(block_e, n_used, xs_tiles, w_gu, b_gu.reshape(b_gu.shape[0], N_EXPERTS, 1, de2), w_down,
      b_down.reshape(b_down.shape[0], N_EXPERTS, 1, d))


CHUNK = 16
COMBINE_ROWS = ROUTER_TILE * TOP_K + 2 * N_EXPERTS * CHUNK


def _combine_kernel(a_ref, nch_ref, tot_ref, y_hbm, x1_ref, col_ref, gate_ref, g_ref, out_ref, ybuf, sems,
                    *, tm, n_tiles, final):
    i = pl.program_id(0)
    slot = i % 2

    @pl.when(i == 0)
    def _():
        ybuf[...] = jnp.zeros_like(ybuf)

    def fetch(tile, buf):
        pos = jnp.int32(0)
        for e in range(N_EXPERTS):
            a = a_ref[tile * N_EXPERTS + e]

            def one(c, pos):
                pltpu.make_async_copy(y_hbm.at[pl.ds(pl.multiple_of(a + c * CHUNK, CHUNK), CHUNK), :],
                                      ybuf.at[buf, pl.ds(pl.multiple_of(pos, CHUNK), CHUNK), :],
                                      sems.at[buf]).start()
                return pos + CHUNK

            pos = lax.fori_loop(0, nch_ref[tile * N_EXPERTS + e], one, pos)

    @pl.when(i == 0)
    def _():
        fetch(0, 0)

    @pl.when(i + 1 < n_tiles)
    def _():
        fetch(i + 1, 1 - slot)

    col = col_ref[...]
    gates = gate_ref[...]
    col_id = lax.broadcasted_iota(jnp.int32, (tm, COMBINE_ROWS), 1)
    g = jnp.zeros((tm, COMBINE_ROWS), F32)
    for kk in range(TOP_K):
        g = jnp.where(col_id == col[:, kk:kk + 1], gates[:, kk:kk + 1], g)

    def wait_one(c, carry):
        pltpu.make_async_copy(y_hbm.at[pl.ds(0, CHUNK), :], ybuf.at[slot, pl.ds(0, CHUNK), :], sems.at[slot]).wait()
        return carry

    lax.fori_loop(0, tot_ref[i], wait_one, 0)
    x = x1_ref[...] + jnp.dot(g.astype(BF16), ybuf[slot], preferred_element_type=F32)
    out_ref[...] = _rms(x, g_ref[...]) if final else x


def _combine(seg_a, seg_nch, tile_chunks, y_slots, x1, col_tk, gates_tk, final_g, final):
    t, d = x1.shape
    tm = ROUTER_TILE
    n_tiles = t // tm
    tok = lambda i, *_: (i, 0)
    return pl.pallas_call(
        functools.partial(_combine_kernel, tm=tm, n_tiles=n_tiles, final=final),
        grid_spec=pltpu.PrefetchScalarGridSpec(
            num_scalar_prefetch=3,
            grid=(n_tiles,),
            in_specs=[pl.BlockSpec(memory_space=pl.ANY),
                      pl.BlockSpec((tm, d), tok),
                      pl.BlockSpec((tm, TOP_K), tok),
                      pl.BlockSpec((tm, TOP_K), tok),
                      pl.BlockSpec((1, d), lambda i, *_: (0, 0))],
            out_specs=pl.BlockSpec((tm, d), tok),
            scratch_shapes=[pltpu.VMEM((2, COMBINE_ROWS, d), BF16), pltpu.SemaphoreType.DMA((2,))]),
        out_shape=jax.ShapeDtypeStruct((t, d), F32),
        compiler_params=_params(("arbitrary",)),
        name="moe_combine",
    )(seg_a, seg_nch, tile_chunks, y_slots, x1, col_tk, gates_tk, final_g)


def _slot_layout(top_idx, rank, counts, tile_base, n_blocks):
    counts = counts.reshape(-1)
    padded = (counts + MOE_BLOCK - 1) // MOE_BLOCK * MOE_BLOCK
    pad_end = jnp.cumsum(padded).astype(jnp.int32)
    pad_start = pad_end - padded
    experts = jnp.arange(N_EXPERTS, dtype=jnp.int32)
    start_of = jnp.sum(jnp.where(top_idx[..., None] == experts, pad_start, 0), axis=-1)
    dest = (start_of + rank).astype(jnp.int32)
    n_used = pad_end[-1] // MOE_BLOCK
    blk = jnp.minimum(jnp.arange(n_blocks, dtype=jnp.int32), n_used - 1) * MOE_BLOCK
    block_e = jnp.minimum(jnp.sum((blk[:, None] >= pad_end[None, :]).astype(jnp.int32), axis=1), N_EXPERTS - 1)
    base = tile_base.reshape(-1, N_EXPERTS)
    run_start = pad_start[None, :] + base
    run_len = jnp.concatenate([base[1:], counts[None, :]], axis=0) - base
    seg_a = run_start // CHUNK * CHUNK
    seg_nch = jnp.where(run_len > 0, (run_start + run_len - seg_a + CHUNK - 1) // CHUNK, 0)
    buf_row0 = (jnp.cumsum(seg_nch, axis=1) - seg_nch) * CHUNK
    shift = jnp.repeat(buf_row0 - seg_a, ROUTER_TILE, axis=0)
    col = dest + jnp.sum(jnp.where(top_idx[..., None] == experts, shift[None], 0), axis=-1)
    flat = lambda a: a.reshape(-1).astype(jnp.int32)
    return (dest, pad_end, padded.astype(jnp.int32), block_e.astype(jnp.int32), n_used.reshape(1).astype(jnp.int32),
            flat(seg_a), flat(seg_nch), flat(jnp.sum(seg_nch, axis=1)), col.astype(jnp.int32))


def _rope_tables(seq):
    half = HEAD_DIM // 2
    inv_freq = ROPE_THETA ** (-jnp.arange(half, dtype=F32) / half)
    ang = jnp.arange(seq, dtype=F32)[:, None] * inv_freq[None, :]
    cos, sin = jnp.cos(ang), jnp.sin(ang)
    reps = LANES // HEAD_DIM
    cos_l = jnp.tile(jnp.concatenate([cos, cos], axis=1), (1, reps))
    sin_l = jnp.tile(jnp.concatenate([-sin, sin], axis=1), (1, reps))
    return cos_l, sin_l


def kernel(x, mem, norm1_g, w_in, pool_w, pool_scale, mem_norm_g, w_mem_kv, grp_norm_g, w_out, norm2_g,
           router_w, router_b, w_gu, b_gu, w_down, b_down, final_g):
    batch, seq, d = x.shape
    depth = w_in.shape[0]
    n_mem = mem.shape[1]
    t = batch * seq
    n_blocks = t * TOP_K // MOE_BLOCK + N_EXPERTS
    cos_l, sin_l = _rope_tables(seq)
    x2 = x.reshape(t, d)
    mem2 = mem.reshape(batch * n_mem, d)
    row = lambda a: a.reshape(1, -1)
    for l in range(depth):
        qkv, u, qm = _in_proj(x2, row(norm1_g[l]), w_in[l].astype(BF16), cos_l, sin_l, batch, seq)
        attn = [_dilated_attention(*qkv[n]) for n in range(len(DILATIONS))]
        kv = _mem_kv(mem2, row(mem_norm_g[l]), w_mem_kv[l].astype(BF16)).reshape(batch, n_mem, 2 * MEM_WIDTH)
        pw_bd = jax.scipy.linalg.block_diag(*[pool_w[l, g] for g in range(len(POOL_WINDOWS))]).astype(BF16)
        rw_t = router_w[l].T
        rw_hi = rw_t.astype(BF16)
        rw_lo = (rw_t - rw_hi.astype(F32)).astype(BF16)
        x1, h2_tiles, top_idx, gates, rank, counts, tile_base = _mix_out(
            attn, u, qm, kv, x2, pw_bd, row(pool_scale[l]), row(grp_norm_g[l]), w_out[l].astype(BF16),
            row(norm2_g[l]), rw_hi, rw_lo, router_b[l].reshape(-1, 1), batch, seq)
        dest, pad_end, padded, block_e, n_used, seg_a, seg_nch, tile_chunks, col = _slot_layout(
            top_idx, rank, counts, tile_base, n_blocks)
        xs_tiles = _dispatch(dest.reshape(-1), pad_end, padded, n_used, h2_tiles, n_blocks * MOE_BLOCK)
        y_slots = _experts(l, block_e, n_used, xs_tiles, w_gu, b_gu, w_down, b_down)
        x2 = _combine(seg_a, seg_nch, tile_chunks, y_slots, x1, col.T, gates.T, row(final_g),
                      final=(l == depth - 1))
    return x2.reshape(batch, seq, d)
```

```python
import functools

import jax
import jax.numpy as jnp
from jax import lax
from jax.experimental import pallas as pl
from jax.experimental.pallas import tpu as pltpu

D_MODEL = 1024
HEAD_DIM = 64
ATTN_WIDTH = 512
DILATIONS = (1, 4, 16)
BAND = 64
ROPE_THETA = 10000.0
POOL_WINDOWS = (2, 4, 8, 16)
POOL_WIDTH = 256
POOL_GROUP = 64
POOL_HALO = 8
MEM_WIDTH = 256
N_EXPERTS = 32
TOP_K = 4
D_EXPERT = 1024
SWIGLU_ALPHA = 1.702
SWIGLU_LIMIT = 7.0
MOE_BLOCK = 256
ROUTER_TILE = 256
NORM_EPS = 1e-5
NEG_INF = -1e30
LANES = 128
SUBLANES = 8
ROW_TILE = D_MODEL // LANES

F32 = jnp.float32
BF16 = jnp.bfloat16
VMEM_LIMIT = 56 * 1024 * 1024

_NT = (((1,), (1,)), ((), ()))


def _params(sem, vmem=VMEM_LIMIT):
    return pltpu.CompilerParams(dimension_semantics=sem, vmem_limit_bytes=vmem)


def _rms(x, g):
    return x * lax.rsqrt(jnp.mean(x * x, axis=-1, keepdims=True) + NORM_EPS) * g


def _to_row_tiles(ref, val):
    m = val.shape[0]
    for s in range(ROW_TILE):
        ref[pl.ds(s, m, stride=ROW_TILE), :] = val[:, s * LANES:(s + 1) * LANES]


def _from_row_tiles(ref, start, m):
    return jnp.concatenate([ref[pl.ds(start * ROW_TILE + s, m, stride=ROW_TILE), :] for s in range(ROW_TILE)],
                           axis=1)


def _in_proj_kernel(x_ref, g_ref, w_ref, cos_ref, sin_ref, q1, k1, v1, q4, k4, v4, q16, k16, v16, u_ref, qm_ref,
                    qkv, *, tm):
    h = _rms(x_ref[...], g_ref[...]).astype(BF16)
    proj = jnp.dot(h, w_ref[...], preferred_element_type=F32)
    cos = cos_ref[...]
    sin = sin_ref[...]
    lane = lax.broadcasted_iota(jnp.int32, cos.shape, 1)
    first_half = (lane % HEAD_DIM) < (HEAD_DIM // 2)
    scale = HEAD_DIM ** -0.5

    def rope(t):
        partner = jnp.where(first_half, pltpu.roll(t, LANES - HEAD_DIM // 2, 1),
                            pltpu.roll(t, HEAD_DIM // 2, 1))
        return t * cos + partner * sin

    a = ATTN_WIDTH
    groups = a // LANES
    for c in range(groups):
        qkv[c] = rope(proj[:, c * LANES:(c + 1) * LANES]) * scale
        qkv[groups + c] = rope(proj[:, a + c * LANES:a + (c + 1) * LANES])
        qkv[2 * groups + c] = proj[:, 2 * a + c * LANES:2 * a + (c + 1) * LANES]
    u_ref[...] = proj[:, 3 * a:3 * a + POOL_WIDTH]
    qm_ref[...] = (proj[:, 3 * a + POOL_WIDTH:] * scale).astype(BF16)

    for dil, refs in ((1, (q1, k1, v1)), (4, (q4, k4, v4)), (16, (q16, k16, v16))):
        for r in range(dil):
            for n, ref in enumerate(refs):
                for c in range(groups):
                    rows = qkv[n * groups + c, pl.ds(r, tm // dil, stride=dil), :] if dil > 1 else qkv[n * groups + c]
                    ref[0, r, :, c * LANES:(c + 1) * LANES] = rows.astype(BF16)


def _in_proj(x2, g, w_bf, cos, sin, batch, seq):
    t, d = x2.shape
    tm = 512
    tiles_per_seq = seq // tm
    cols = w_bf.shape[1]
    row = lambda i: (i, 0)
    fixed = lambda i: (0, 0)
    cls = lambda i: (i // tiles_per_seq, 0, i % tiles_per_seq, 0)
    cls_specs, cls_shapes = [], []
    for dil in DILATIONS:
        for _ in range(3):
            cls_specs.append(pl.BlockSpec((1, dil, tm // dil, ATTN_WIDTH), cls))
            cls_shapes.append(jax.ShapeDtypeStruct((batch, dil, seq // dil, ATTN_WIDTH), BF16))
    outs = pl.pallas_call(
        functools.partial(_in_proj_kernel, tm=tm),
        grid=(t // tm,),
        in_specs=[pl.BlockSpec((tm, d), row),
                  pl.BlockSpec((1, d), fixed),
                  pl.BlockSpec((d, cols), fixed),
                  pl.BlockSpec((tm, LANES), lambda i: (i % tiles_per_seq, 0)),
                  pl.BlockSpec((tm, LANES), lambda i: (i % tiles_per_seq, 0))],
        out_specs=cls_specs + [pl.BlockSpec((tm, POOL_WIDTH), row), pl.BlockSpec((tm, MEM_WIDTH), row)],
        out_shape=cls_shapes + [jax.ShapeDtypeStruct((t, POOL_WIDTH), F32),
                                jax.ShapeDtypeStruct((t, MEM_WIDTH), BF16)],
        scratch_shapes=[pltpu.VMEM((3 * ATTN_WIDTH // LANES, tm, LANES), F32)],
        compiler_params=_params(("parallel",)),
        name="in_proj",
    )(x2, g, w_bf, cos, sin)
    qkv = [outs[3 * n:3 * n + 3] for n in range(len(DILATIONS))]
    return qkv, outs[-2], outs[-1]


def _head_pair_attention(q, k, v, valid):
    lane = lax.broadcasted_iota(jnp.int32, q.shape, 1)
    outs, lses = [], []
    for half in range(2):
        mine = (lane // HEAD_DIM) == half
        s = lax.dot_general(jnp.where(mine, q, jnp.zeros_like(q)), k, _NT,
                            preferred_element_type=F32)
        if valid is not None:
            s = jnp.where(valid, s, NEG_INF)
        m = jnp.max(s, axis=-1, keepdims=True)
        p = jnp.exp(s - m)
        den = jnp.sum(p, axis=-1, keepdims=True)
        pv = jnp.dot(p.astype(BF16), v, preferred_element_type=F32)
        outs.append(pv / den)
        lses.append(m + jnp.log(den))
    return jnp.where((lane // HEAD_DIM) == 0, outs[0], outs[1]), lses


def _attn_kernel(q_ref, kp_ref, kc_ref, kn_ref, vp_ref, vc_ref, vn_ref, o_ref, lse_ref, *, tq, sub, length):
    j = pl.program_id(2)
    nk = sub + 2 * BAND
    row = lax.broadcasted_iota(jnp.int32, (sub, nk), 0)
    col = lax.broadcasted_iota(jnp.int32, (sub, nk), 1)
    in_band = jnp.abs(col - BAND - row) <= BAND
    lane = lax.broadcasted_iota(jnp.int32, (sub, LANES), 1)
    for c in range(ATTN_WIDTH // LANES):
        sl = slice(c * LANES, (c + 1) * LANES)
        k = jnp.concatenate([kp_ref[0, 0, :, sl], kc_ref[0, 0, :, sl], kn_ref[0, 0, :, sl]], axis=0)
        v = jnp.concatenate([vp_ref[0, 0, :, sl], vc_ref[0, 0, :, sl], vn_ref[0, 0, :, sl]], axis=0)
        for s in range(tq // sub):
            rows = slice(s * sub, (s + 1) * sub)
            key = j * tq + s * sub - BAND + col
            valid = in_band & (key >= 0) & (key < length)
            o, lses = _head_pair_attention(q_ref[0, 0, rows, sl], k[s * sub:s * sub + nk], v[s * sub:s * sub + nk],
                                           valid)
            o_ref[0, 0, rows, sl] = o
            lse_tile = jnp.where(lane == 2 * c, lses[0], lses[1])
            if c == 0:
                lse_ref[0, 0, rows, :] = lse_tile
            else:
                lse_ref[0, 0, rows, :] = jnp.where((lane // 2) == c, lse_tile, lse_ref[0, 0, rows, :])


def _dilated_attention(q, k, v):
    batch, dilation, length, _ = q.shape
    tq = 512
    per = tq // BAND
    n_band_blocks = length // BAND
    cur = lambda b, r, j: (b, r, j, 0)
    prev = lambda b, r, j: (b, r, jnp.maximum(j * per - 1, 0), 0)
    nxt = lambda b, r, j: (b, r, jnp.minimum((j + 1) * per, n_band_blocks - 1), 0)
    big = lambda imap: pl.BlockSpec((1, 1, tq, ATTN_WIDTH), imap)
    halo = lambda imap: pl.BlockSpec((1, 1, BAND, ATTN_WIDTH), imap)
    return pl.pallas_call(
        functools.partial(_attn_kernel, tq=tq, sub=128, length=length),
        grid=(batch, dilation, length // tq),
        in_specs=[big(cur), halo(prev), big(cur), halo(nxt), halo(prev), big(cur), halo(nxt)],
        out_specs=[pl.BlockSpec((1, 1, tq, ATTN_WIDTH), cur), pl.BlockSpec((1, 1, tq, LANES), cur)],
        out_shape=[jax.ShapeDtypeStruct((batch, dilation, length, ATTN_WIDTH), F32),
                   jax.ShapeDtypeStruct((batch, dilation, length, LANES), F32)],
        compiler_params=_params(("parallel", "parallel", "parallel")),
        name=f"dilated_attn_d{dilation}",
    )(q, k, k, k, v, v, v)


def _mem_kv_kernel(m_ref, g_ref, w_ref, kv_ref):
    h = _rms(m_ref[...], g_ref[...]).astype(BF16)
    kv_ref[...] = jnp.dot(h, w_ref[...], preferred_element_type=F32).astype(BF16)


def _mem_kv(mem2, g, w_bf):
    n, d = mem2.shape
    cols = w_bf.shape[1]
    tm = 256
    return pl.pallas_call(
        _mem_kv_kernel,
        grid=(n // tm,),
        in_specs=[pl.BlockSpec((tm, d), lambda i: (i, 0)),
                  pl.BlockSpec((1, d), lambda i: (0, 0)),
                  pl.BlockSpec((d, cols), lambda i: (0, 0))],
        out_specs=pl.BlockSpec((tm, cols), lambda i: (i, 0)),
        out_shape=jax.ShapeDtypeStruct((n, cols), BF16),
        compiler_params=_params(("parallel",)),
        name="mem_kv",
    )(mem2, g, w_bf)


def _mix_out_kernel(o1_ref, o4_ref, o16_ref, l1_ref, l4_ref, l16_ref, up_ref, u_ref, un_ref, qm_ref, kv_ref,
                    x_ref, pw_ref, ps_ref, gg_ref, wo_ref, n2_ref, rwh_ref, rwl_ref, rb_ref,
                    x1_ref, h2_ref, idx_ref, gate_ref, rank_ref, cnt_ref, base_ref,
                    o4_s, o16_s, l4_s, l16_s, carry, *, tm, seq):
    i = pl.program_id(0)
    tiles_per_seq = seq // tm
    pos0 = (i % tiles_per_seq) * tm

    for dil, src, dst in ((4, o4_ref, o4_s), (16, o16_ref, o16_s), (4, l4_ref, l4_s), (16, l16_ref, l16_s)):
        for r in range(dil):
            for c in range(dst.shape[0]):
                dst[c, pl.ds(r, tm // dil, stride=dil), :] = src[0, r, :, c * LANES:(c + 1) * LANES]

    l1, l2, l3 = l1_ref[0, 0], l4_s[0], l16_s[0]
    lm = jnp.maximum(jnp.maximum(l1, l2), l3)
    e1, e2, e3 = jnp.exp(l1 - lm), jnp.exp(l2 - lm), jnp.exp(l3 - lm)
    es = e1 + e2 + e3
    w1, w2, w3 = e1 / es, e2 / es, e3 / es
    ya = []
    for c in range(ATTN_WIDTH // LANES):
        sl = slice(c * LANES, (c + 1) * LANES)
        lane = lax.broadcasted_iota(jnp.int32, (tm, LANES), 1)
        lo = lane < HEAD_DIM

        def per_head(w):
            return jnp.where(lo, w[:, 2 * c:2 * c + 1], w[:, 2 * c + 1:2 * c + 2])

        ya.append(per_head(w1) * o1_ref[0, 0, :, sl] + per_head(w2) * o4_s[c] + per_head(w3) * o16_s[c])
    ya = jnp.concatenate(ya, axis=1)

    u = u_ref[...]
    before = jnp.where(pos0 > 0, up_ref[...], 0.0)
    after = jnp.where(pos0 + tm < seq, un_ref[...], 0.0)
    ext = jnp.concatenate([before, u, after], axis=0)
    n_ext = tm + 2 * POOL_HALO
    sums = []
    acc = ext
    shift = 1
    for w in POOL_WINDOWS:
        if w == 2:
            acc = pltpu.roll(ext, 1, 0) + ext
        else:
            acc = pltpu.roll(acc, shift, 0) + pltpu.roll(acc, n_ext - shift, 0)
            shift *= 2
        sums.append(acc[POOL_HALO:POOL_HALO + tm])
    pos = pos0 + lax.broadcasted_iota(jnp.int32, (tm, POOL_WIDTH), 0)
    grp = lax.broadcasted_iota(jnp.int32, (tm, POOL_WIDTH), 1) // POOL_GROUP
    mean = jnp.zeros((tm, POOL_WIDTH), F32)
    for g, w in enumerate(POOL_WINDOWS):
        cnt = jnp.minimum(pos + (w - 1 - w // 2), seq - 1) + 1 - jnp.maximum(pos - w // 2, 0)
        mean = jnp.where(grp == g, sums[g] / cnt.astype(F32), mean)
    d = (mean - u).astype(BF16)
    yp = jnp.dot(d, pw_ref[...], preferred_element_type=F32) * ps_ref[...]

    ym = []
    for c in range(MEM_WIDTH // LANES):
        sl = slice(c * LANES, (c + 1) * LANES)
        o, _ = _head_pair_attention(qm_ref[:, sl], kv_ref[0, :, sl],
                                    kv_ref[0, :, MEM_WIDTH + c * LANES:MEM_WIDTH + (c + 1) * LANES], None)
        ym.append(o)
    ym = jnp.concatenate(ym, axis=1)

    gg = gg_ref[...]
    a, p = ATTN_WIDTH, POOL_WIDTH
    y = jnp.concatenate([_rms(ya, gg[:, :a]), _rms(yp, gg[:, a:a + p]), _rms(ym, gg[:, a + p:])], axis=1)
    x1 = x_ref[...] + jnp.dot(y.astype(BF16), wo_ref[...], preferred_element_type=F32)
    x1_ref[...] = x1

    h2 = _rms(x1, n2_ref[...])
    _to_row_tiles(h2_ref, h2)

    hi = h2.astype(BF16)
    lo = (h2 - hi.astype(F32)).astype(BF16)
    logits = (lax.dot_general(rwh_ref[...], hi, _NT, preferred_element_type=F32)
              + lax.dot_general(rwh_ref[...], lo, _NT, preferred_element_type=F32)
              + lax.dot_general(rwl_ref[...], hi, _NT, preferred_element_type=F32)) + rb_ref[...]
    eidx = lax.broadcasted_iota(jnp.int32, logits.shape, 0)
    krow = lax.broadcasted_iota(jnp.int32, (TOP_K, tm), 0)
    vals = jnp.zeros((TOP_K, tm), F32)
    idxs = jnp.zeros((TOP_K, tm), jnp.int32)
    work = logits
    args = []
    for kk in range(TOP_K):
        best = jnp.max(work, axis=0, keepdims=True)
        arg = jnp.min(jnp.where(work == best, eidx, N_EXPERTS), axis=0, keepdims=True)
        vals = jnp.where(krow == kk, best, vals)
        idxs = jnp.where(krow == kk, arg, idxs)
        work = jnp.where(eidx == arg, -jnp.inf, work)
        args.append(arg)
    ex = jnp.exp(vals - vals[0:1])
    gate_ref[...] = ex / jnp.sum(ex, axis=0, keepdims=True)
    idx_ref[...] = idxs

    @pl.when(i == 0)
    def _():
        carry[...] = jnp.zeros_like(carry)

    chosen = (work == -jnp.inf).astype(BF16)
    earlier = (lax.broadcasted_iota(jnp.int32, (tm, tm), 0)
               < lax.broadcasted_iota(jnp.int32, (tm, tm), 1)).astype(BF16)
    before_me = jnp.dot(chosen, earlier, preferred_element_type=F32) + carry[...]
    ranks = jnp.zeros((TOP_K, tm), F32)
    for kk in range(TOP_K):
        ranks = jnp.where(krow == kk, jnp.sum(jnp.where(eidx == args[kk], before_me, 0.0), axis=0, keepdims=True),
                          ranks)
    rank_ref[...] = ranks.astype(jnp.int32)
    base_ref[0] = carry[...].astype(jnp.int32)
    carry[...] = carry[...] + jnp.sum(chosen.astype(F32), axis=1, keepdims=True)
    cnt_ref[...] = carry[...].astype(jnp.int32)


def _mix_out(attn, u, qm, kv, x2, pw_bd, ps, gg, wo_bf, n2, rw_hi, rw_lo, rb, batch, seq):
    t, d = x2.shape
    tm = ROUTER_TILE
    hp = tm // POOL_HALO
    n_halo = t // POOL_HALO
    tiles_per_seq = seq // tm
    row = lambda i: (i, 0)
    fixed = lambda i: (0, 0)
    cls = lambda i: (i // tiles_per_seq, 0, i % tiles_per_seq, 0)
    rowspec = lambda w: pl.BlockSpec((tm, w), row)
    clsspec = lambda dil, w: pl.BlockSpec((1, dil, tm // dil, w), cls)
    tok_cols = lambda i: (0, i)
    (o1, l1), (o4, l4), (o16, l16) = attn
    return pl.pallas_call(
        functools.partial(_mix_out_kernel, tm=tm, seq=seq),
        grid=(t // tm,),
        in_specs=[clsspec(1, ATTN_WIDTH), clsspec(4, ATTN_WIDTH), clsspec(16, ATTN_WIDTH),
                  clsspec(1, LANES), clsspec(4, LANES), clsspec(16, LANES),
                  pl.BlockSpec((POOL_HALO, POOL_WIDTH), lambda i: (jnp.maximum(i * hp - 1, 0), 0)),
                  rowspec(POOL_WIDTH),
                  pl.BlockSpec((POOL_HALO, POOL_WIDTH), lambda i: (jnp.minimum((i + 1) * hp, n_halo - 1), 0)),
                  rowspec(MEM_WIDTH),
                  pl.BlockSpec((1, kv.shape[1], kv.shape[2]), lambda i: (i // tiles_per_seq, 0, 0)),
                  rowspec(d),
                  pl.BlockSpec(pw_bd.shape, fixed),
                  pl.BlockSpec(ps.shape, fixed),
                  pl.BlockSpec(gg.shape, fixed),
                  pl.BlockSpec(wo_bf.shape, fixed),
                  pl.BlockSpec(n2.shape, fixed),
                  pl.BlockSpec(rw_hi.shape, fixed),
                  pl.BlockSpec(rw_lo.shape, fixed),
                  pl.BlockSpec(rb.shape, fixed)],
        out_specs=[rowspec(d),
                   pl.BlockSpec((tm * ROW_TILE, LANES), row),
                   pl.BlockSpec((TOP_K, tm), tok_cols),
                   pl.BlockSpec((TOP_K, tm), tok_cols),
                   pl.BlockSpec((TOP_K, tm), tok_cols),
                   pl.BlockSpec((N_EXPERTS, 1), fixed),
                   pl.BlockSpec((1, N_EXPERTS, 1), lambda i: (i, 0, 0))],
        out_shape=[jax.ShapeDtypeStruct((t, d), F32),
                   jax.ShapeDtypeStruct((t * ROW_TILE, LANES), F32),
                   jax.ShapeDtypeStruct((TOP_K, t), jnp.int32),
                   jax.ShapeDtypeStruct((TOP_K, t), F32),
                   jax.ShapeDtypeStruct((TOP_K, t), jnp.int32),
                   jax.ShapeDtypeStruct((N_EXPERTS, 1), jnp.int32),
                   jax.ShapeDtypeStruct((t // tm, N_EXPERTS, 1), jnp.int32)],
        scratch_shapes=[pltpu.VMEM((ATTN_WIDTH // LANES, tm, LANES), F32),
                        pltpu.VMEM((ATTN_WIDTH // LANES, tm, LANES), F32),
                        pltpu.VMEM((1, tm, LANES), F32), pltpu.VMEM((1, tm, LANES), F32),
                        pltpu.VMEM((N_EXPERTS, 1), F32)],
        compiler_params=_params(("arbitrary",)),
        name="mix_out_router",
    )(o1, o4, o16, l1, l4, l16, u, u, u, qm, kv, x2, pw_bd, ps, gg, wo_bf, n2, rw_hi, rw_lo, rb)


def _dispatch_kernel(dest_ref, pend_ref, padded_ref, nb_ref, h_ref, xs_hbm, zeros, sem, *, tm, n_tok, n_blocks):
    i = pl.program_id(0)
    blk = MOE_BLOCK * ROW_TILE

    @pl.when(i == 0)
    def _():
        zeros[...] = jnp.zeros_like(zeros)

        def clear(block):
            return pltpu.make_async_copy(zeros, xs_hbm.at[pl.ds(pl.multiple_of(block * blk, blk), blk), :], sem)

        def for_each_cleared_block(fn):
            for e in range(N_EXPERTS):
                @pl.when(padded_ref[e] > 0)
                def _():
                    fn(clear(pend_ref[e] // MOE_BLOCK - 1))

                @pl.when(nb_ref[0] + e < n_blocks)
                def _():
                    fn(clear(nb_ref[0] + e))

        for_each_cleared_block(lambda c: c.start())
        for_each_cleared_block(lambda c: c.wait())

    def start(r, c):
        src = h_ref.at[pl.ds(pl.multiple_of(r * ROW_TILE, ROW_TILE), ROW_TILE), :]
        for kk in range(TOP_K):
            slot = dest_ref[kk * n_tok + i * tm + r]
            pltpu.make_async_copy(src, xs_hbm.at[pl.ds(pl.multiple_of(slot * ROW_TILE, ROW_TILE), ROW_TILE), :],
                                  sem).start()
        return c

    lax.fori_loop(0, tm, start, 0, unroll=4)
    n = tm * TOP_K * ROW_TILE
    pltpu.make_async_copy(xs_hbm.at[pl.ds(0, n), :], xs_hbm.at[pl.ds(0, n), :], sem).wait()


def _dispatch(dest, pad_end, padded, n_used, h2_tiles, n_slots):
    n_tok = h2_tiles.shape[0] // ROW_TILE
    tm = 512
    return pl.pallas_call(
        functools.partial(_dispatch_kernel, tm=tm, n_tok=n_tok, n_blocks=n_slots // MOE_BLOCK),
        grid_spec=pltpu.PrefetchScalarGridSpec(
            num_scalar_prefetch=4,
            grid=(n_tok // tm,),
            in_specs=[pl.BlockSpec((tm * ROW_TILE, LANES), lambda i, *_: (i, 0))],
            out_specs=pl.BlockSpec(memory_space=pl.ANY),
            scratch_shapes=[pltpu.VMEM((MOE_BLOCK * ROW_TILE, LANES), F32), pltpu.SemaphoreType.DMA(())]),
        out_shape=jax.ShapeDtypeStruct((n_slots * ROW_TILE, LANES), F32),
        compiler_params=_params(("arbitrary",)),
        name="moe_dispatch",
    )(dest, pad_end, padded, n_used, h2_tiles)


def _expert_kernel(first_ref, nblk_ref, nb_ref, xs_hbm, wgu_ref, bgu_ref, wdn_ref, bdn_ref, y_hbm,
                   wgu_bf, wdn_bf, xbuf, ybuf, zeros, sem_in, sem_out, *, n_blocks):
    e = pl.program_id(0)
    first = first_ref[e]
    n = nblk_ref[e]
    rows = MOE_BLOCK * ROW_TILE

    def load(j, s):
        return pltpu.make_async_copy(xs_hbm.at[pl.ds(pl.multiple_of((first + j) * rows, rows), rows), :],
                                     xbuf.at[s], sem_in.at[s])

    def store(j, s):
        return pltpu.make_async_copy(ybuf.at[s], y_hbm.at[pl.ds(pl.multiple_of((first + j) * MOE_BLOCK, MOE_BLOCK),
                                                                 MOE_BLOCK), :], sem_out.at[s])

    @pl.when(n > 0)
    def _():
        load(0, 0).start()
        wgu_bf[...] = wgu_ref[0, 0].astype(BF16)
        wdn_bf[...] = wdn_ref[0, 0].astype(BF16)

        def block(j, carry):
            s = j % 2
            load(j, s).wait()

            @pl.when(j + 1 < n)
            def _():
                load(j + 1, 1 - s).start()

            x = _from_row_tiles(xbuf.at[s], 0, MOE_BLOCK).astype(BF16)
            gu = jnp.dot(x, wgu_bf[...], preferred_element_type=F32) + bgu_ref[0, 0]
            gate = jnp.minimum(gu[:, :D_EXPERT], SWIGLU_LIMIT)
            up = jnp.clip(gu[:, D_EXPERT:], -SWIGLU_LIMIT, SWIGLU_LIMIT)
            act = (up + 1.0) * gate * jax.nn.sigmoid(SWIGLU_ALPHA * gate)
            y = jnp.dot(act.astype(BF16), wdn_bf[...], preferred_element_type=F32) + bdn_ref[0, 0]

            @pl.when(j >= 2)
            def _():
                store(j - 2, s).wait()

            ybuf[s] = y.astype(BF16)
            store(j, s).start()
            return carry

        lax.fori_loop(0, n, block, 0)

        @pl.when(n >= 2)
        def _():
            store(n - 2, n % 2).wait()

        store(n - 1, (n - 1) % 2).wait()

    @pl.when(e == N_EXPERTS - 1)
    def _():
        zeros[...] = jnp.zeros_like(zeros)

        def clear(k):
            blk = pl.multiple_of((nb_ref[0] + k) * MOE_BLOCK, MOE_BLOCK)
            return pltpu.make_async_copy(zeros, y_hbm.at[pl.ds(blk, MOE_BLOCK), :], sem_out.at[0])

        for k in range(N_EXPERTS):
            @pl.when(nb_ref[0] + k < n_blocks)
            def _():
                clear(k).start()
        for k in range(N_EXPERTS):
            @pl.when(nb_ref[0] + k < n_blocks)
            def _():
                clear(k).wait()


def _experts(layer, first_block, n_block, n_used, xs_tiles, w_gu, b_gu, w_down, b_down):
    n_slots = xs_tiles.shape[0] // ROW_TILE
    d = w_gu.shape[2]
    de2 = w_gu.shape[3]
    by_expert = lambda e, *_: (layer, e, 0, 0)
    return pl.pallas_call(
        functools.partial(_expert_kernel, n_blocks=n_slots // MOE_BLOCK),
        grid_spec=pltpu.PrefetchScalarGridSpec(
            num_scalar_prefetch=3,
            grid=(N_EXPERTS,),
            in_specs=[pl.BlockSpec(memory_space=pl.ANY),
                      pl.BlockSpec((1, 1, d, de2), by_expert),
                      pl.BlockSpec((1, 1, 1, de2), by_expert),
                      pl.BlockSpec((1, 1, D_EXPERT, d), by_expert),
                      pl.BlockSpec((1, 1, 1, d), by_expert)],
            out_specs=pl.BlockSpec(memory_space=pl.ANY),
            scratch_shapes=[pltpu.VMEM((d, de2), BF16), pltpu.VMEM((D_EXPERT, d), BF16),
                            pltpu.VMEM((2, MOE_BLOCK * ROW_TILE, LANES), F32),
                            pltpu.VMEM((2, MOE_BLOCK, d), BF16),
                            pltpu.VMEM((MOE_BLOCK, d), BF16),
                            pltpu.SemaphoreType.DMA((2,)), pltpu.SemaphoreType.DMA((2,))]),
        out_shape=jax.ShapeDtypeStruct((n_slots, d), BF16),
        compiler_params=_params(("arbitrary",)),
        name="moe_experts",
    )(first_block, n_block, n_used, xs_tiles, w_gu, b_gu.reshape(b_gu.shape[0], N_EXPERTS, 1, de2), w_down,
      b_down.reshape(b_down.shape[0], N_EXPERTS, 1, d))


CHUNK = 16
COMBINE_ROWS = ROUTER_TILE * TOP_K + 2 * N_EXPERTS * CHUNK


def _combine_kernel(a_ref, nch_ref, tot_ref, y_hbm, x1_ref, col_ref, gate_ref, g_ref, out_ref, ybuf, sems,
                    *, tm, n_tiles, final):
    i = pl.program_id(0)
    slot = i % 2

    @pl.when(i == 0)
    def _():
        ybuf[...] = jnp.zeros_like(ybuf)

    def fetch(tile, buf):
        pos = jnp.int32(0)
        for e in range(N_EXPERTS):
            a = a_ref[tile * N_EXPERTS + e]

            def one(c, pos):
                pltpu.make_async_copy(y_hbm.at[pl.ds(pl.multiple_of(a + c * CHUNK, CHUNK), CHUNK), :],
                                      ybuf.at[buf, pl.ds(pl.multiple_of(pos, CHUNK), CHUNK), :],
                                      sems.at[buf]).start()
                return pos + CHUNK

            pos = lax.fori_loop(0, nch_ref[tile * N_EXPERTS + e], one, pos)

    @pl.when(i == 0)
    def _():
        fetch(0, 0)

    @pl.when(i + 1 < n_tiles)
    def _():
        fetch(i + 1, 1 - slot)

    col = col_ref[...]
    gates = gate_ref[...]
    col_id = lax.broadcasted_iota(jnp.int32, (tm, COMBINE_ROWS), 1)
    g = jnp.zeros((tm, COMBINE_ROWS), F32)
    for kk in range(TOP_K):
        g = jnp.where(col_id == col[:, kk:kk + 1], gates[:, kk:kk + 1], g)

    def wait_one(c, carry):
        pltpu.make_async_copy(y_hbm.at[pl.ds(0, CHUNK), :], ybuf.at[slot, pl.ds(0, CHUNK), :], sems.at[slot]).wait()
        return carry

    lax.fori_loop(0, tot_ref[i], wait_one, 0)
    x = x1_ref[...] + jnp.dot(g.astype(BF16), ybuf[slot], preferred_element_type=F32)
    out_ref[...] = _rms(x, g_ref[...]) if final else x


def _combine(seg_a, seg_nch, tile_chunks, y_slots, x1, col_tk, gates_tk, final_g, final):
    t, d = x1.shape
    tm = ROUTER_TILE
    n_tiles = t // tm
    tok = lambda i, *_: (i, 0)
    return pl.pallas_call(
        functools.partial(_combine_kernel, tm=tm, n_tiles=n_tiles, final=final),
        grid_spec=pltpu.PrefetchScalarGridSpec(
            num_scalar_prefetch=3,
            grid=(n_tiles,),
            in_specs=[pl.BlockSpec(memory_space=pl.ANY),
                      pl.BlockSpec((tm, d), tok),
                      pl.BlockSpec((tm, TOP_K), tok),
                      pl.BlockSpec((tm, TOP_K), tok),
                      pl.BlockSpec((1, d), lambda i, *_: (0, 0))],
            out_specs=pl.BlockSpec((tm, d), tok),
            scratch_shapes=[pltpu.VMEM((2, COMBINE_ROWS, d), BF16), pltpu.SemaphoreType.DMA((2,))]),
        out_shape=jax.ShapeDtypeStruct((t, d), F32),
        compiler_params=_params(("arbitrary",)),
        name="moe_combine",
    )(seg_a, seg_nch, tile_chunks, y_slots, x1, col_tk, gates_tk, final_g)


def _slot_layout(top_idx, rank, counts, tile_base):
    counts = counts.reshape(-1)
    padded = (counts + MOE_BLOCK - 1) // MOE_BLOCK * MOE_BLOCK
    pad_end = jnp.cumsum(padded).astype(jnp.int32)
    pad_start = pad_end - padded
    experts = jnp.arange(N_EXPERTS, dtype=jnp.int32)
    start_of = jnp.sum(jnp.where(top_idx[..., None] == experts, pad_start, 0), axis=-1)
    dest = (start_of + rank).astype(jnp.int32)
    n_used = pad_end[-1] // MOE_BLOCK
    base = tile_base.reshape(-1, N_EXPERTS)
    run_start = pad_start[None, :] + base
    run_len = jnp.concatenate([base[1:], counts[None, :]], axis=0) - base
    seg_a = run_start // CHUNK * CHUNK
    seg_nch = jnp.where(run_len > 0, (run_start + run_len - seg_a + CHUNK - 1) // CHUNK, 0)
    buf_row0 = (jnp.cumsum(seg_nch, axis=1) - seg_nch) * CHUNK
    shift = jnp.repeat(buf_row0 - seg_a, ROUTER_TILE, axis=0)
    col = dest + jnp.sum(jnp.where(top_idx[..., None] == experts, shift[None], 0), axis=-1)
    flat = lambda a: a.reshape(-1).astype(jnp.int32)
    return (dest, pad_end, padded.astype(jnp.int32), n_used.reshape(1).astype(jnp.int32),
            flat(seg_a), flat(seg_nch), flat(jnp.sum(seg_nch, axis=1)), col.astype(jnp.int32))


def _rope_tables(seq):
    half = HEAD_DIM // 2
    inv_freq = ROPE_THETA ** (-jnp.arange(half, dtype=F32) / half)
    ang = jnp.arange(seq, dtype=F32)[:, None] * inv_freq[None, :]
    cos, sin = jnp.cos(ang), jnp.sin(ang)
    reps = LANES // HEAD_DIM
    cos_l = jnp.tile(jnp.concatenate([cos, cos], axis=1), (1, reps))
    sin_l = jnp.tile(jnp.concatenate([-sin, sin], axis=1), (1, reps))
    return cos_l, sin_l


def kernel(x, mem, norm1_g, w_in, pool_w, pool_scale, mem_norm_g, w_mem_kv, grp_norm_g, w_out, norm2_g,
           router_w, router_b, w_gu, b_gu, w_down, b_down, final_g):
    batch, seq, d = x.shape
    depth = w_in.shape[0]
    n_mem = mem.shape[1]
    t = batch * seq
    n_blocks = t * TOP_K // MOE_BLOCK + N_EXPERTS
    cos_l, sin_l = _rope_tables(seq)
    x2 = x.reshape(t, d)
    mem2 = mem.reshape(batch * n_mem, d)
    row = lambda a: a.reshape(1, -1)
    for l in range(depth):
        qkv, u, qm = _in_proj(x2, row(norm1_g[l]), w_in[l].astype(BF16), cos_l, sin_l, batch, seq)
        attn = [_dilated_attention(*qkv[n]) for n in range(len(DILATIONS))]
        kv = _mem_kv(mem2, row(mem_norm_g[l]), w_mem_kv[l].astype(BF16)).reshape(batch, n_mem, 2 * MEM_WIDTH)
        pw_bd = jax.scipy.linalg.block_diag(*[pool_w[l, g] for g in range(len(POOL_WINDOWS))]).astype(BF16)
        rw_t = router_w[l].T
        rw_hi = rw_t.astype(BF16)
        rw_lo = (rw_t - rw_hi.astype(F32)).astype(BF16)
        x1, h2_tiles, top_idx, gates, rank, counts, tile_base = _mix_out(
            attn, u, qm, kv, x2, pw_bd, row(pool_scale[l]), row(grp_norm_g[l]), w_out[l].astype(BF16),
            row(norm2_g[l]), rw_hi, rw_lo, router_b[l].reshape(-1, 1), batch, seq)
        dest, pad_end, padded, n_used, seg_a, seg_nch, tile_chunks, col = _slot_layout(
            top_idx, rank, counts, tile_base)
        xs_tiles = _dispatch(dest.reshape(-1), pad_end, padded, n_used, h2_tiles, n_blocks * MOE_BLOCK)
        y_slots = _experts(l, (pad_end - padded) // MOE_BLOCK, padded // MOE_BLOCK, n_used, xs_tiles,
                           w_gu, b_gu, w_down, b_down)
        x2 = _combine(seg_a, seg_nch, tile_chunks, y_slots, x1, col.T, gates.T, row(final_g),
                      final=(l == depth - 1))
    return x2.reshape(batch, seq, d)
```

```python
import functools

import jax
import jax.numpy as jnp
from jax import lax
from jax.experimental import pallas as pl
from jax.experimental.pallas import tpu as pltpu

D_MODEL = 1024
HEAD_DIM = 64
ATTN_WIDTH = 512
DILATIONS = (1, 4, 16)
BAND = 64
ROPE_THETA = 10000.0
POOL_WINDOWS = (2, 4, 8, 16)
POOL_WIDTH = 256
POOL_GROUP = 64
POOL_HALO = 8
MEM_WIDTH = 256
N_EXPERTS = 32
TOP_K = 4
D_EXPERT = 1024
SWIGLU_ALPHA = 1.702
SWIGLU_LIMIT = 7.0
MOE_BLOCK = 256
ROUTER_TILE = 256
NORM_EPS = 1e-5
NEG_INF = -1e30
LANES = 128
SUBLANES = 8
ROW_TILE = D_MODEL // LANES

F32 = jnp.float32
BF16 = jnp.bfloat16
VMEM_LIMIT = 56 * 1024 * 1024

_NT = (((1,), (1,)), ((), ()))


def _params(sem, vmem=VMEM_LIMIT):
    return pltpu.CompilerParams(dimension_semantics=sem, vmem_limit_bytes=vmem)


def _rms(x, g):
    return x * lax.rsqrt(jnp.mean(x * x, axis=-1, keepdims=True) + NORM_EPS) * g


def _to_row_tiles(ref, val):
    m = val.shape[0]
    for s in range(ROW_TILE):
        ref[pl.ds(s, m, stride=ROW_TILE), :] = val[:, s * LANES:(s + 1) * LANES]


def _from_row_tiles(ref, start, m):
    return jnp.concatenate([ref[pl.ds(start * ROW_TILE + s, m, stride=ROW_TILE), :] for s in range(ROW_TILE)],
                           axis=1)


def _in_proj_kernel(x_ref, g_ref, w_ref, cos_ref, sin_ref, q1, k1, v1, q4, k4, v4, q16, k16, v16, u_ref, qm_ref,
                    qkv, cls4, *, tm):
    h = _rms(x_ref[...], g_ref[...]).astype(BF16)
    proj = jnp.dot(h, w_ref[...], preferred_element_type=F32)
    cos = cos_ref[...]
    sin = sin_ref[...]
    lane = lax.broadcasted_iota(jnp.int32, cos.shape, 1)
    first_half = (lane % HEAD_DIM) < (HEAD_DIM // 2)
    scale = HEAD_DIM ** -0.5

    def rope(t):
        partner = jnp.where(first_half, pltpu.roll(t, LANES - HEAD_DIM // 2, 1),
                            pltpu.roll(t, HEAD_DIM // 2, 1))
        return t * cos + partner * sin

    a = ATTN_WIDTH
    groups = a // LANES
    for c in range(groups):
        qkv[c] = rope(proj[:, c * LANES:(c + 1) * LANES]) * scale
        qkv[groups + c] = rope(proj[:, a + c * LANES:a + (c + 1) * LANES])
        qkv[2 * groups + c] = proj[:, 2 * a + c * LANES:2 * a + (c + 1) * LANES]
    u_ref[...] = proj[:, 3 * a:3 * a + POOL_WIDTH]
    qm_ref[...] = (proj[:, 3 * a + POOL_WIDTH:] * scale).astype(BF16)

    for n, (r1, r4, r16) in enumerate(((q1, q4, q16), (k1, k4, k16), (v1, v4, v16))):
        for c in range(groups):
            g = n * groups + c
            cols = slice(c * LANES, (c + 1) * LANES)
            r1[0, 0, :, cols] = qkv[g].astype(BF16)
            for a in range(4):
                rows = qkv[g, pl.ds(a, tm // 4, stride=4), :]
                r4[0, a, :, cols] = rows.astype(BF16)
                cls4[g * 4 + a] = rows
            for a in range(4):
                for b in range(4):
                    r16[0, a + 4 * b, :, cols] = cls4[g * 4 + a, pl.ds(b, tm // 16, stride=4), :].astype(BF16)


def _in_proj(x2, g, w_bf, cos, sin, batch, seq):
    t, d = x2.shape
    tm = 512
    tiles_per_seq = seq // tm
    cols = w_bf.shape[1]
    row = lambda i: (i, 0)
    fixed = lambda i: (0, 0)
    cls = lambda i: (i // tiles_per_seq, 0, i % tiles_per_seq, 0)
    cls_specs, cls_shapes = [], []
    for dil in DILATIONS:
        for _ in range(3):
            cls_specs.append(pl.BlockSpec((1, dil, tm // dil, ATTN_WIDTH), cls))
            cls_shapes.append(jax.ShapeDtypeStruct((batch, dil, seq // dil, ATTN_WIDTH), BF16))
    outs = pl.pallas_call(
        functools.partial(_in_proj_kernel, tm=tm),
        grid=(t // tm,),
        in_specs=[pl.BlockSpec((tm, d), row),
                  pl.BlockSpec((1, d), fixed),
                  pl.BlockSpec((d, cols), fixed),
                  pl.BlockSpec((tm, LANES), lambda i: (i % tiles_per_seq, 0)),
                  pl.BlockSpec((tm, LANES), lambda i: (i % tiles_per_seq, 0))],
        out_specs=cls_specs + [pl.BlockSpec((tm, POOL_WIDTH), row), pl.BlockSpec((tm, MEM_WIDTH), row)],
        out_shape=cls_shapes + [jax.ShapeDtypeStruct((t, POOL_WIDTH), F32),
                                jax.ShapeDtypeStruct((t, MEM_WIDTH), BF16)],
        scratch_shapes=[pltpu.VMEM((3 * ATTN_WIDTH // LANES, tm, LANES), F32),
                        pltpu.VMEM((4 * 3 * ATTN_WIDTH // LANES, tm // 4, LANES), F32)],
        compiler_params=_params(("parallel",)),
        name="in_proj",
    )(x2, g, w_bf, cos, sin)
    qkv = [outs[3 * n:3 * n + 3] for n in range(len(DILATIONS))]
    return qkv, outs[-2], outs[-1]


def _head_pair_attention(q, k, v, valid):
    lane = lax.broadcasted_iota(jnp.int32, q.shape, 1)
    outs, lses = [], []
    for half in range(2):
        mine = (lane // HEAD_DIM) == half
        s = lax.dot_general(jnp.where(mine, q, jnp.zeros_like(q)), k, _NT,
                            preferred_element_type=F32)
        if valid is not None:
            s = jnp.where(valid, s, NEG_INF)
        m = jnp.max(s, axis=-1, keepdims=True)
        p = jnp.exp(s - m)
        den = jnp.sum(p, axis=-1, keepdims=True)
        pv = jnp.dot(p.astype(BF16), v, preferred_element_type=F32)
        outs.append(pv / den)
        lses.append(m + jnp.log(den))
    return jnp.where((lane // HEAD_DIM) == 0, outs[0], outs[1]), lses


def _attn_kernel(q_ref, kp_ref, kc_ref, kn_ref, vp_ref, vc_ref, vn_ref, o_ref, lse_ref, *, tq, sub, length):
    j = pl.program_id(2)
    nk = sub + 2 * BAND
    row = lax.broadcasted_iota(jnp.int32, (sub, nk), 0)
    col = lax.broadcasted_iota(jnp.int32, (sub, nk), 1)
    in_band = jnp.abs(col - BAND - row) <= BAND
    lane = lax.broadcasted_iota(jnp.int32, (sub, LANES), 1)
    for c in range(ATTN_WIDTH // LANES):
        sl = slice(c * LANES, (c + 1) * LANES)
        k = jnp.concatenate([kp_ref[0, 0, :, sl], kc_ref[0, 0, :, sl], kn_ref[0, 0, :, sl]], axis=0)
        v = jnp.concatenate([vp_ref[0, 0, :, sl], vc_ref[0, 0, :, sl], vn_ref[0, 0, :, sl]], axis=0)
        for s in range(tq // sub):
            rows = slice(s * sub, (s + 1) * sub)
            key = j * tq + s * sub - BAND + col
            valid = in_band & (key >= 0) & (key < length)
            o, lses = _head_pair_attention(q_ref[0, 0, rows, sl], k[s * sub:s * sub + nk], v[s * sub:s * sub + nk],
                                           valid)
            o_ref[0, 0, rows, sl] = o
            lse_tile = jnp.where(lane == 2 * c, lses[0], lses[1])
            if c == 0:
                lse_ref[0, 0, rows, :] = lse_tile
            else:
                lse_ref[0, 0, rows, :] = jnp.where((lane // 2) == c, lse_tile, lse_ref[0, 0, rows, :])


def _dilated_attention(q, k, v):
    batch, dilation, length, _ = q.shape
    tq = 512
    per = tq // BAND
    n_band_blocks = length // BAND
    cur = lambda b, r, j: (b, r, j, 0)
    prev = lambda b, r, j: (b, r, jnp.maximum(j * per - 1, 0), 0)
    nxt = lambda b, r, j: (b, r, jnp.minimum((j + 1) * per, n_band_blocks - 1), 0)
    big = lambda imap: pl.BlockSpec((1, 1, tq, ATTN_WIDTH), imap)
    halo = lambda imap: pl.BlockSpec((1, 1, BAND, ATTN_WIDTH), imap)
    return pl.pallas_call(
        functools.partial(_attn_kernel, tq=tq, sub=128, length=length),
        grid=(batch, dilation, length // tq),
        in_specs=[big(cur), halo(prev), big(cur), halo(nxt), halo(prev), big(cur), halo(nxt)],
        out_specs=[pl.BlockSpec((1, 1, tq, ATTN_WIDTH), cur), pl.BlockSpec((1, 1, tq, LANES), cur)],
        out_shape=[jax.ShapeDtypeStruct((batch, dilation, length, ATTN_WIDTH), F32),
                   jax.ShapeDtypeStruct((batch, dilation, length, LANES), F32)],
        compiler_params=_params(("parallel", "parallel", "parallel")),
        name=f"dilated_attn_d{dilation}",
    )(q, k, k, k, v, v, v)


def _mem_kv_kernel(m_ref, g_ref, w_ref, kv_ref):
    h = _rms(m_ref[...], g_ref[...]).astype(BF16)
    kv_ref[...] = jnp.dot(h, w_ref[...], preferred_element_type=F32).astype(BF16)


def _mem_kv(mem2, g, w_bf):
    n, d = mem2.shape
    cols = w_bf.shape[1]
    tm = 256
    return pl.pallas_call(
        _mem_kv_kernel,
        grid=(n // tm,),
        in_specs=[pl.BlockSpec((tm, d), lambda i: (i, 0)),
                  pl.BlockSpec((1, d), lambda i: (0, 0)),
                  pl.BlockSpec((d, cols), lambda i: (0, 0))],
        out_specs=pl.BlockSpec((tm, cols), lambda i: (i, 0)),
        out_shape=jax.ShapeDtypeStruct((n, cols), BF16),
        compiler_params=_params(("parallel",)),
        name="mem_kv",
    )(mem2, g, w_bf)


def _mix_out_kernel(o1_ref, o4_ref, o16_ref, l1_ref, l4_ref, l16_ref, up_ref, u_ref, un_ref, qm_ref, kv_ref,
                    x_ref, pw_ref, ps_ref, gg_ref, wo_ref, n2_ref, rwh_ref, rwl_ref, rb_ref,
                    x1_ref, h2_ref, idx_ref, gate_ref, rank_ref, cnt_ref, base_ref,
                    o4_s, o16_s, l4_s, l16_s, cls4, carry, *, tm, seq):
    i = pl.program_id(0)
    tiles_per_seq = seq // tm
    pos0 = (i % tiles_per_seq) * tm

    for src4, src16, dst4, dst16 in ((o4_ref, o16_ref, o4_s, o16_s), (l4_ref, l16_ref, l4_s, l16_s)):
        for c in range(dst4.shape[0]):
            cols = slice(c * LANES, (c + 1) * LANES)
            for a in range(4):
                dst4[c, pl.ds(a, tm // 4, stride=4), :] = src4[0, a, :, cols]
                for b in range(4):
                    cls4[a, pl.ds(b, tm // 16, stride=4), :] = src16[0, a + 4 * b, :, cols]
            for a in range(4):
                dst16[c, pl.ds(a, tm // 4, stride=4), :] = cls4[a]

    l1, l2, l3 = l1_ref[0, 0], l4_s[0], l16_s[0]
    lm = jnp.maximum(jnp.maximum(l1, l2), l3)
    e1, e2, e3 = jnp.exp(l1 - lm), jnp.exp(l2 - lm), jnp.exp(l3 - lm)
    es = e1 + e2 + e3
    w1, w2, w3 = e1 / es, e2 / es, e3 / es
    ya = []
    for c in range(ATTN_WIDTH // LANES):
        sl = slice(c * LANES, (c + 1) * LANES)
        lane = lax.broadcasted_iota(jnp.int32, (tm, LANES), 1)
        lo = lane < HEAD_DIM

        def per_head(w):
            return jnp.where(lo, w[:, 2 * c:2 * c + 1], w[:, 2 * c + 1:2 * c + 2])

        ya.append(per_head(w1) * o1_ref[0, 0, :, sl] + per_head(w2) * o4_s[c] + per_head(w3) * o16_s[c])
    ya = jnp.concatenate(ya, axis=1)

    u = u_ref[...]
    before = jnp.where(pos0 > 0, up_ref[...], 0.0)
    after = jnp.where(pos0 + tm < seq, un_ref[...], 0.0)
    ext = jnp.concatenate([before, u, after], axis=0)
    n_ext = tm + 2 * POOL_HALO
    sums = []
    acc = ext
    shift = 1
    for w in POOL_WINDOWS:
        if w == 2:
            acc = pltpu.roll(ext, 1, 0) + ext
        else:
            acc = pltpu.roll(acc, shift, 0) + pltpu.roll(acc, n_ext - shift, 0)
            shift *= 2
        sums.append(acc[POOL_HALO:POOL_HALO + tm])
    pos = pos0 + lax.broadcasted_iota(jnp.int32, (tm, POOL_WIDTH), 0)
    grp = lax.broadcasted_iota(jnp.int32, (tm, POOL_WIDTH), 1) // POOL_GROUP
    mean = jnp.zeros((tm, POOL_WIDTH), F32)
    for g, w in enumerate(POOL_WINDOWS):
        cnt = jnp.minimum(pos + (w - 1 - w // 2), seq - 1) + 1 - jnp.maximum(pos - w // 2, 0)
        mean = jnp.where(grp == g, sums[g] / cnt.astype(F32), mean)
    d = (mean - u).astype(BF16)
    yp = jnp.dot(d, pw_ref[...], preferred_element_type=F32) * ps_ref[...]

    ym = []
    for c in range(MEM_WIDTH // LANES):
        sl = slice(c * LANES, (c + 1) * LANES)
        o, _ = _head_pair_attention(qm_ref[:, sl], kv_ref[0, :, sl],
                                    kv_ref[0, :, MEM_WIDTH + c * LANES:MEM_WIDTH + (c + 1) * LANES], None)
        ym.append(o)
    ym = jnp.concatenate(ym, axis=1)

    gg = gg_ref[...]
    a, p = ATTN_WIDTH, POOL_WIDTH
    y = jnp.concatenate([_rms(ya, gg[:, :a]), _rms(yp, gg[:, a:a + p]), _rms(ym, gg[:, a + p:])], axis=1)
    x1 = x_ref[...] + jnp.dot(y.astype(BF16), wo_ref[...], preferred_element_type=F32)
    x1_ref[...] = x1

    h2 = _rms(x1, n2_ref[...])
    _to_row_tiles(h2_ref, h2)

    hi = h2.astype(BF16)
    lo = (h2 - hi.astype(F32)).astype(BF16)
    logits = (lax.dot_general(rwh_ref[...], hi, _NT, preferred_element_type=F32)
              + lax.dot_general(rwh_ref[...], lo, _NT, preferred_element_type=F32)
              + lax.dot_general(rwl_ref[...], hi, _NT, preferred_element_type=F32)) + rb_ref[...]
    eidx = lax.broadcasted_iota(jnp.int32, logits.shape, 0)
    krow = lax.broadcasted_iota(jnp.int32, (TOP_K, tm), 0)
    vals = jnp.zeros((TOP_K, tm), F32)
    idxs = jnp.zeros((TOP_K, tm), jnp.int32)
    work = logits
    args = []
    for kk in range(TOP_K):
        best = jnp.max(work, axis=0, keepdims=True)
        arg = jnp.min(jnp.where(work == best, eidx, N_EXPERTS), axis=0, keepdims=True)
        vals = jnp.where(krow == kk, best, vals)
        idxs = jnp.where(krow == kk, arg, idxs)
        work = jnp.where(eidx == arg, -jnp.inf, work)
        args.append(arg)
    ex = jnp.exp(vals - vals[0:1])
    gate_ref[...] = ex / jnp.sum(ex, axis=0, keepdims=True)
    idx_ref[...] = idxs

    @pl.when(i == 0)
    def _():
        carry[...] = jnp.zeros_like(carry)

    chosen = (work == -jnp.inf).astype(BF16)
    earlier = (lax.broadcasted_iota(jnp.int32, (tm, tm), 0)
               < lax.broadcasted_iota(jnp.int32, (tm, tm), 1)).astype(BF16)
    before_me = jnp.dot(chosen, earlier, preferred_element_type=F32) + carry[...]
    ranks = jnp.zeros((TOP_K, tm), F32)
    for kk in range(TOP_K):
        ranks = jnp.where(krow == kk, jnp.sum(jnp.where(eidx == args[kk], before_me, 0.0), axis=0, keepdims=True),
                          ranks)
    rank_ref[...] = ranks.astype(jnp.int32)
    base_ref[0] = carry[...].astype(jnp.int32)
    carry[...] = carry[...] + jnp.sum(chosen.astype(F32), axis=1, keepdims=True)
    cnt_ref[...] = carry[...].astype(jnp.int32)


def _mix_out(attn, u, qm, kv, x2, pw_bd, ps, gg, wo_bf, n2, rw_hi, rw_lo, rb, batch, seq):
    t, d = x2.shape
    tm = ROUTER_TILE
    hp = tm // POOL_HALO
    n_halo = t // POOL_HALO
    tiles_per_seq = seq // tm
    row = lambda i: (i, 0)
    fixed = lambda i: (0, 0)
    cls = lambda i: (i // tiles_per_seq, 0, i % tiles_per_seq, 0)
    rowspec = lambda w: pl.BlockSpec((tm, w), row)
    clsspec = lambda dil, w: pl.BlockSpec((1, dil, tm // dil, w), cls)
    tok_cols = lambda i: (0, i)
    (o1, l1), (o4, l4), (o16, l16) = attn
    return pl.pallas_call(
        functools.partial(_mix_out_kernel, tm=tm, seq=seq),
        grid=(t // tm,),
        in_specs=[clsspec(1, ATTN_WIDTH), clsspec(4, ATTN_WIDTH), clsspec(16, ATTN_WIDTH),
                  clsspec(1, LANES), clsspec(4, LANES), clsspec(16, LANES),
                  pl.BlockSpec((POOL_HALO, POOL_WIDTH), lambda i: (jnp.maximum(i * hp - 1, 0), 0)),
                  rowspec(POOL_WIDTH),
                  pl.BlockSpec((POOL_HALO, POOL_WIDTH), lambda i: (jnp.minimum((i + 1) * hp, n_halo - 1), 0)),
                  rowspec(MEM_WIDTH),
                  pl.BlockSpec((1, kv.shape[1], kv.shape[2]), lambda i: (i // tiles_per_seq, 0, 0)),
                  rowspec(d),
                  pl.BlockSpec(pw_bd.shape, fixed),
                  pl.BlockSpec(ps.shape, fixed),
                  pl.BlockSpec(gg.shape, fixed),
                  pl.BlockSpec(wo_bf.shape, fixed),
                  pl.BlockSpec(n2.shape, fixed),
                  pl.BlockSpec(rw_hi.shape, fixed),
                  pl.BlockSpec(rw_lo.shape, fixed),
                  pl.BlockSpec(rb.shape, fixed)],
        out_specs=[rowspec(d),
                   pl.BlockSpec((tm * ROW_TILE, LANES), row),
                   pl.BlockSpec((TOP_K, tm), tok_cols),
                   pl.BlockSpec((TOP_K, tm), tok_cols),
                   pl.BlockSpec((TOP_K, tm), tok_cols),
                   pl.BlockSpec((N_EXPERTS, 1), fixed),
                   pl.BlockSpec((1, N_EXPERTS, 1), lambda i: (i, 0, 0))],
        out_shape=[jax.ShapeDtypeStruct((t, d), F32),
                   jax.ShapeDtypeStruct((t * ROW_TILE, LANES), F32),
                   jax.ShapeDtypeStruct((TOP_K, t), jnp.int32),
                   jax.ShapeDtypeStruct((TOP_K, t), F32),
                   jax.ShapeDtypeStruct((TOP_K, t), jnp.int32),
                   jax.ShapeDtypeStruct((N_EXPERTS, 1), jnp.int32),
                   jax.ShapeDtypeStruct((t // tm, N_EXPERTS, 1), jnp.int32)],
        scratch_shapes=[pltpu.VMEM((ATTN_WIDTH // LANES, tm, LANES), F32),
                        pltpu.VMEM((ATTN_WIDTH // LANES, tm, LANES), F32),
                        pltpu.VMEM((1, tm, LANES), F32), pltpu.VMEM((1, tm, LANES), F32),
                        pltpu.VMEM((4, tm // 4, LANES), F32),
                        pltpu.VMEM((N_EXPERTS, 1), F32)],
        compiler_params=_params(("arbitrary",)),
        name="mix_out_router",
    )(o1, o4, o16, l1, l4, l16, u, u, u, qm, kv, x2, pw_bd, ps, gg, wo_bf, n2, rw_hi, rw_lo, rb)


def _dispatch_kernel(dest_ref, pend_ref, padded_ref, nb_ref, h_ref, xs_hbm, zeros, sem, *, tm, n_tok, n_blocks):
    i = pl.program_id(0)
    blk = MOE_BLOCK * ROW_TILE

    @pl.when(i == 0)
    def _():
        zeros[...] = jnp.zeros_like(zeros)

        def clear(block):
            return pltpu.make_async_copy(zeros, xs_hbm.at[pl.ds(pl.multiple_of(block * blk, blk), blk), :], sem)

        def for_each_cleared_block(fn):
            for e in range(N_EXPERTS):
                @pl.when(padded_ref[e] > 0)
                def _():
                    fn(clear(pend_ref[e] // MOE_BLOCK - 1))

                @pl.when(nb_ref[0] + e < n_blocks)
                def _():
                    fn(clear(nb_ref[0] + e))

        for_each_cleared_block(lambda c: c.start())
        for_each_cleared_block(lambda c: c.wait())

    def start(r, c):
        src = h_ref.at[pl.ds(pl.multiple_of(r * ROW_TILE, ROW_TILE), ROW_TILE), :]
        for kk in range(TOP_K):
            slot = dest_ref[kk * n_tok + i * tm + r]
            pltpu.make_async_copy(src, xs_hbm.at[pl.ds(pl.multiple_of(slot * ROW_TILE, ROW_TILE), ROW_TILE), :],
                                  sem).start()
        return c

    lax.fori_loop(0, tm, start, 0, unroll=4)
    n = tm * TOP_K * ROW_TILE
    pltpu.make_async_copy(xs_hbm.at[pl.ds(0, n), :], xs_hbm.at[pl.ds(0, n), :], sem).wait()


def _dispatch(dest, pad_end, padded, n_used, h2_tiles, n_slots):
    n_tok = h2_tiles.shape[0] // ROW_TILE
    tm = 512
    return pl.pallas_call(
        functools.partial(_dispatch_kernel, tm=tm, n_tok=n_tok, n_blocks=n_slots // MOE_BLOCK),
        grid_spec=pltpu.PrefetchScalarGridSpec(
            num_scalar_prefetch=4,
            grid=(n_tok // tm,),
            in_specs=[pl.BlockSpec((tm * ROW_TILE, LANES), lambda i, *_: (i, 0))],
            out_specs=pl.BlockSpec(memory_space=pl.ANY),
            scratch_shapes=[pltpu.VMEM((MOE_BLOCK * ROW_TILE, LANES), F32), pltpu.SemaphoreType.DMA(())]),
        out_shape=jax.ShapeDtypeStruct((n_slots * ROW_TILE, LANES), F32),
        compiler_params=_params(("arbitrary",)),
        name="moe_dispatch",
    )(dest, pad_end, padded, n_used, h2_tiles)


EXPERT_GROUP = 2


def _expert_kernel(first_ref, nblk_ref, nb_ref, xs_hbm, wgu_ref, bgu_ref, wdn_ref, bdn_ref, y_hbm,
                   wgu_bf, wdn_bf, xbuf, ybuf, xtail, ytail, zeros, sem_in, sem_out, *, n_blocks):
    e = pl.program_id(0)
    first = first_ref[e]
    n = nblk_ref[e]
    rows = MOE_BLOCK * ROW_TILE
    n_groups = n // EXPERT_GROUP
    tail = first + n_groups * EXPERT_GROUP

    def load(block, count, dst, sem):
        return pltpu.make_async_copy(xs_hbm.at[pl.ds(pl.multiple_of(block * rows, rows), count * rows), :], dst, sem)

    def store(block, count, src, sem):
        return pltpu.make_async_copy(src, y_hbm.at[pl.ds(pl.multiple_of(block * MOE_BLOCK, MOE_BLOCK),
                                                         count * MOE_BLOCK), :], sem)

    def group_load(j, s):
        return load(first + j * EXPERT_GROUP, EXPERT_GROUP, xbuf.at[s], sem_in.at[s])

    def group_store(j, s):
        return store(first + j * EXPERT_GROUP, EXPERT_GROUP, ybuf.at[s], sem_out.at[s])

    def ffn(x_tiles, m):
        x = _from_row_tiles(x_tiles, 0, m).astype(BF16)
        gu = jnp.dot(x, wgu_bf[...], preferred_element_type=F32) + bgu_ref[0, 0]
        gate = jnp.minimum(gu[:, :D_EXPERT], SWIGLU_LIMIT)
        up = jnp.clip(gu[:, D_EXPERT:], -SWIGLU_LIMIT, SWIGLU_LIMIT)
        act = (up + 1.0) * gate * jax.nn.sigmoid(SWIGLU_ALPHA * gate)
        return (jnp.dot(act.astype(BF16), wdn_bf[...], preferred_element_type=F32) + bdn_ref[0, 0]).astype(BF16)

    @pl.when(n > 0)
    def _():
        has_tail = n_groups * EXPERT_GROUP < n

        @pl.when(n_groups > 0)
        def _():
            group_load(0, 0).start()

        @pl.when(has_tail)
        def _():
            load(tail, 1, xtail, sem_in.at[2]).start()

        wgu_bf[...] = wgu_ref[0, 0].astype(BF16)
        wdn_bf[...] = wdn_ref[0, 0].astype(BF16)

        def group(j, carry):
            s = j % 2
            group_load(j, s).wait()

            @pl.when(j + 1 < n_groups)
            def _():
                group_load(j + 1, 1 - s).start()

            y = ffn(xbuf.at[s], EXPERT_GROUP * MOE_BLOCK)

            @pl.when(j >= 2)
            def _():
                group_store(j - 2, s).wait()

            ybuf[s] = y
            group_store(j, s).start()
            return carry

        lax.fori_loop(0, n_groups, group, 0)

        @pl.when(has_tail)
        def _():
            load(tail, 1, xtail, sem_in.at[2]).wait()
            ytail[...] = ffn(xtail, MOE_BLOCK)
            store(tail, 1, ytail, sem_out.at[2]).start()

        @pl.when(n_groups >= 2)
        def _():
            group_store(n_groups - 2, n_groups % 2).wait()

        @pl.when(n_groups >= 1)
        def _():
            group_store(n_groups - 1, (n_groups - 1) % 2).wait()

        @pl.when(has_tail)
        def _():
            store(tail, 1, ytail, sem_out.at[2]).wait()

    @pl.when(e == N_EXPERTS - 1)
    def _():
        zeros[...] = jnp.zeros_like(zeros)

        def clear(k):
            blk = pl.multiple_of((nb_ref[0] + k) * MOE_BLOCK, MOE_BLOCK)
            return pltpu.make_async_copy(zeros, y_hbm.at[pl.ds(blk, MOE_BLOCK), :], sem_out.at[0])

        for k in range(N_EXPERTS):
            @pl.when(nb_ref[0] + k < n_blocks)
            def _():
                clear(k).start()
        for k in range(N_EXPERTS):
            @pl.when(nb_ref[0] + k < n_blocks)
            def _():
                clear(k).wait()


def _experts(layer, first_block, n_block, n_used, xs_tiles, w_gu, b_gu, w_down, b_down):
    n_slots = xs_tiles.shape[0] // ROW_TILE
    d = w_gu.shape[2]
    de2 = w_gu.shape[3]
    by_expert = lambda e, *_: (layer, e, 0, 0)
    return pl.pallas_call(
        functools.partial(_expert_kernel, n_blocks=n_slots // MOE_BLOCK),
        grid_spec=pltpu.PrefetchScalarGridSpec(
            num_scalar_prefetch=3,
            grid=(N_EXPERTS,),
            in_specs=[pl.BlockSpec(memory_space=pl.ANY),
                      pl.BlockSpec((1, 1, d, de2), by_expert),
                      pl.BlockSpec((1, 1, 1, de2), by_expert),
                      pl.BlockSpec((1, 1, D_EXPERT, d), by_expert),
                      pl.BlockSpec((1, 1, 1, d), by_expert)],
            out_specs=pl.BlockSpec(memory_space=pl.ANY),
            scratch_shapes=[pltpu.VMEM((d, de2), BF16), pltpu.VMEM((D_EXPERT, d), BF16),
                            pltpu.VMEM((2, EXPERT_GROUP * MOE_BLOCK * ROW_TILE, LANES), F32),
                            pltpu.VMEM((2, EXPERT_GROUP * MOE_BLOCK, d), BF16),
                            pltpu.VMEM((MOE_BLOCK * ROW_TILE, LANES), F32),
                            pltpu.VMEM((MOE_BLOCK, d), BF16),
                            pltpu.VMEM((MOE_BLOCK, d), BF16),
                            pltpu.SemaphoreType.DMA((3,)), pltpu.SemaphoreType.DMA((3,))]),
        out_shape=jax.ShapeDtypeStruct((n_slots, d), BF16),
        compiler_params=_params(("arbitrary",)),
        name="moe_experts",
    )(first_block, n_block, n_used, xs_tiles, w_gu, b_gu.reshape(b_gu.shape[0], N_EXPERTS, 1, de2), w_down,
      b_down.reshape(b_down.shape[0], N_EXPERTS, 1, d))


CHUNK = 16
COMBINE_ROWS = ROUTER_TILE * TOP_K + 2 * N_EXPERTS * CHUNK


def _combine_kernel(a_ref, nch_ref, tot_ref, y_hbm, x1_ref, col_ref, gate_ref, g_ref, out_ref, ybuf, sems,
                    *, tm, n_tiles, final):
    i = pl.program_id(0)
    slot = i % 2

    @pl.when(i == 0)
    def _():
        ybuf[...] = jnp.zeros_like(ybuf)

    def fetch(tile, buf):
        pos = jnp.int32(0)
        for e in range(N_EXPERTS):
            a = a_ref[tile * N_EXPERTS + e]

            def one(c, pos):
                pltpu.make_async_copy(y_hbm.at[pl.ds(pl.multiple_of(a + c * CHUNK, CHUNK), CHUNK), :],
                                      ybuf.at[buf, pl.ds(pl.multiple_of(pos, CHUNK), CHUNK), :],
                                      sems.at[buf]).start()
                return pos + CHUNK

            pos = lax.fori_loop(0, nch_ref[tile * N_EXPERTS + e], one, pos)

    @pl.when(i == 0)
    def _():
        fetch(0, 0)

    @pl.when(i + 1 < n_tiles)
    def _():
        fetch(i + 1, 1 - slot)

    col = col_ref[...]
    gates = gate_ref[...]
    col_id = lax.broadcasted_iota(jnp.int32, (tm, COMBINE_ROWS), 1)
    g = jnp.zeros((tm, COMBINE_ROWS), F32)
    for kk in range(TOP_K):
        g = jnp.where(col_id == col[:, kk:kk + 1], gates[:, kk:kk + 1], g)

    def wait_one(c, carry):
        pltpu.make_async_copy(y_hbm.at[pl.ds(0, CHUNK), :], ybuf.at[slot, pl.ds(0, CHUNK), :], sems.at[slot]).wait()
        return carry

    lax.fori_loop(0, tot_ref[i], wait_one, 0)
    x = x1_ref[...] + jnp.dot(g.astype(BF16), ybuf[slot], preferred_element_type=F32)
    out_ref[...] = _rms(x, g_ref[...]) if final else x


def _combine(seg_a, seg_nch, tile_chunks, y_slots, x1, col_tk, gates_tk, final_g, final):
    t, d = x1.shape
    tm = ROUTER_TILE
    n_tiles = t // tm
    tok = lambda i, *_: (i, 0)
    return pl.pallas_call(
        functools.partial(_combine_kernel, tm=tm, n_tiles=n_tiles, final=final),
        grid_spec=pltpu.PrefetchScalarGridSpec(
            num_scalar_prefetch=3,
            grid=(n_tiles,),
            in_specs=[pl.BlockSpec(memory_space=pl.ANY),
                      pl.BlockSpec((tm, d), tok),
                      pl.BlockSpec((tm, TOP_K), tok),
                      pl.BlockSpec((tm, TOP_K), tok),
                      pl.BlockSpec((1, d), lambda i, *_: (0, 0))],
            out_specs=pl.BlockSpec((tm, d), tok),
            scratch_shapes=[pltpu.VMEM((2, COMBINE_ROWS, d), BF16), pltpu.SemaphoreType.DMA((2,))]),
        out_shape=jax.ShapeDtypeStruct((t, d), F32),
        compiler_params=_params(("arbitrary",)),
        name="moe_combine",
    )(seg_a, seg_nch, tile_chunks, y_slots, x1, col_tk, gates_tk, final_g)


def _slot_layout(top_idx, rank, counts, tile_base):
    counts = counts.reshape(-1)
    padded = (counts + MOE_BLOCK - 1) // MOE_BLOCK * MOE_BLOCK
    pad_end = jnp.cumsum(padded).astype(jnp.int32)
    pad_start = pad_end - padded
    experts = jnp.arange(N_EXPERTS, dtype=jnp.int32)
    start_of = jnp.sum(jnp.where(top_idx[..., None] == experts, pad_start, 0), axis=-1)
    dest = (start_of + rank).astype(jnp.int32)
    n_used = pad_end[-1] // MOE_BLOCK
    base = tile_base.reshape(-1, N_EXPERTS)
    run_start = pad_start[None, :] + base
    run_len = jnp.concatenate([base[1:], counts[None, :]], axis=0) - base
    seg_a = run_start // CHUNK * CHUNK
    seg_nch = jnp.where(run_len > 0, (run_start + run_len - seg_a + CHUNK - 1) // CHUNK, 0)
    buf_row0 = (jnp.cumsum(seg_nch, axis=1) - seg_nch) * CHUNK
    shift = jnp.repeat(buf_row0 - seg_a, ROUTER_TILE, axis=0)
    col = dest + jnp.sum(jnp.where(top_idx[..., None] == experts, shift[None], 0), axis=-1)
    flat = lambda a: a.reshape(-1).astype(jnp.int32)
    return (dest, pad_end, padded.astype(jnp.int32), n_used.reshape(1).astype(jnp.int32),
            flat(seg_a), flat(seg_nch), flat(jnp.sum(seg_nch, axis=1)), col.astype(jnp.int32))


def _rope_tables(seq):
    half = HEAD_DIM // 2
    inv_freq = ROPE_THETA ** (-jnp.arange(half, dtype=F32) / half)
    ang = jnp.arange(seq, dtype=F32)[:, None] * inv_freq[None, :]
    cos, sin = jnp.cos(ang), jnp.sin(ang)
    reps = LANES // HEAD_DIM
    cos_l = jnp.tile(jnp.concatenate([cos, cos], axis=1), (1, reps))
    sin_l = jnp.tile(jnp.concatenate([-sin, sin], axis=1), (1, reps))
    return cos_l, sin_l


def kernel(x, mem, norm1_g, w_in, pool_w, pool_scale, mem_norm_g, w_mem_kv, grp_norm_g, w_out, norm2_g,
           router_w, router_b, w_gu, b_gu, w_down, b_down, final_g):
    batch, seq, d = x.shape
    depth = w_in.shape[0]
    n_mem = mem.shape[1]
    t = batch * seq
    n_blocks = t * TOP_K // MOE_BLOCK + N_EXPERTS
    cos_l, sin_l = _rope_tables(seq)
    x2 = x.reshape(t, d)
    mem2 = mem.reshape(batch * n_mem, d)
    row = lambda a: a.reshape(1, -1)
    for l in range(depth):
        qkv, u, qm = _in_proj(x2, row(norm1_g[l]), w_in[l].astype(BF16), cos_l, sin_l, batch, seq)
        attn = [_dilated_attention(*qkv[n]) for n in range(len(DILATIONS))]
        kv = _mem_kv(mem2, row(mem_norm_g[l]), w_mem_kv[l].astype(BF16)).reshape(batch, n_mem, 2 * MEM_WIDTH)
        pw_bd = jax.scipy.linalg.block_diag(*[pool_w[l, g] for g in range(len(POOL_WINDOWS))]).astype(BF16)
        rw_t = router_w[l].T
        rw_hi = rw_t.astype(BF16)
        rw_lo = (rw_t - rw_hi.astype(F32)).astype(BF16)
        x1, h2_tiles, top_idx, gates, rank, counts, tile_base = _mix_out(
            attn, u, qm, kv, x2, pw_bd, row(pool_scale[l]), row(grp_norm_g[l]), w_out[l].astype(BF16),
            row(norm2_g[l]), rw_hi, rw_lo, router_b[l].reshape(-1, 1), batch, seq)
        dest, pad_end, padded, n_used, seg_a, seg_nch, tile_chunks, col = _slot_layout(
            top_idx, rank, counts, tile_base)
        xs_tiles = _dispatch(dest.reshape(-1), pad_end, padded, n_used, h2_tiles, n_blocks * MOE_BLOCK)
        y_slots = _experts(l, (pad_end - padded) // MOE_BLOCK, padded // MOE_BLOCK, n_used, xs_tiles,
                           w_gu, b_gu, w_down, b_down)
        x2 = _combine(seg_a, seg_nch, tile_chunks, y_slots, x1, col.T, gates.T, row(final_g),
                      final=(l == depth - 1))
    return x2.reshape(batch, seq, d)
```

```python
import functools

import jax
import jax.numpy as jnp
from jax import lax
from jax.experimental import pallas as pl
from jax.experimental.pallas import tpu as pltpu

D_MODEL = 1024
HEAD_DIM = 64
ATTN_WIDTH = 512
DILATIONS = (1, 4, 16)
BAND = 64
ROPE_THETA = 10000.0
POOL_WINDOWS = (2, 4, 8, 16)
POOL_WIDTH = 256
POOL_GROUP = 64
POOL_HALO = 8
MEM_WIDTH = 256
N_EXPERTS = 32
TOP_K = 4
D_EXPERT = 1024
SWIGLU_ALPHA = 1.702
SWIGLU_LIMIT = 7.0
MOE_BLOCK = 256
ROUTER_TILE = 256
NORM_EPS = 1e-5
NEG_INF = -1e30
LANES = 128
SUBLANES = 8
ROW_TILE = D_MODEL // LANES

F32 = jnp.float32
BF16 = jnp.bfloat16
VMEM_LIMIT = 56 * 1024 * 1024

_NT = (((1,), (1,)), ((), ()))


def _params(sem, vmem=VMEM_LIMIT):
    return pltpu.CompilerParams(dimension_semantics=sem, vmem_limit_bytes=vmem)


def _rms(x, g):
    return x * lax.rsqrt(jnp.mean(x * x, axis=-1, keepdims=True) + NORM_EPS) * g


def _to_row_tiles(ref, val):
    m = val.shape[0]
    for s in range(ROW_TILE):
        ref[pl.ds(s, m, stride=ROW_TILE), :] = val[:, s * LANES:(s + 1) * LANES]


def _from_row_tiles(ref, start, m):
    return jnp.concatenate([ref[pl.ds(start * ROW_TILE + s, m, stride=ROW_TILE), :] for s in range(ROW_TILE)],
                           axis=1)


def _in_proj_kernel(x_ref, g_ref, w_ref, cos_ref, sin_ref, q1, k1, v1, q4, k4, v4, q16, k16, v16, u_ref, qm_ref,
                    qkv, cls4, *, tm):
    h = _rms(x_ref[...], g_ref[...]).astype(BF16)
    proj = jnp.dot(h, w_ref[...], preferred_element_type=F32)
    cos = cos_ref[...]
    sin = sin_ref[...]
    lane = lax.broadcasted_iota(jnp.int32, cos.shape, 1)
    first_half = (lane % HEAD_DIM) < (HEAD_DIM // 2)
    scale = HEAD_DIM ** -0.5

    def rope(t):
        partner = jnp.where(first_half, pltpu.roll(t, LANES - HEAD_DIM // 2, 1),
                            pltpu.roll(t, HEAD_DIM // 2, 1))
        return t * cos + partner * sin

    a = ATTN_WIDTH
    groups = a // LANES
    for c in range(groups):
        qkv[c] = rope(proj[:, c * LANES:(c + 1) * LANES]) * scale
        qkv[groups + c] = rope(proj[:, a + c * LANES:a + (c + 1) * LANES])
        qkv[2 * groups + c] = proj[:, 2 * a + c * LANES:2 * a + (c + 1) * LANES]
    u_ref[...] = proj[:, 3 * a:3 * a + POOL_WIDTH]
    qm_ref[...] = (proj[:, 3 * a + POOL_WIDTH:] * scale).astype(BF16)

    for n, (r1, r4, r16) in enumerate(((q1, q4, q16), (k1, k4, k16), (v1, v4, v16))):
        for c in range(groups):
            g = n * groups + c
            cols = slice(c * LANES, (c + 1) * LANES)
            r1[0, 0, :, cols] = qkv[g].astype(BF16)
            for a in range(4):
                rows = qkv[g, pl.ds(a, tm // 4, stride=4), :]
                r4[0, a, :, cols] = rows.astype(BF16)
                cls4[g * 4 + a] = rows
            for a in range(4):
                for b in range(4):
                    r16[0, a + 4 * b, :, cols] = cls4[g * 4 + a, pl.ds(b, tm // 16, stride=4), :].astype(BF16)


def _in_proj(x2, g, w_bf, cos, sin, batch, seq):
    t, d = x2.shape
    tm = 512
    tiles_per_seq = seq // tm
    cols = w_bf.shape[1]
    row = lambda i: (i, 0)
    fixed = lambda i: (0, 0)
    cls = lambda i: (i // tiles_per_seq, 0, i % tiles_per_seq, 0)
    cls_specs, cls_shapes = [], []
    for dil in DILATIONS:
        for _ in range(3):
            cls_specs.append(pl.BlockSpec((1, dil, tm // dil, ATTN_WIDTH), cls))
            cls_shapes.append(jax.ShapeDtypeStruct((batch, dil, seq // dil, ATTN_WIDTH), BF16))
    outs = pl.pallas_call(
        functools.partial(_in_proj_kernel, tm=tm),
        grid=(t // tm,),
        in_specs=[pl.BlockSpec((tm, d), row),
                  pl.BlockSpec((1, d), fixed),
                  pl.BlockSpec((d, cols), fixed),
                  pl.BlockSpec((tm, LANES), lambda i: (i % tiles_per_seq, 0)),
                  pl.BlockSpec((tm, LANES), lambda i: (i % tiles_per_seq, 0))],
        out_specs=cls_specs + [pl.BlockSpec((tm, POOL_WIDTH), row), pl.BlockSpec((tm, MEM_WIDTH), row)],
        out_shape=cls_shapes + [jax.ShapeDtypeStruct((t, POOL_WIDTH), F32),
                                jax.ShapeDtypeStruct((t, MEM_WIDTH), BF16)],
        scratch_shapes=[pltpu.VMEM((3 * ATTN_WIDTH // LANES, tm, LANES), F32),
                        pltpu.VMEM((4 * 3 * ATTN_WIDTH // LANES, tm // 4, LANES), F32)],
        compiler_params=_params(("parallel",)),
        name="in_proj",
    )(x2, g, w_bf, cos, sin)
    qkv = [outs[3 * n:3 * n + 3] for n in range(len(DILATIONS))]
    return qkv, outs[-2], outs[-1]


def _head_pair_attention(q, k, v, valid):
    lane = lax.broadcasted_iota(jnp.int32, q.shape, 1)
    outs, lses = [], []
    for half in range(2):
        mine = (lane // HEAD_DIM) == half
        s = lax.dot_general(jnp.where(mine, q, jnp.zeros_like(q)), k, _NT,
                            preferred_element_type=F32)
        if valid is not None:
            s = jnp.where(valid, s, NEG_INF)
        m = jnp.max(s, axis=-1, keepdims=True)
        p = jnp.exp(s - m)
        den = jnp.sum(p, axis=-1, keepdims=True)
        pv = jnp.dot(p.astype(BF16), v, preferred_element_type=F32)
        outs.append(pv / den)
        lses.append(m + jnp.log(den))
    return jnp.where((lane // HEAD_DIM) == 0, outs[0], outs[1]), lses


def _attn_kernel(q_ref, kp_ref, kc_ref, kn_ref, vp_ref, vc_ref, vn_ref, o_ref, lse_ref, *, tq, sub, length):
    j = pl.program_id(2)
    nk = sub + 2 * BAND
    row = lax.broadcasted_iota(jnp.int32, (sub, nk), 0)
    col = lax.broadcasted_iota(jnp.int32, (sub, nk), 1)
    in_band = jnp.abs(col - BAND - row) <= BAND
    lane = lax.broadcasted_iota(jnp.int32, (sub, LANES), 1)
    for c in range(ATTN_WIDTH // LANES):
        sl = slice(c * LANES, (c + 1) * LANES)
        k = jnp.concatenate([kp_ref[0, 0, :, sl], kc_ref[0, 0, :, sl], kn_ref[0, 0, :, sl]], axis=0)
        v = jnp.concatenate([vp_ref[0, 0, :, sl], vc_ref[0, 0, :, sl], vn_ref[0, 0, :, sl]], axis=0)
        for s in range(tq // sub):
            rows = slice(s * sub, (s + 1) * sub)
            key = j * tq + s * sub - BAND + col
            valid = in_band & (key >= 0) & (key < length)
            o, lses = _head_pair_attention(q_ref[0, 0, rows, sl], k[s * sub:s * sub + nk], v[s * sub:s * sub + nk],
                                           valid)
            o_ref[0, 0, rows, sl] = o
            lse_tile = jnp.where(lane == 2 * c, lses[0], lses[1])
            if c == 0:
                lse_ref[0, 0, rows, :] = lse_tile
            else:
                lse_ref[0, 0, rows, :] = jnp.where((lane // 2) == c, lse_tile, lse_ref[0, 0, rows, :])


def _dilated_attention(q, k, v):
    batch, dilation, length, _ = q.shape
    tq = 512
    per = tq // BAND
    n_band_blocks = length // BAND
    cur = lambda b, r, j: (b, r, j, 0)
    prev = lambda b, r, j: (b, r, jnp.maximum(j * per - 1, 0), 0)
    nxt = lambda b, r, j: (b, r, jnp.minimum((j + 1) * per, n_band_blocks - 1), 0)
    big = lambda imap: pl.BlockSpec((1, 1, tq, ATTN_WIDTH), imap)
    halo = lambda imap: pl.BlockSpec((1, 1, BAND, ATTN_WIDTH), imap)
    return pl.pallas_call(
        functools.partial(_attn_kernel, tq=tq, sub=128, length=length),
        grid=(batch, dilation, length // tq),
        in_specs=[big(cur), halo(prev), big(cur), halo(nxt), halo(prev), big(cur), halo(nxt)],
        out_specs=[pl.BlockSpec((1, 1, tq, ATTN_WIDTH), cur), pl.BlockSpec((1, 1, tq, LANES), cur)],
        out_shape=[jax.ShapeDtypeStruct((batch, dilation, length, ATTN_WIDTH), F32),
                   jax.ShapeDtypeStruct((batch, dilation, length, LANES), F32)],
        compiler_params=_params(("parallel", "parallel", "parallel")),
        name=f"dilated_attn_d{dilation}",
    )(q, k, k, k, v, v, v)


def _mem_kv_kernel(m_ref, g_ref, w_ref, kv_ref):
    h = _rms(m_ref[...], g_ref[...]).astype(BF16)
    kv_ref[...] = jnp.dot(h, w_ref[...], preferred_element_type=F32).astype(BF16)


def _mem_kv(mem2, g, w_bf):
    n, d = mem2.shape
    cols = w_bf.shape[1]
    tm = 256
    return pl.pallas_call(
        _mem_kv_kernel,
        grid=(n // tm,),
        in_specs=[pl.BlockSpec((tm, d), lambda i: (i, 0)),
                  pl.BlockSpec((1, d), lambda i: (0, 0)),
                  pl.BlockSpec((d, cols), lambda i: (0, 0))],
        out_specs=pl.BlockSpec((tm, cols), lambda i: (i, 0)),
        out_shape=jax.ShapeDtypeStruct((n, cols), BF16),
        compiler_params=_params(("parallel",)),
        name="mem_kv",
    )(mem2, g, w_bf)


def _mix_out_kernel(o1_ref, o4_ref, o16_ref, l1_ref, l4_ref, l16_ref, up_ref, u_ref, un_ref, qm_ref, kv_ref,
                    x_ref, pw_ref, ps_ref, gg_ref, wo_ref, n2_ref, rwh_ref, rwl_ref, rb_ref,
                    x1_ref, h2_ref, idx_ref, gate_ref, rank_ref, cnt_ref, base_ref,
                    o4_s, o16_s, l4_s, l16_s, cls4, carry, *, tm, seq):
    i = pl.program_id(0)
    tiles_per_seq = seq // tm
    pos0 = (i % tiles_per_seq) * tm

    for src4, src16, dst4, dst16 in ((o4_ref, o16_ref, o4_s, o16_s), (l4_ref, l16_ref, l4_s, l16_s)):
        for c in range(dst4.shape[0]):
            cols = slice(c * LANES, (c + 1) * LANES)
            for a in range(4):
                dst4[c, pl.ds(a, tm // 4, stride=4), :] = src4[0, a, :, cols]
                for b in range(4):
                    cls4[a, pl.ds(b, tm // 16, stride=4), :] = src16[0, a + 4 * b, :, cols]
            for a in range(4):
                dst16[c, pl.ds(a, tm // 4, stride=4), :] = cls4[a]

    l1, l2, l3 = l1_ref[0, 0], l4_s[0], l16_s[0]
    lm = jnp.maximum(jnp.maximum(l1, l2), l3)
    e1, e2, e3 = jnp.exp(l1 - lm), jnp.exp(l2 - lm), jnp.exp(l3 - lm)
    es = e1 + e2 + e3
    w1, w2, w3 = e1 / es, e2 / es, e3 / es
    ya = []
    for c in range(ATTN_WIDTH // LANES):
        sl = slice(c * LANES, (c + 1) * LANES)
        lane = lax.broadcasted_iota(jnp.int32, (tm, LANES), 1)
        lo = lane < HEAD_DIM

        def per_head(w):
            return jnp.where(lo, w[:, 2 * c:2 * c + 1], w[:, 2 * c + 1:2 * c + 2])

        ya.append(per_head(w1) * o1_ref[0, 0, :, sl] + per_head(w2) * o4_s[c] + per_head(w3) * o16_s[c])
    ya = jnp.concatenate(ya, axis=1)

    u = u_ref[...]
    before = jnp.where(pos0 > 0, up_ref[...], 0.0)
    after = jnp.where(pos0 + tm < seq, un_ref[...], 0.0)
    ext = jnp.concatenate([before, u, after], axis=0)
    n_ext = tm + 2 * POOL_HALO
    sums = []
    acc = ext
    shift = 1
    for w in POOL_WINDOWS:
        if w == 2:
            acc = pltpu.roll(ext, 1, 0) + ext
        else:
            acc = pltpu.roll(acc, shift, 0) + pltpu.roll(acc, n_ext - shift, 0)
            shift *= 2
        sums.append(acc[POOL_HALO:POOL_HALO + tm])
    pos = pos0 + lax.broadcasted_iota(jnp.int32, (tm, POOL_WIDTH), 0)
    grp = lax.broadcasted_iota(jnp.int32, (tm, POOL_WIDTH), 1) // POOL_GROUP
    mean = jnp.zeros((tm, POOL_WIDTH), F32)
    for g, w in enumerate(POOL_WINDOWS):
        cnt = jnp.minimum(pos + (w - 1 - w // 2), seq - 1) + 1 - jnp.maximum(pos - w // 2, 0)
        mean = jnp.where(grp == g, sums[g] / cnt.astype(F32), mean)
    d = (mean - u).astype(BF16)
    yp = jnp.dot(d, pw_ref[...], preferred_element_type=F32) * ps_ref[...]

    ym = []
    for c in range(MEM_WIDTH // LANES):
        sl = slice(c * LANES, (c + 1) * LANES)
        o, _ = _head_pair_attention(qm_ref[:, sl], kv_ref[0, :, sl],
                                    kv_ref[0, :, MEM_WIDTH + c * LANES:MEM_WIDTH + (c + 1) * LANES], None)
        ym.append(o)
    ym = jnp.concatenate(ym, axis=1)

    gg = gg_ref[...]
    a, p = ATTN_WIDTH, POOL_WIDTH
    y = jnp.concatenate([_rms(ya, gg[:, :a]), _rms(yp, gg[:, a:a + p]), _rms(ym, gg[:, a + p:])], axis=1)
    x1 = x_ref[...] + jnp.dot(y.astype(BF16), wo_ref[...], preferred_element_type=F32)
    x1_ref[...] = x1

    h2 = _rms(x1, n2_ref[...])
    _to_row_tiles(h2_ref, h2)

    hi = h2.astype(BF16)
    lo = (h2 - hi.astype(F32)).astype(BF16)
    logits = (lax.dot_general(rwh_ref[...], hi, _NT, preferred_element_type=F32)
              + lax.dot_general(rwh_ref[...], lo, _NT, preferred_element_type=F32)
              + lax.dot_general(rwl_ref[...], hi, _NT, preferred_element_type=F32)) + rb_ref[...]
    eidx = lax.broadcasted_iota(jnp.int32, logits.shape, 0)
    krow = lax.broadcasted_iota(jnp.int32, (TOP_K, tm), 0)
    vals = jnp.zeros((TOP_K, tm), F32)
    idxs = jnp.zeros((TOP_K, tm), jnp.int32)
    work = logits
    args = []
    for kk in range(TOP_K):
        best = jnp.max(work, axis=0, keepdims=True)
        arg = jnp.min(jnp.where(work == best, eidx, N_EXPERTS), axis=0, keepdims=True)
        vals = jnp.where(krow == kk, best, vals)
        idxs = jnp.where(krow == kk, arg, idxs)
        work = jnp.where(eidx == arg, -jnp.inf, work)
        args.append(arg)
    ex = jnp.exp(vals - vals[0:1])
    gate_ref[...] = ex / jnp.sum(ex, axis=0, keepdims=True)
    idx_ref[...] = idxs

    @pl.when(i == 0)
    def _():
        carry[...] = jnp.zeros_like(carry)

    chosen = (work == -jnp.inf).astype(BF16)
    earlier = (lax.broadcasted_iota(jnp.int32, (tm, tm), 0)
               < lax.broadcasted_iota(jnp.int32, (tm, tm), 1)).astype(BF16)
    before_me = jnp.dot(chosen, earlier, preferred_element_type=F32) + carry[...]
    ranks = jnp.zeros((TOP_K, tm), F32)
    for kk in range(TOP_K):
        ranks = jnp.where(krow == kk, jnp.sum(jnp.where(eidx == args[kk], before_me, 0.0), axis=0, keepdims=True),
                          ranks)
    rank_ref[...] = ranks.astype(jnp.int32)
    base_ref[0] = carry[...].astype(jnp.int32)
    carry[...] = carry[...] + jnp.sum(chosen.astype(F32), axis=1, keepdims=True)
    cnt_ref[...] = carry[...].astype(jnp.int32)


def _mix_out(attn, u, qm, kv, x2, pw_bd, ps, gg, wo_bf, n2, rw_hi, rw_lo, rb, batch, seq):
    t, d = x2.shape
    tm = ROUTER_TILE
    hp = tm // POOL_HALO
    n_halo = t // POOL_HALO
    tiles_per_seq = seq // tm
    row = lambda i: (i, 0)
    fixed = lambda i: (0, 0)
    cls = lambda i: (i // tiles_per_seq, 0, i % tiles_per_seq, 0)
    rowspec = lambda w: pl.BlockSpec((tm, w), row)
    clsspec = lambda dil, w: pl.BlockSpec((1, dil, tm // dil, w), cls)
    tok_cols = lambda i: (0, i)
    (o1, l1), (o4, l4), (o16, l16) = attn
    return pl.pallas_call(
        functools.partial(_mix_out_kernel, tm=tm, seq=seq),
        grid=(t // tm,),
        in_specs=[clsspec(1, ATTN_WIDTH), clsspec(4, ATTN_WIDTH), clsspec(16, ATTN_WIDTH),
                  clsspec(1, LANES), clsspec(4, LANES), clsspec(16, LANES),
                  pl.BlockSpec((POOL_HALO, POOL_WIDTH), lambda i: (jnp.maximum(i * hp - 1, 0), 0)),
                  rowspec(POOL_WIDTH),
                  pl.BlockSpec((POOL_HALO, POOL_WIDTH), lambda i: (jnp.minimum((i + 1) * hp, n_halo - 1), 0)),
                  rowspec(MEM_WIDTH),
                  pl.BlockSpec((1, kv.shape[1], kv.shape[2]), lambda i: (i // tiles_per_seq, 0, 0)),
                  rowspec(d),
                  pl.BlockSpec(pw_bd.shape, fixed),
                  pl.BlockSpec(ps.shape, fixed),
                  pl.BlockSpec(gg.shape, fixed),
                  pl.BlockSpec(wo_bf.shape, fixed),
                  pl.BlockSpec(n2.shape, fixed),
                  pl.BlockSpec(rw_hi.shape, fixed),
                  pl.BlockSpec(rw_lo.shape, fixed),
                  pl.BlockSpec(rb.shape, fixed)],
        out_specs=[rowspec(d),
                   pl.BlockSpec((tm * ROW_TILE, LANES), row),
                   pl.BlockSpec((TOP_K, tm), tok_cols),
                   pl.BlockSpec((TOP_K, tm), tok_cols),
                   pl.BlockSpec((TOP_K, tm), tok_cols),
                   pl.BlockSpec((N_EXPERTS, 1), fixed),
                   pl.BlockSpec((1, N_EXPERTS, 1), lambda i: (i, 0, 0))],
        out_shape=[jax.ShapeDtypeStruct((t, d), F32),
                   jax.ShapeDtypeStruct((t * ROW_TILE, LANES), F32),
                   jax.ShapeDtypeStruct((TOP_K, t), jnp.int32),
                   jax.ShapeDtypeStruct((TOP_K, t), F32),
                   jax.ShapeDtypeStruct((TOP_K, t), jnp.int32),
                   jax.ShapeDtypeStruct((N_EXPERTS, 1), jnp.int32),
                   jax.ShapeDtypeStruct((t // tm, N_EXPERTS, 1), jnp.int32)],
        scratch_shapes=[pltpu.VMEM((ATTN_WIDTH // LANES, tm, LANES), F32),
                        pltpu.VMEM((ATTN_WIDTH // LANES, tm, LANES), F32),
                        pltpu.VMEM((1, tm, LANES), F32), pltpu.VMEM((1, tm, LANES), F32),
                        pltpu.VMEM((4, tm // 4, LANES), F32),
                        pltpu.VMEM((N_EXPERTS, 1), F32)],
        compiler_params=_params(("arbitrary",)),
        name="mix_out_router",
    )(o1, o4, o16, l1, l4, l16, u, u, u, qm, kv, x2, pw_bd, ps, gg, wo_bf, n2, rw_hi, rw_lo, rb)


def _dispatch_kernel(dest_ref, pend_ref, padded_ref, nb_ref, h_ref, xs_hbm, zeros, sem, *, tm, n_tok, n_blocks):
    i = pl.program_id(0)
    blk = MOE_BLOCK * ROW_TILE

    @pl.when(i == 0)
    def _():
        zeros[...] = jnp.zeros_like(zeros)

        def clear(block):
            return pltpu.make_async_copy(zeros, xs_hbm.at[pl.ds(pl.multiple_of(block * blk, blk), blk), :], sem)

        def for_each_cleared_block(fn):
            for e in range(N_EXPERTS):
                @pl.when(padded_ref[e] > 0)
                def _():
                    fn(clear(pend_ref[e] // MOE_BLOCK - 1))

                @pl.when(nb_ref[0] + e < n_blocks)
                def _():
                    fn(clear(nb_ref[0] + e))

        for_each_cleared_block(lambda c: c.start())
        for_each_cleared_block(lambda c: c.wait())

    def start(r, c):
        src = h_ref.at[pl.ds(pl.multiple_of(r * ROW_TILE, ROW_TILE), ROW_TILE), :]
        for kk in range(TOP_K):
            slot = dest_ref[kk * n_tok + i * tm + r]
            pltpu.async_copy(src, xs_hbm.at[pl.ds(pl.multiple_of(slot * ROW_TILE, ROW_TILE), ROW_TILE), :],
                             sem, priority=kk % 2)
        return c

    lax.fori_loop(0, tm, start, 0, unroll=4)
    n = tm * TOP_K * ROW_TILE
    pltpu.make_async_copy(xs_hbm.at[pl.ds(0, n), :], xs_hbm.at[pl.ds(0, n), :], sem).wait()


def _dispatch(dest, pad_end, padded, n_used, h2_tiles, n_slots):
    n_tok = h2_tiles.shape[0] // ROW_TILE
    tm = 512
    return pl.pallas_call(
        functools.partial(_dispatch_kernel, tm=tm, n_tok=n_tok, n_blocks=n_slots // MOE_BLOCK),
        grid_spec=pltpu.PrefetchScalarGridSpec(
            num_scalar_prefetch=4,
            grid=(n_tok // tm,),
            in_specs=[pl.BlockSpec((tm * ROW_TILE, LANES), lambda i, *_: (i, 0))],
            out_specs=pl.BlockSpec(memory_space=pl.ANY),
            scratch_shapes=[pltpu.VMEM((MOE_BLOCK * ROW_TILE, LANES), F32), pltpu.SemaphoreType.DMA(())]),
        out_shape=jax.ShapeDtypeStruct((n_slots * ROW_TILE, LANES), F32),
        compiler_params=_params(("arbitrary",)),
        name="moe_dispatch",
    )(dest, pad_end, padded, n_used, h2_tiles)


EXPERT_GROUP = 2


WEIGHT_DMA_QUEUE = 1


def _expert_kernel(first_ref, nblk_ref, nb_ref, xs_hbm, wgu_hbm, bgu_ref, wdn_hbm, bdn_ref, y_hbm,
                   wgu_f32, wdn_f32, wgu_bf, wdn_bf, xbuf, ybuf, xtail, ytail, zeros, sem_w, sem_in, sem_out,
                   *, layer, n_blocks):
    e = pl.program_id(0)
    first = first_ref[e]
    n = nblk_ref[e]
    rows = MOE_BLOCK * ROW_TILE
    n_groups = n // EXPERT_GROUP
    tail = first + n_groups * EXPERT_GROUP

    def load(block, count, dst, sem):
        return pltpu.make_async_copy(xs_hbm.at[pl.ds(pl.multiple_of(block * rows, rows), count * rows), :], dst, sem)

    def store(block, count, src, sem):
        return pltpu.make_async_copy(src, y_hbm.at[pl.ds(pl.multiple_of(block * MOE_BLOCK, MOE_BLOCK),
                                                         count * MOE_BLOCK), :], sem)

    def group_load(j, s):
        return load(first + j * EXPERT_GROUP, EXPERT_GROUP, xbuf.at[s], sem_in.at[s])

    def group_store(j, s):
        return store(first + j * EXPERT_GROUP, EXPERT_GROUP, ybuf.at[s], sem_out.at[s])

    def ffn(x_tiles, m):
        x = _from_row_tiles(x_tiles, 0, m).astype(BF16)
        gu = jnp.dot(x, wgu_bf[...], preferred_element_type=F32) + bgu_ref[0, 0]
        gate = jnp.minimum(gu[:, :D_EXPERT], SWIGLU_LIMIT)
        up = jnp.clip(gu[:, D_EXPERT:], -SWIGLU_LIMIT, SWIGLU_LIMIT)
        act = (up + 1.0) * gate * jax.nn.sigmoid(SWIGLU_ALPHA * gate)
        return (jnp.dot(act.astype(BF16), wdn_bf[...], preferred_element_type=F32) + bdn_ref[0, 0]).astype(BF16)

    def weights(ex, fn):
        s = ex % 2
        fn(wgu_hbm.at[layer, ex], wgu_f32.at[s], sem_w.at[0, s])
        fn(wdn_hbm.at[layer, ex], wdn_f32.at[s], sem_w.at[1, s])

    def w_start(src, dst, sem):
        pltpu.async_copy(src, dst, sem, priority=WEIGHT_DMA_QUEUE)

    def w_wait(src, dst, sem):
        pltpu.make_async_copy(src, dst, sem).wait()

    @pl.when(e == 0)
    def _():
        weights(0, w_start)

    has_tail = n_groups * EXPERT_GROUP < n

    @pl.when(n_groups > 0)
    def _():
        group_load(0, 0).start()

    @pl.when(has_tail)
    def _():
        load(tail, 1, xtail, sem_in.at[2]).start()

    @pl.when(e + 1 < N_EXPERTS)
    def _():
        weights(e + 1, w_start)

    weights(e, w_wait)

    @pl.when(n > 0)
    def _():
        wgu_bf[...] = wgu_f32[e % 2].astype(BF16)
        wdn_bf[...] = wdn_f32[e % 2].astype(BF16)

        def group(j, carry):
            s = j % 2
            group_load(j, s).wait()

            @pl.when(j + 1 < n_groups)
            def _():
                group_load(j + 1, 1 - s).start()

            y = ffn(xbuf.at[s], EXPERT_GROUP * MOE_BLOCK)

            @pl.when(j >= 2)
            def _():
                group_store(j - 2, s).wait()

            ybuf[s] = y
            group_store(j, s).start()
            return carry

        lax.fori_loop(0, n_groups, group, 0)

        @pl.when(has_tail)
        def _():
            load(tail, 1, xtail, sem_in.at[2]).wait()
            ytail[...] = ffn(xtail, MOE_BLOCK)
            store(tail, 1, ytail, sem_out.at[2]).start()

        @pl.when(n_groups >= 2)
        def _():
            group_store(n_groups - 2, n_groups % 2).wait()

        @pl.when(n_groups >= 1)
        def _():
            group_store(n_groups - 1, (n_groups - 1) % 2).wait()

        @pl.when(has_tail)
        def _():
            store(tail, 1, ytail, sem_out.at[2]).wait()

    @pl.when(e == N_EXPERTS - 1)
    def _():
        zeros[...] = jnp.zeros_like(zeros)

        def clear(k):
            blk = pl.multiple_of((nb_ref[0] + k) * MOE_BLOCK, MOE_BLOCK)
            return pltpu.make_async_copy(zeros, y_hbm.at[pl.ds(blk, MOE_BLOCK), :], sem_out.at[0])

        for k in range(N_EXPERTS):
            @pl.when(nb_ref[0] + k < n_blocks)
            def _():
                clear(k).start()
        for k in range(N_EXPERTS):
            @pl.when(nb_ref[0] + k < n_blocks)
            def _():
                clear(k).wait()


def _experts(layer, first_block, n_block, n_used, xs_tiles, w_gu, b_gu, w_down, b_down):
    n_slots = xs_tiles.shape[0] // ROW_TILE
    d = w_gu.shape[2]
    de2 = w_gu.shape[3]
    by_expert = lambda e, *_: (layer, e, 0, 0)
    return pl.pallas_call(
        functools.partial(_expert_kernel, layer=layer, n_blocks=n_slots // MOE_BLOCK),
        grid_spec=pltpu.PrefetchScalarGridSpec(
            num_scalar_prefetch=3,
            grid=(N_EXPERTS,),
            in_specs=[pl.BlockSpec(memory_space=pl.ANY),
                      pl.BlockSpec(memory_space=pl.ANY),
                      pl.BlockSpec((1, 1, 1, de2), by_expert),
                      pl.BlockSpec(memory_space=pl.ANY),
                      pl.BlockSpec((1, 1, 1, d), by_expert)],
            out_specs=pl.BlockSpec(memory_space=pl.ANY),
            scratch_shapes=[pltpu.VMEM((2, d, de2), F32), pltpu.VMEM((2, D_EXPERT, d), F32),
                            pltpu.VMEM((d, de2), BF16), pltpu.VMEM((D_EXPERT, d), BF16),
                            pltpu.VMEM((2, EXPERT_GROUP * MOE_BLOCK * ROW_TILE, LANES), F32),
                            pltpu.VMEM((2, EXPERT_GROUP * MOE_BLOCK, d), BF16),
                            pltpu.VMEM((MOE_BLOCK * ROW_TILE, LANES), F32),
                            pltpu.VMEM((MOE_BLOCK, d), BF16),
                            pltpu.VMEM((MOE_BLOCK, d), BF16),
                            pltpu.SemaphoreType.DMA((2, 2)),
                            pltpu.SemaphoreType.DMA((3,)), pltpu.SemaphoreType.DMA((3,))]),
        out_shape=jax.ShapeDtypeStruct((n_slots, d), BF16),
        compiler_params=_params(("arbitrary",)),
        name="moe_experts",
    )(first_block, n_block, n_used, xs_tiles, w_gu, b_gu.reshape(b_gu.shape[0], N_EXPERTS, 1, de2), w_down,
      b_down.reshape(b_down.shape[0], N_EXPERTS, 1, d))


CHUNK = 16
COMBINE_ROWS = ROUTER_TILE * TOP_K + 2 * N_EXPERTS * CHUNK


def _combine_kernel(a_ref, nch_ref, tot_ref, y_hbm, x1_ref, col_ref, gate_ref, g_ref, out_ref, ybuf, sems,
                    *, tm, n_tiles, final):
    i = pl.program_id(0)
    slot = i % 2

    @pl.when(i == 0)
    def _():
        ybuf[...] = jnp.zeros_like(ybuf)

    def fetch(tile, buf):
        pos = jnp.int32(0)
        for e in range(N_EXPERTS):
            a = a_ref[tile * N_EXPERTS + e]

            def one(c, pos):
                pltpu.make_async_copy(y_hbm.at[pl.ds(pl.multiple_of(a + c * CHUNK, CHUNK), CHUNK), :],
                                      ybuf.at[buf, pl.ds(pl.multiple_of(pos, CHUNK), CHUNK), :],
                                      sems.at[buf]).start()
                return pos + CHUNK

            pos = lax.fori_loop(0, nch_ref[tile * N_EXPERTS + e], one, pos)

    @pl.when(i == 0)
    def _():
        fetch(0, 0)

    @pl.when(i + 1 < n_tiles)
    def _():
        fetch(i + 1, 1 - slot)

    col = col_ref[...]
    gates = gate_ref[...]
    col_id = lax.broadcasted_iota(jnp.int32, (tm, COMBINE_ROWS), 1)
    g = jnp.zeros((tm, COMBINE_ROWS), F32)
    for kk in range(TOP_K):
        g = jnp.where(col_id == col[:, kk:kk + 1], gates[:, kk:kk + 1], g)

    def wait_one(c, carry):
        pltpu.make_async_copy(y_hbm.at[pl.ds(0, CHUNK), :], ybuf.at[slot, pl.ds(0, CHUNK), :], sems.at[slot]).wait()
        return carry

    lax.fori_loop(0, tot_ref[i], wait_one, 0)
    x = x1_ref[...] + jnp.dot(g.astype(BF16), ybuf[slot], preferred_element_type=F32)
    out_ref[...] = _rms(x, g_ref[...]) if final else x


def _combine(seg_a, seg_nch, tile_chunks, y_slots, x1, col_tk, gates_tk, final_g, final):
    t, d = x1.shape
    tm = ROUTER_TILE
    n_tiles = t // tm
    tok = lambda i, *_: (i, 0)
    return pl.pallas_call(
        functools.partial(_combine_kernel, tm=tm, n_tiles=n_tiles, final=final),
        grid_spec=pltpu.PrefetchScalarGridSpec(
            num_scalar_prefetch=3,
            grid=(n_tiles,),
            in_specs=[pl.BlockSpec(memory_space=pl.ANY),
                      pl.BlockSpec((tm, d), tok),
                      pl.BlockSpec((tm, TOP_K), tok),
                      pl.BlockSpec((tm, TOP_K), tok),
                      pl.BlockSpec((1, d), lambda i, *_: (0, 0))],
            out_specs=pl.BlockSpec((tm, d), tok),
            scratch_shapes=[pltpu.VMEM((2, COMBINE_ROWS, d), BF16), pltpu.SemaphoreType.DMA((2,))]),
        out_shape=jax.ShapeDtypeStruct((t, d), F32),
        compiler_params=_params(("arbitrary",)),
        name="moe_combine",
    )(seg_a, seg_nch, tile_chunks, y_slots, x1, col_tk, gates_tk, final_g)


def _slot_layout(top_idx, rank, counts, tile_base):
    counts = counts.reshape(-1)
    padded = (counts + MOE_BLOCK - 1) // MOE_BLOCK * MOE_BLOCK
    pad_end = jnp.cumsum(padded).astype(jnp.int32)
    pad_start = pad_end - padded
    experts = jnp.arange(N_EXPERTS, dtype=jnp.int32)
    start_of = jnp.sum(jnp.where(top_idx[..., None] == experts, pad_start, 0), axis=-1)
    dest = (start_of + rank).astype(jnp.int32)
    n_used = pad_end[-1] // MOE_BLOCK
    base = tile_base.reshape(-1, N_EXPERTS)
    run_start = pad_start[None, :] + base
    run_len = jnp.concatenate([base[1:], counts[None, :]], axis=0) - base
    seg_a = run_start // CHUNK * CHUNK
    seg_nch = jnp.where(run_len > 0, (run_start + run_len - seg_a + CHUNK - 1) // CHUNK, 0)
    buf_row0 = (jnp.cumsum(seg_nch, axis=1) - seg_nch) * CHUNK
    shift = jnp.repeat(buf_row0 - seg_a, ROUTER_TILE, axis=0)
    col = dest + jnp.sum(jnp.where(top_idx[..., None] == experts, shift[None], 0), axis=-1)
    flat = lambda a: a.reshape(-1).astype(jnp.int32)
    return (dest, pad_end, padded.astype(jnp.int32), n_used.reshape(1).astype(jnp.int32),
            flat(seg_a), flat(seg_nch), flat(jnp.sum(seg_nch, axis=1)), col.astype(jnp.int32))


def _rope_tables(seq):
    half = HEAD_DIM // 2
    inv_freq = ROPE_THETA ** (-jnp.arange(half, dtype=F32) / half)
    ang = jnp.arange(seq, dtype=F32)[:, None] * inv_freq[None, :]
    cos, sin = jnp.cos(ang), jnp.sin(ang)
    reps = LANES // HEAD_DIM
    cos_l = jnp.tile(jnp.concatenate([cos, cos], axis=1), (1, reps))
    sin_l = jnp.tile(jnp.concatenate([-sin, sin], axis=1), (1, reps))
    return cos_l, sin_l


def kernel(x, mem, norm1_g, w_in, pool_w, pool_scale, mem_norm_g, w_mem_kv, grp_norm_g, w_out, norm2_g,
           router_w, router_b, w_gu, b_gu, w_down, b_down, final_g):
    batch, seq, d = x.shape
    depth = w_in.shape[0]
    n_mem = mem.shape[1]
    t = batch * seq
    n_blocks = t * TOP_K // MOE_BLOCK + N_EXPERTS
    cos_l, sin_l = _rope_tables(seq)
    x2 = x.reshape(t, d)
    mem2 = mem.reshape(batch * n_mem, d)
    row = lambda a: a.reshape(1, -1)
    for l in range(depth):
        qkv, u, qm = _in_proj(x2, row(norm1_g[l]), w_in[l].astype(BF16), cos_l, sin_l, batch, seq)
        attn = [_dilated_attention(*qkv[n]) for n in range(len(DILATIONS))]
        kv = _mem_kv(mem2, row(mem_norm_g[l]), w_mem_kv[l].astype(BF16)).reshape(batch, n_mem, 2 * MEM_WIDTH)
        pw_bd = jax.scipy.linalg.block_diag(*[pool_w[l, g] for g in range(len(POOL_WINDOWS))]).astype(BF16)
        rw_t = router_w[l].T
        rw_hi = rw_t.astype(BF16)
        rw_lo = (rw_t - rw_hi.astype(F32)).astype(BF16)
        x1, h2_tiles, top_idx, gates, rank, counts, tile_base = _mix_out(
            attn, u, qm, kv, x2, pw_bd, row(pool_scale[l]), row(grp_norm_g[l]), w_out[l].astype(BF16),
            row(norm2_g[l]), rw_hi, rw_lo, router_b[l].reshape(-1, 1), batch, seq)
        dest, pad_end, padded, n_used, seg_a, seg_nch, tile_chunks, col = _slot_layout(
            top_idx, rank, counts, tile_base)
        xs_tiles = _dispatch(dest.reshape(-1), pad_end, padded, n_used, h2_tiles, n_blocks * MOE_BLOCK)
        y_slots = _experts(l, (pad_end - padded) // MOE_BLOCK, padded // MOE_BLOCK, n_used, xs_tiles,
                           w_gu, b_gu, w_down, b_down)
        x2 = _combine(seg_a, seg_nch, tile_chunks, y_slots, x1, col.T, gates.T, row(final_g),
                      final=(l == depth - 1))
    return x2.reshape(batch, seq, d)
```

```python
import functools

import jax
import jax.numpy as jnp
from jax import lax
from jax.experimental import pallas as pl
from jax.experimental.pallas import tpu as pltpu

D_MODEL = 1024
HEAD_DIM = 64
ATTN_WIDTH = 512
DILATIONS = (1, 4, 16)
BAND = 64
ROPE_THETA = 10000.0
POOL_WINDOWS = (2, 4, 8, 16)
POOL_WIDTH = 256
POOL_GROUP = 64
POOL_HALO = 8
MEM_WIDTH = 256
N_EXPERTS = 32
TOP_K = 4
D_EXPERT = 1024
SWIGLU_ALPHA = 1.702
SWIGLU_LIMIT = 7.0
MOE_BLOCK = 256
ROUTER_TILE = 512
NORM_EPS = 1e-5
NEG_INF = -1e30
LANES = 128
SUBLANES = 8
ROW_TILE = D_MODEL // LANES

F32 = jnp.float32
BF16 = jnp.bfloat16
VMEM_LIMIT = 56 * 1024 * 1024

_NT = (((1,), (1,)), ((), ()))


def _params(sem, vmem=VMEM_LIMIT):
    return pltpu.CompilerParams(dimension_semantics=sem, vmem_limit_bytes=vmem)


def _rms(x, g):
    return x * lax.rsqrt(jnp.mean(x * x, axis=-1, keepdims=True) + NORM_EPS) * g


def _to_row_tiles(ref, val):
    m = val.shape[0]
    for s in range(ROW_TILE):
        ref[pl.ds(s, m, stride=ROW_TILE), :] = val[:, s * LANES:(s + 1) * LANES]


def _from_row_tiles(ref, start, m):
    return jnp.concatenate([ref[pl.ds(start * ROW_TILE + s, m, stride=ROW_TILE), :] for s in range(ROW_TILE)],
                           axis=1)


def _in_proj_kernel(x_ref, g_ref, w_ref, cos_ref, sin_ref, q1, k1, v1, q4, k4, v4, q16, k16, v16, u_ref, qm_ref,
                    qkv, cls4, *, tm):
    h = _rms(x_ref[...], g_ref[...]).astype(BF16)
    proj = jnp.dot(h, w_ref[...], preferred_element_type=F32)
    cos = cos_ref[...]
    sin = sin_ref[...]
    lane = lax.broadcasted_iota(jnp.int32, cos.shape, 1)
    first_half = (lane % HEAD_DIM) < (HEAD_DIM // 2)
    scale = HEAD_DIM ** -0.5

    def rope(t):
        partner = jnp.where(first_half, pltpu.roll(t, LANES - HEAD_DIM // 2, 1),
                            pltpu.roll(t, HEAD_DIM // 2, 1))
        return t * cos + partner * sin

    a = ATTN_WIDTH
    groups = a // LANES
    for c in range(groups):
        qkv[c] = rope(proj[:, c * LANES:(c + 1) * LANES]) * scale
        qkv[groups + c] = rope(proj[:, a + c * LANES:a + (c + 1) * LANES])
        qkv[2 * groups + c] = proj[:, 2 * a + c * LANES:2 * a + (c + 1) * LANES]
    u_ref[...] = proj[:, 3 * a:3 * a + POOL_WIDTH]
    qm_ref[...] = (proj[:, 3 * a + POOL_WIDTH:] * scale).astype(BF16)

    for n, (r1, r4, r16) in enumerate(((q1, q4, q16), (k1, k4, k16), (v1, v4, v16))):
        for c in range(groups):
            g = n * groups + c
            cols = slice(c * LANES, (c + 1) * LANES)
            r1[0, 0, :, cols] = qkv[g].astype(BF16)
            for a in range(4):
                rows = qkv[g, pl.ds(a, tm // 4, stride=4), :]
                r4[0, a, :, cols] = rows.astype(BF16)
                cls4[g * 4 + a] = rows
            for a in range(4):
                for b in range(4):
                    r16[0, a + 4 * b, :, cols] = cls4[g * 4 + a, pl.ds(b, tm // 16, stride=4), :].astype(BF16)


def _in_proj(x2, g, w_bf, cos, sin, batch, seq):
    t, d = x2.shape
    tm = 512
    tiles_per_seq = seq // tm
    cols = w_bf.shape[1]
    row = lambda i: (i, 0)
    fixed = lambda i: (0, 0)
    cls = lambda i: (i // tiles_per_seq, 0, i % tiles_per_seq, 0)
    cls_specs, cls_shapes = [], []
    for dil in DILATIONS:
        for _ in range(3):
            cls_specs.append(pl.BlockSpec((1, dil, tm // dil, ATTN_WIDTH), cls))
            cls_shapes.append(jax.ShapeDtypeStruct((batch, dil, seq // dil, ATTN_WIDTH), BF16))
    outs = pl.pallas_call(
        functools.partial(_in_proj_kernel, tm=tm),
        grid=(t // tm,),
        in_specs=[pl.BlockSpec((tm, d), row),
                  pl.BlockSpec((1, d), fixed),
                  pl.BlockSpec((d, cols), fixed),
                  pl.BlockSpec((tm, LANES), lambda i: (i % tiles_per_seq, 0)),
                  pl.BlockSpec((tm, LANES), lambda i: (i % tiles_per_seq, 0))],
        out_specs=cls_specs + [pl.BlockSpec((tm, POOL_WIDTH), row), pl.BlockSpec((tm, MEM_WIDTH), row)],
        out_shape=cls_shapes + [jax.ShapeDtypeStruct((t, POOL_WIDTH), F32),
                                jax.ShapeDtypeStruct((t, MEM_WIDTH), BF16)],
        scratch_shapes=[pltpu.VMEM((3 * ATTN_WIDTH // LANES, tm, LANES), F32),
                        pltpu.VMEM((4 * 3 * ATTN_WIDTH // LANES, tm // 4, LANES), F32)],
        compiler_params=_params(("parallel",)),
        name="in_proj",
    )(x2, g, w_bf, cos, sin)
    qkv = [outs[3 * n:3 * n + 3] for n in range(len(DILATIONS))]
    return qkv, outs[-2], outs[-1]


def _head_pair_attention(q, k, v, valid):
    lane = lax.broadcasted_iota(jnp.int32, q.shape, 1)
    outs, lses = [], []
    for half in range(2):
        mine = (lane // HEAD_DIM) == half
        s = lax.dot_general(jnp.where(mine, q, jnp.zeros_like(q)), k, _NT,
                            preferred_element_type=F32)
        if valid is not None:
            s = jnp.where(valid, s, NEG_INF)
        m = jnp.max(s, axis=-1, keepdims=True)
        p = jnp.exp(s - m)
        den = jnp.sum(p, axis=-1, keepdims=True)
        pv = jnp.dot(p.astype(BF16), v, preferred_element_type=F32)
        outs.append(pv / den)
        lses.append(m + jnp.log(den))
    return jnp.where((lane // HEAD_DIM) == 0, outs[0], outs[1]), lses


def _attn_kernel(q_ref, kp_ref, kc_ref, kn_ref, vp_ref, vc_ref, vn_ref, o_ref, lse_ref, *, tq, sub, length):
    j = pl.program_id(2)
    nk = sub + 2 * BAND
    row = lax.broadcasted_iota(jnp.int32, (sub, nk), 0)
    col = lax.broadcasted_iota(jnp.int32, (sub, nk), 1)
    in_band = jnp.abs(col - BAND - row) <= BAND
    lane = lax.broadcasted_iota(jnp.int32, (sub, LANES), 1)
    for c in range(ATTN_WIDTH // LANES):
        sl = slice(c * LANES, (c + 1) * LANES)
        k = jnp.concatenate([kp_ref[0, 0, :, sl], kc_ref[0, 0, :, sl], kn_ref[0, 0, :, sl]], axis=0)
        v = jnp.concatenate([vp_ref[0, 0, :, sl], vc_ref[0, 0, :, sl], vn_ref[0, 0, :, sl]], axis=0)
        for s in range(tq // sub):
            rows = slice(s * sub, (s + 1) * sub)
            key = j * tq + s * sub - BAND + col
            valid = in_band & (key >= 0) & (key < length)
            o, lses = _head_pair_attention(q_ref[0, 0, rows, sl], k[s * sub:s * sub + nk], v[s * sub:s * sub + nk],
                                           valid)
            o_ref[0, 0, rows, sl] = o
            lse_tile = jnp.where(lane == 2 * c, lses[0], lses[1])
            if c == 0:
                lse_ref[0, 0, rows, :] = lse_tile
            else:
                lse_ref[0, 0, rows, :] = jnp.where((lane // 2) == c, lse_tile, lse_ref[0, 0, rows, :])


def _dilated_attention(q, k, v):
    batch, dilation, length, _ = q.shape
    tq = 512
    per = tq // BAND
    n_band_blocks = length // BAND
    cur = lambda b, r, j: (b, r, j, 0)
    prev = lambda b, r, j: (b, r, jnp.maximum(j * per - 1, 0), 0)
    nxt = lambda b, r, j: (b, r, jnp.minimum((j + 1) * per, n_band_blocks - 1), 0)
    big = lambda imap: pl.BlockSpec((1, 1, tq, ATTN_WIDTH), imap)
    halo = lambda imap: pl.BlockSpec((1, 1, BAND, ATTN_WIDTH), imap)
    return pl.pallas_call(
        functools.partial(_attn_kernel, tq=tq, sub=128, length=length),
        grid=(batch, dilation, length // tq),
        in_specs=[big(cur), halo(prev), big(cur), halo(nxt), halo(prev), big(cur), halo(nxt)],
        out_specs=[pl.BlockSpec((1, 1, tq, ATTN_WIDTH), cur), pl.BlockSpec((1, 1, tq, LANES), cur)],
        out_shape=[jax.ShapeDtypeStruct((batch, dilation, length, ATTN_WIDTH), F32),
                   jax.ShapeDtypeStruct((batch, dilation, length, LANES), F32)],
        compiler_params=_params(("parallel", "parallel", "parallel")),
        name=f"dilated_attn_d{dilation}",
    )(q, k, k, k, v, v, v)


def _mem_kv_kernel(m_ref, g_ref, w_ref, kv_ref):
    h = _rms(m_ref[...], g_ref[...]).astype(BF16)
    kv_ref[...] = jnp.dot(h, w_ref[...], preferred_element_type=F32).astype(BF16)


def _mem_kv(mem2, g, w_bf):
    n, d = mem2.shape
    cols = w_bf.shape[1]
    tm = 256
    return pl.pallas_call(
        _mem_kv_kernel,
        grid=(n // tm,),
        in_specs=[pl.BlockSpec((tm, d), lambda i: (i, 0)),
                  pl.BlockSpec((1, d), lambda i: (0, 0)),
                  pl.BlockSpec((d, cols), lambda i: (0, 0))],
        out_specs=pl.BlockSpec((tm, cols), lambda i: (i, 0)),
        out_shape=jax.ShapeDtypeStruct((n, cols), BF16),
        compiler_params=_params(("parallel",)),
        name="mem_kv",
    )(mem2, g, w_bf)


def _mix_out_kernel(o1_ref, o4_ref, o16_ref, l1_ref, l4_ref, l16_ref, up_ref, u_ref, un_ref, qm_ref, kv_ref,
                    x_ref, pw_ref, ps_ref, gg_ref, wo_ref, n2_ref, rwh_ref, rwl_ref, rb_ref,
                    x1_ref, h2_ref, idx_ref, gate_ref, rank_ref, cnt_ref, base_ref,
                    o4_s, o16_s, l4_s, l16_s, cls4, carry, *, tm, seq):
    i = pl.program_id(0)
    tiles_per_seq = seq // tm
    pos0 = (i % tiles_per_seq) * tm

    for src4, src16, dst4, dst16 in ((o4_ref, o16_ref, o4_s, o16_s), (l4_ref, l16_ref, l4_s, l16_s)):
        for c in range(dst4.shape[0]):
            cols = slice(c * LANES, (c + 1) * LANES)
            for a in range(4):
                dst4[c, pl.ds(a, tm // 4, stride=4), :] = src4[0, a, :, cols]
                for b in range(4):
                    cls4[a, pl.ds(b, tm // 16, stride=4), :] = src16[0, a + 4 * b, :, cols]
            for a in range(4):
                dst16[c, pl.ds(a, tm // 4, stride=4), :] = cls4[a]

    l1, l2, l3 = l1_ref[0, 0], l4_s[0], l16_s[0]
    lm = jnp.maximum(jnp.maximum(l1, l2), l3)
    e1, e2, e3 = jnp.exp(l1 - lm), jnp.exp(l2 - lm), jnp.exp(l3 - lm)
    es = e1 + e2 + e3
    w1, w2, w3 = e1 / es, e2 / es, e3 / es
    ya = []
    for c in range(ATTN_WIDTH // LANES):
        sl = slice(c * LANES, (c + 1) * LANES)
        lane = lax.broadcasted_iota(jnp.int32, (tm, LANES), 1)
        lo = lane < HEAD_DIM

        def per_head(w):
            return jnp.where(lo, w[:, 2 * c:2 * c + 1], w[:, 2 * c + 1:2 * c + 2])

        ya.append(per_head(w1) * o1_ref[0, 0, :, sl] + per_head(w2) * o4_s[c] + per_head(w3) * o16_s[c])
    ya = jnp.concatenate(ya, axis=1)

    u = u_ref[...]
    before = jnp.where(pos0 > 0, up_ref[...], 0.0)
    after = jnp.where(pos0 + tm < seq, un_ref[...], 0.0)
    ext = jnp.concatenate([before, u, after], axis=0)
    n_ext = tm + 2 * POOL_HALO
    sums = []
    acc = ext
    shift = 1
    for w in POOL_WINDOWS:
        if w == 2:
            acc = pltpu.roll(ext, 1, 0) + ext
        else:
            acc = pltpu.roll(acc, shift, 0) + pltpu.roll(acc, n_ext - shift, 0)
            shift *= 2
        sums.append(acc[POOL_HALO:POOL_HALO + tm])
    pos = pos0 + lax.broadcasted_iota(jnp.int32, (tm, 1), 0)
    grp = lax.broadcasted_iota(jnp.int32, (tm, POOL_WIDTH), 1) // POOL_GROUP
    mean = jnp.zeros((tm, POOL_WIDTH), F32)
    for g, w in enumerate(POOL_WINDOWS):
        cnt = jnp.minimum(pos + (w - 1 - w // 2), seq - 1) + 1 - jnp.maximum(pos - w // 2, 0)
        mean = jnp.where(grp == g, sums[g] / cnt.astype(F32), mean)
    d = (mean - u).astype(BF16)
    yp = jnp.dot(d, pw_ref[...], preferred_element_type=F32) * ps_ref[...]

    ym = []
    for c in range(MEM_WIDTH // LANES):
        sl = slice(c * LANES, (c + 1) * LANES)
        o, _ = _head_pair_attention(qm_ref[:, sl], kv_ref[0, :, sl],
                                    kv_ref[0, :, MEM_WIDTH + c * LANES:MEM_WIDTH + (c + 1) * LANES], None)
        ym.append(o)
    ym = jnp.concatenate(ym, axis=1)

    gg = gg_ref[...]
    a, p = ATTN_WIDTH, POOL_WIDTH
    y = jnp.concatenate([_rms(ya, gg[:, :a]), _rms(yp, gg[:, a:a + p]), _rms(ym, gg[:, a + p:])], axis=1)
    x1 = x_ref[...] + jnp.dot(y.astype(BF16), wo_ref[...], preferred_element_type=F32)
    x1_ref[...] = x1

    h2 = _rms(x1, n2_ref[...])
    _to_row_tiles(h2_ref, h2)

    hi = h2.astype(BF16)
    lo = (h2 - hi.astype(F32)).astype(BF16)
    logits = (lax.dot_general(rwh_ref[...], hi, _NT, preferred_element_type=F32)
              + lax.dot_general(rwh_ref[...], lo, _NT, preferred_element_type=F32)
              + lax.dot_general(rwl_ref[...], hi, _NT, preferred_element_type=F32)) + rb_ref[...]
    eidx = lax.broadcasted_iota(jnp.int32, logits.shape, 0)
    krow = lax.broadcasted_iota(jnp.int32, (TOP_K, tm), 0)
    vals = jnp.zeros((TOP_K, tm), F32)
    idxs = jnp.zeros((TOP_K, tm), jnp.int32)
    work = logits
    args = []
    for kk in range(TOP_K):
        best = jnp.max(work, axis=0, keepdims=True)
        arg = jnp.min(jnp.where(work == best, eidx, N_EXPERTS), axis=0, keepdims=True)
        vals = jnp.where(krow == kk, best, vals)
        idxs = jnp.where(krow == kk, arg, idxs)
        work = jnp.where(eidx == arg, -jnp.inf, work)
        args.append(arg)
    ex = jnp.exp(vals - vals[0:1])
    gate_ref[...] = ex / jnp.sum(ex, axis=0, keepdims=True)
    idx_ref[...] = idxs

    @pl.when(i == 0)
    def _():
        carry[...] = jnp.zeros_like(carry)

    chosen = (work == -jnp.inf).astype(BF16)
    earlier = (lax.broadcasted_iota(jnp.int32, (tm, tm), 0)
               < lax.broadcasted_iota(jnp.int32, (tm, tm), 1)).astype(BF16)
    before_me = jnp.dot(chosen, earlier, preferred_element_type=F32) + carry[...]
    ranks = jnp.zeros((TOP_K, tm), F32)
    for kk in range(TOP_K):
        ranks = jnp.where(krow == kk, jnp.sum(jnp.where(eidx == args[kk], before_me, 0.0), axis=0, keepdims=True),
                          ranks)
    rank_ref[...] = ranks.astype(jnp.int32)
    base_ref[0] = carry[...].astype(jnp.int32)
    carry[...] = carry[...] + jnp.sum(chosen.astype(F32), axis=1, keepdims=True)
    cnt_ref[...] = carry[...].astype(jnp.int32)


def _mix_out(attn, u, qm, kv, x2, pw_bd, ps, gg, wo_bf, n2, rw_hi, rw_lo, rb, batch, seq):
    t, d = x2.shape
    tm = ROUTER_TILE
    hp = tm // POOL_HALO
    n_halo = t // POOL_HALO
    tiles_per_seq = seq // tm
    row = lambda i: (i, 0)
    fixed = lambda i: (0, 0)
    cls = lambda i: (i // tiles_per_seq, 0, i % tiles_per_seq, 0)
    rowspec = lambda w: pl.BlockSpec((tm, w), row)
    clsspec = lambda dil, w: pl.BlockSpec((1, dil, tm // dil, w), cls)
    tok_cols = lambda i: (0, i)
    (o1, l1), (o4, l4), (o16, l16) = attn
    return pl.pallas_call(
        functools.partial(_mix_out_kernel, tm=tm, seq=seq),
        grid=(t // tm,),
        in_specs=[clsspec(1, ATTN_WIDTH), clsspec(4, ATTN_WIDTH), clsspec(16, ATTN_WIDTH),
                  clsspec(1, LANES), clsspec(4, LANES), clsspec(16, LANES),
                  pl.BlockSpec((POOL_HALO, POOL_WIDTH), lambda i: (jnp.maximum(i * hp - 1, 0), 0)),
                  rowspec(POOL_WIDTH),
                  pl.BlockSpec((POOL_HALO, POOL_WIDTH), lambda i: (jnp.minimum((i + 1) * hp, n_halo - 1), 0)),
                  rowspec(MEM_WIDTH),
                  pl.BlockSpec((1, kv.shape[1], kv.shape[2]), lambda i: (i // tiles_per_seq, 0, 0)),
                  rowspec(d),
                  pl.BlockSpec(pw_bd.shape, fixed),
                  pl.BlockSpec(ps.shape, fixed),
                  pl.BlockSpec(gg.shape, fixed),
                  pl.BlockSpec(wo_bf.shape, fixed),
                  pl.BlockSpec(n2.shape, fixed),
                  pl.BlockSpec(rw_hi.shape, fixed),
                  pl.BlockSpec(rw_lo.shape, fixed),
                  pl.BlockSpec(rb.shape, fixed)],
        out_specs=[rowspec(d),
                   pl.BlockSpec((tm * ROW_TILE, LANES), row),
                   pl.BlockSpec((TOP_K, tm), tok_cols),
                   pl.BlockSpec((TOP_K, tm), tok_cols),
                   pl.BlockSpec((TOP_K, tm), tok_cols),
                   pl.BlockSpec((N_EXPERTS, 1), fixed),
                   pl.BlockSpec((1, N_EXPERTS, 1), lambda i: (i, 0, 0))],
        out_shape=[jax.ShapeDtypeStruct((t, d), F32),
                   jax.ShapeDtypeStruct((t * ROW_TILE, LANES), F32),
                   jax.ShapeDtypeStruct((TOP_K, t), jnp.int32),
                   jax.ShapeDtypeStruct((TOP_K, t), F32),
                   jax.ShapeDtypeStruct((TOP_K, t), jnp.int32),
                   jax.ShapeDtypeStruct((N_EXPERTS, 1), jnp.int32),
                   jax.ShapeDtypeStruct((t // tm, N_EXPERTS, 1), jnp.int32)],
        scratch_shapes=[pltpu.VMEM((ATTN_WIDTH // LANES, tm, LANES), F32),
                        pltpu.VMEM((ATTN_WIDTH // LANES, tm, LANES), F32),
                        pltpu.VMEM((1, tm, LANES), F32), pltpu.VMEM((1, tm, LANES), F32),
                        pltpu.VMEM((4, tm // 4, LANES), F32),
                        pltpu.VMEM((N_EXPERTS, 1), F32)],
        compiler_params=_params(("arbitrary",)),
        name="mix_out_router",
    )(o1, o4, o16, l1, l4, l16, u, u, u, qm, kv, x2, pw_bd, ps, gg, wo_bf, n2, rw_hi, rw_lo, rb)


def _dispatch_kernel(dest_ref, pend_ref, padded_ref, nb_ref, h_ref, xs_hbm, zeros, sem, *, tm, n_tok, n_blocks):
    i = pl.program_id(0)
    blk = MOE_BLOCK * ROW_TILE

    @pl.when(i == 0)
    def _():
        zeros[...] = jnp.zeros_like(zeros)

        def clear(block):
            return pltpu.make_async_copy(zeros, xs_hbm.at[pl.ds(pl.multiple_of(block * blk, blk), blk), :], sem)

        def for_each_cleared_block(fn):
            for e in range(N_EXPERTS):
                @pl.when(padded_ref[e] > 0)
                def _():
                    fn(clear(pend_ref[e] // MOE_BLOCK - 1))

                @pl.when(nb_ref[0] + e < n_blocks)
                def _():
                    fn(clear(nb_ref[0] + e))

        for_each_cleared_block(lambda c: c.start())
        for_each_cleared_block(lambda c: c.wait())

    def start(r, c):
        src = h_ref.at[pl.ds(pl.multiple_of(r * ROW_TILE, ROW_TILE), ROW_TILE), :]
        for kk in range(TOP_K):
            slot = dest_ref[kk * n_tok + i * tm + r]
            pltpu.async_copy(src, xs_hbm.at[pl.ds(pl.multiple_of(slot * ROW_TILE, ROW_TILE), ROW_TILE), :],
                             sem, priority=kk % 2)
        return c

    lax.fori_loop(0, tm, start, 0, unroll=4)
    n = tm * TOP_K * ROW_TILE
    pltpu.make_async_copy(xs_hbm.at[pl.ds(0, n), :], xs_hbm.at[pl.ds(0, n), :], sem).wait()


def _dispatch(dest, pad_end, padded, n_used, h2_tiles, n_slots):
    n_tok = h2_tiles.shape[0] // ROW_TILE
    tm = 512
    return pl.pallas_call(
        functools.partial(_dispatch_kernel, tm=tm, n_tok=n_tok, n_blocks=n_slots // MOE_BLOCK),
        grid_spec=pltpu.PrefetchScalarGridSpec(
            num_scalar_prefetch=4,
            grid=(n_tok // tm,),
            in_specs=[pl.BlockSpec((tm * ROW_TILE, LANES), lambda i, *_: (i, 0))],
            out_specs=pl.BlockSpec(memory_space=pl.ANY),
            scratch_shapes=[pltpu.VMEM((MOE_BLOCK * ROW_TILE, LANES), F32), pltpu.SemaphoreType.DMA(())]),
        out_shape=jax.ShapeDtypeStruct((n_slots * ROW_TILE, LANES), F32),
        compiler_params=_params(("arbitrary",)),
        name="moe_dispatch",
    )(dest, pad_end, padded, n_used, h2_tiles)


EXPERT_GROUP = 2


WEIGHT_DMA_QUEUE = 1


def _expert_kernel(first_ref, nblk_ref, nb_ref, xs_hbm, wgu_hbm, bgu_ref, wdn_hbm, bdn_ref, y_hbm,
                   wgu_f32, wdn_f32, wgu_bf, wdn_bf, xbuf, ybuf, xtail, ytail, zeros, sem_w, sem_in, sem_out,
                   *, layer, n_blocks):
    e = pl.program_id(0)
    first = first_ref[e]
    n = nblk_ref[e]
    rows = MOE_BLOCK * ROW_TILE
    n_groups = n // EXPERT_GROUP
    tail = first + n_groups * EXPERT_GROUP

    def load(block, count, dst, sem):
        return pltpu.make_async_copy(xs_hbm.at[pl.ds(pl.multiple_of(block * rows, rows), count * rows), :], dst, sem)

    def store(block, count, src, sem):
        return pltpu.make_async_copy(src, y_hbm.at[pl.ds(pl.multiple_of(block * MOE_BLOCK, MOE_BLOCK),
                                                         count * MOE_BLOCK), :], sem)

    def group_load(j, s):
        return load(first + j * EXPERT_GROUP, EXPERT_GROUP, xbuf.at[s], sem_in.at[s])

    def group_store(j, s):
        return store(first + j * EXPERT_GROUP, EXPERT_GROUP, ybuf.at[s], sem_out.at[s])

    def ffn(x_tiles, m):
        x = _from_row_tiles(x_tiles, 0, m).astype(BF16)
        gu = jnp.dot(x, wgu_bf[...], preferred_element_type=F32) + bgu_ref[0, 0]
        gate = jnp.minimum(gu[:, :D_EXPERT], SWIGLU_LIMIT)
        up = jnp.clip(gu[:, D_EXPERT:], -SWIGLU_LIMIT, SWIGLU_LIMIT)
        act = (up + 1.0) * gate * jax.nn.sigmoid(SWIGLU_ALPHA * gate)
        return (jnp.dot(act.astype(BF16), wdn_bf[...], preferred_element_type=F32) + bdn_ref[0, 0]).astype(BF16)

    def weights(ex, fn):
        s = ex % 2
        fn(wgu_hbm.at[layer, ex], wgu_f32.at[s], sem_w.at[0, s])
        fn(wdn_hbm.at[layer, ex], wdn_f32.at[s], sem_w.at[1, s])

    def w_start(src, dst, sem):
        pltpu.async_copy(src, dst, sem, priority=WEIGHT_DMA_QUEUE)

    def w_wait(src, dst, sem):
        pltpu.make_async_copy(src, dst, sem).wait()

    @pl.when(e == 0)
    def _():
        weights(0, w_start)

    has_tail = n_groups * EXPERT_GROUP < n

    @pl.when(n_groups > 0)
    def _():
        group_load(0, 0).start()

    @pl.when(has_tail)
    def _():
        load(tail, 1, xtail, sem_in.at[2]).start()

    @pl.when(e + 1 < N_EXPERTS)
    def _():
        weights(e + 1, w_start)

    weights(e, w_wait)

    @pl.when(n > 0)
    def _():
        wgu_bf[...] = wgu_f32[e % 2].astype(BF16)
        wdn_bf[...] = wdn_f32[e % 2].astype(BF16)

        def group(j, carry):
            s = j % 2
            group_load(j, s).wait()

            @pl.when(j + 1 < n_groups)
            def _():
                group_load(j + 1, 1 - s).start()

            y = ffn(xbuf.at[s], EXPERT_GROUP * MOE_BLOCK)

            @pl.when(j >= 2)
            def _():
                group_store(j - 2, s).wait()

            ybuf[s] = y
            group_store(j, s).start()
            return carry

        lax.fori_loop(0, n_groups, group, 0)

        @pl.when(has_tail)
        def _():
            load(tail, 1, xtail, sem_in.at[2]).wait()
            ytail[...] = ffn(xtail, MOE_BLOCK)
            store(tail, 1, ytail, sem_out.at[2]).start()

        @pl.when(n_groups >= 2)
        def _():
            group_store(n_groups - 2, n_groups % 2).wait()

        @pl.when(n_groups >= 1)
        def _():
            group_store(n_groups - 1, (n_groups - 1) % 2).wait()

        @pl.when(has_tail)
        def _():
            store(tail, 1, ytail, sem_out.at[2]).wait()

    @pl.when(e == N_EXPERTS - 1)
    def _():
        zeros[...] = jnp.zeros_like(zeros)

        def clear(k):
            blk = pl.multiple_of((nb_ref[0] + k) * MOE_BLOCK, MOE_BLOCK)
            return pltpu.make_async_copy(zeros, y_hbm.at[pl.ds(blk, MOE_BLOCK), :], sem_out.at[0])

        for k in range(N_EXPERTS):
            @pl.when(nb_ref[0] + k < n_blocks)
            def _():
                clear(k).start()
        for k in range(N_EXPERTS):
            @pl.when(nb_ref[0] + k < n_blocks)
            def _():
                clear(k).wait()


def _experts(layer, first_block, n_block, n_used, xs_tiles, w_gu, b_gu, w_down, b_down):
    n_slots = xs_tiles.shape[0] // ROW_TILE
    d = w_gu.shape[2]
    de2 = w_gu.shape[3]
    by_expert = lambda e, *_: (layer, e, 0, 0)
    return pl.pallas_call(
        functools.partial(_expert_kernel, layer=layer, n_blocks=n_slots // MOE_BLOCK),
        grid_spec=pltpu.PrefetchScalarGridSpec(
            num_scalar_prefetch=3,
            grid=(N_EXPERTS,),
            in_specs=[pl.BlockSpec(memory_space=pl.ANY),
                      pl.BlockSpec(memory_space=pl.ANY),
                      pl.BlockSpec((1, 1, 1, de2), by_expert),
                      pl.BlockSpec(memory_space=pl.ANY),
                      pl.BlockSpec((1, 1, 1, d), by_expert)],
            out_specs=pl.BlockSpec(memory_space=pl.ANY),
            scratch_shapes=[pltpu.VMEM((2, d, de2), F32), pltpu.VMEM((2, D_EXPERT, d), F32),
                            pltpu.VMEM((d, de2), BF16), pltpu.VMEM((D_EXPERT, d), BF16),
                            pltpu.VMEM((2, EXPERT_GROUP * MOE_BLOCK * ROW_TILE, LANES), F32),
                            pltpu.VMEM((2, EXPERT_GROUP * MOE_BLOCK, d), BF16),
                            pltpu.VMEM((MOE_BLOCK * ROW_TILE, LANES), F32),
                            pltpu.VMEM((MOE_BLOCK, d), BF16),
                            pltpu.VMEM((MOE_BLOCK, d), BF16),
                            pltpu.SemaphoreType.DMA((2, 2)),
                            pltpu.SemaphoreType.DMA((3,)), pltpu.SemaphoreType.DMA((3,))]),
        out_shape=jax.ShapeDtypeStruct((n_slots, d), BF16),
        compiler_params=_params(("arbitrary",)),
        name="moe_experts",
    )(first_block, n_block, n_used, xs_tiles, w_gu, b_gu.reshape(b_gu.shape[0], N_EXPERTS, 1, de2), w_down,
      b_down.reshape(b_down.shape[0], N_EXPERTS, 1, d))


CHUNK = 16
CHUNK_BATCH = 8
COMBINE_ROWS = ROUTER_TILE * TOP_K + 2 * N_EXPERTS * CHUNK
COMBINE_CHUNKS = COMBINE_ROWS // CHUNK


def _combine_kernel(src_ref, nbatch_ref, y_hbm, x1_ref, col_ref, gate_ref, g_ref, out_ref, ybuf, sems,
                    *, tm, n_tiles, final):
    i = pl.program_id(0)
    slot = i % 2
    batch_rows = CHUNK_BATCH * CHUNK

    @pl.when(i == 0)
    def _():
        ybuf[...] = jnp.zeros_like(ybuf)

    def fetch(tile, buf):
        def batch(b, carry):
            for u in range(CHUNK_BATCH):
                c = b * CHUNK_BATCH + u
                src = src_ref[tile * COMBINE_CHUNKS + c]
                pltpu.make_async_copy(y_hbm.at[pl.ds(pl.multiple_of(src, CHUNK), CHUNK), :],
                                      ybuf.at[buf, pl.ds(pl.multiple_of(c * CHUNK, CHUNK), CHUNK), :],
                                      sems.at[buf]).start()
            return carry

        lax.fori_loop(0, nbatch_ref[tile], batch, 0)

    @pl.when(i == 0)
    def _():
        fetch(0, 0)

    @pl.when(i + 1 < n_tiles)
    def _():
        fetch(i + 1, 1 - slot)

    col = col_ref[...]
    gates = gate_ref[...]
    col_id = lax.broadcasted_iota(jnp.int32, (tm, COMBINE_ROWS), 1)
    g = jnp.zeros((tm, COMBINE_ROWS), F32)
    for kk in range(TOP_K):
        g = jnp.where(col_id == col[:, kk:kk + 1], gates[:, kk:kk + 1], g)

    def wait_batch(b, carry):
        pltpu.make_async_copy(y_hbm.at[pl.ds(0, batch_rows), :], ybuf.at[slot, pl.ds(0, batch_rows), :],
                              sems.at[slot]).wait()
        return carry

    lax.fori_loop(0, nbatch_ref[i], wait_batch, 0)
    x = x1_ref[...] + jnp.dot(g.astype(BF16), ybuf[slot], preferred_element_type=F32)
    out_ref[...] = _rms(x, g_ref[...]) if final else x


def _combine(chunk_src, tile_batches, y_slots, x1, col_tk, gates_tk, final_g, final):
    t, d = x1.shape
    tm = ROUTER_TILE
    n_tiles = t // tm
    tok = lambda i, *_: (i, 0)
    return pl.pallas_call(
        functools.partial(_combine_kernel, tm=tm, n_tiles=n_tiles, final=final),
        grid_spec=pltpu.PrefetchScalarGridSpec(
            num_scalar_prefetch=2,
            grid=(n_tiles,),
            in_specs=[pl.BlockSpec(memory_space=pl.ANY),
                      pl.BlockSpec((tm, d), tok),
                      pl.BlockSpec((tm, TOP_K), tok),
                      pl.BlockSpec((tm, TOP_K), tok),
                      pl.BlockSpec((1, d), lambda i, *_: (0, 0))],
            out_specs=pl.BlockSpec((tm, d), tok),
            scratch_shapes=[pltpu.VMEM((2, COMBINE_ROWS, d), BF16), pltpu.SemaphoreType.DMA((2,))]),
        out_shape=jax.ShapeDtypeStruct((t, d), F32),
        compiler_params=_params(("arbitrary",)),
        name="moe_combine",
    )(chunk_src, tile_batches, y_slots, x1, col_tk, gates_tk, final_g)


def _slot_layout(top_idx, rank, counts, tile_base):
    counts = counts.reshape(-1)
    padded = (counts + MOE_BLOCK - 1) // MOE_BLOCK * MOE_BLOCK
    pad_end = jnp.cumsum(padded).astype(jnp.int32)
    pad_start = pad_end - padded
    experts = jnp.arange(N_EXPERTS, dtype=jnp.int32)
    start_of = jnp.sum(jnp.where(top_idx[..., None] == experts, pad_start, 0), axis=-1)
    dest = (start_of + rank).astype(jnp.int32)
    n_used = pad_end[-1] // MOE_BLOCK
    base = tile_base.reshape(-1, N_EXPERTS)
    run_start = pad_start[None, :] + base
    run_len = jnp.concatenate([base[1:], counts[None, :]], axis=0) - base
    seg_a = run_start // CHUNK * CHUNK
    seg_nch = jnp.where(run_len > 0, (run_start + run_len - seg_a + CHUNK - 1) // CHUNK, 0)
    buf_row0 = (jnp.cumsum(seg_nch, axis=1) - seg_nch) * CHUNK
    shift = jnp.repeat(buf_row0 - seg_a, ROUTER_TILE, axis=0)
    col = dest + jnp.sum(jnp.where(top_idx[..., None] == experts, shift[None], 0), axis=-1)
    chunk_end = jnp.cumsum(seg_nch, axis=1)
    c = jnp.arange(COMBINE_CHUNKS, dtype=jnp.int32)
    owner = c[None, :, None] >= chunk_end[:, None, :]
    owner_e = jnp.minimum(jnp.sum(owner.astype(jnp.int32), axis=-1), N_EXPERTS - 1)
    pick = owner_e[..., None] == experts
    first_chunk = jnp.sum(jnp.where(pick, (chunk_end - seg_nch)[:, None, :], 0), axis=-1)
    run_a = jnp.sum(jnp.where(pick, seg_a[:, None, :], 0), axis=-1)
    total = chunk_end[:, -1]
    chunk_src = jnp.where(c[None, :] < total[:, None], run_a + (c[None, :] - first_chunk) * CHUNK, 0)
    flat = lambda a: a.reshape(-1).astype(jnp.int32)
    return (dest, pad_end, padded.astype(jnp.int32), n_used.reshape(1).astype(jnp.int32),
            flat(chunk_src), flat((total + CHUNK_BATCH - 1) // CHUNK_BATCH), col.astype(jnp.int32))


def _rope_tables(seq):
    half = HEAD_DIM // 2
    inv_freq = ROPE_THETA ** (-jnp.arange(half, dtype=F32) / half)
    ang = jnp.arange(seq, dtype=F32)[:, None] * inv_freq[None, :]
    cos, sin = jnp.cos(ang), jnp.sin(ang)
    reps = LANES // HEAD_DIM
    cos_l = jnp.tile(jnp.concatenate([cos, cos], axis=1), (1, reps))
    sin_l = jnp.tile(jnp.concatenate([-sin, sin], axis=1), (1, reps))
    return cos_l, sin_l


def kernel(x, mem, norm1_g, w_in, pool_w, pool_scale, mem_norm_g, w_mem_kv, grp_norm_g, w_out, norm2_g,
           router_w, router_b, w_gu, b_gu, w_down, b_down, final_g):
    batch, seq, d = x.shape
    depth = w_in.shape[0]
    n_mem = mem.shape[1]
    t = batch * seq
    n_blocks = t * TOP_K // MOE_BLOCK + N_EXPERTS
    cos_l, sin_l = _rope_tables(seq)
    x2 = x.reshape(t, d)
    mem2 = mem.reshape(batch * n_mem, d)
    row = lambda a: a.reshape(1, -1)
    for l in range(depth):
        qkv, u, qm = _in_proj(x2, row(norm1_g[l]), w_in[l].astype(BF16), cos_l, sin_l, batch, seq)
        attn = [_dilated_attention(*qkv[n]) for n in range(len(DILATIONS))]
        kv = _mem_kv(mem2, row(mem_norm_g[l]), w_mem_kv[l].astype(BF16)).reshape(batch, n_mem, 2 * MEM_WIDTH)
        pw_bd = jax.scipy.linalg.block_diag(*[pool_w[l, g] for g in range(len(POOL_WINDOWS))]).astype(BF16)
        rw_t = router_w[l].T
        rw_hi = rw_t.astype(BF16)
        rw_lo = (rw_t - rw_hi.astype(F32)).astype(BF16)
        x1, h2_tiles, top_idx, gates, rank, counts, tile_base = _mix_out(
            attn, u, qm, kv, x2, pw_bd, row(pool_scale[l]), row(grp_norm_g[l]), w_out[l].astype(BF16),
            row(norm2_g[l]), rw_hi, rw_lo, router_b[l].reshape(-1, 1), batch, seq)
        dest, pad_end, padded, n_used, chunk_src, tile_batches, col = _slot_layout(
            top_idx, rank, counts, tile_base)
        xs_tiles = _dispatch(dest.reshape(-1), pad_end, padded, n_used, h2_tiles, n_blocks * MOE_BLOCK)
        y_slots = _experts(l, (pad_end - padded) // MOE_BLOCK, padded // MOE_BLOCK, n_used, xs_tiles,
                           w_gu, b_gu, w_down, b_down)
        x2 = _combine(chunk_src, tile_batches, y_slots, x1, col.T, gates.T, row(final_g),
                      final=(l == depth - 1))
    return x2.reshape(batch, seq, d)
```

```python
import functools

import jax
import jax.numpy as jnp
from jax import lax
from jax.experimental import pallas as pl
from jax.experimental.pallas import tpu as pltpu

D_MODEL = 1024
HEAD_DIM = 64
ATTN_WIDTH = 512
DILATIONS = (1, 4, 16)
BAND = 64
ROPE_THETA = 10000.0
POOL_WINDOWS = (2, 4, 8, 16)
POOL_WIDTH = 256
POOL_GROUP = 64
POOL_HALO = 8
MEM_WIDTH = 256
N_EXPERTS = 32
TOP_K = 4
D_EXPERT = 1024
SWIGLU_ALPHA = 1.702
SWIGLU_LIMIT = 7.0
MOE_BLOCK = 256
ROUTER_TILE = 512
COMBINE_TILE = 256
NORM_EPS = 1e-5
NEG_INF = -1e30
LANES = 128
SUBLANES = 8
ROW_TILE = D_MODEL // LANES

F32 = jnp.float32
BF16 = jnp.bfloat16
VMEM_LIMIT = 56 * 1024 * 1024

_NT = (((1,), (1,)), ((), ()))


def _params(sem, vmem=VMEM_LIMIT):
    return pltpu.CompilerParams(dimension_semantics=sem, vmem_limit_bytes=vmem)


def _rms(x, g):
    return x * lax.rsqrt(jnp.mean(x * x, axis=-1, keepdims=True) + NORM_EPS) * g


def _to_row_tiles(ref, val):
    m = val.shape[0]
    for s in range(ROW_TILE):
        ref[pl.ds(s, m, stride=ROW_TILE), :] = val[:, s * LANES:(s + 1) * LANES]


def _from_row_tiles(ref, start, m):
    return jnp.concatenate([ref[pl.ds(start * ROW_TILE + s, m, stride=ROW_TILE), :] for s in range(ROW_TILE)],
                           axis=1)


def _in_proj_kernel(x_ref, g_ref, w_ref, cos_ref, sin_ref, q1, k1, v1, q4, k4, v4, q16, k16, v16, u_ref, qm_ref,
                    qkv, cls4, *, tm):
    h = _rms(x_ref[...], g_ref[...]).astype(BF16)
    proj = jnp.dot(h, w_ref[...], preferred_element_type=F32)
    cos = cos_ref[...]
    sin = sin_ref[...]
    lane = lax.broadcasted_iota(jnp.int32, cos.shape, 1)
    first_half = (lane % HEAD_DIM) < (HEAD_DIM // 2)
    scale = HEAD_DIM ** -0.5

    def rope(t):
        partner = jnp.where(first_half, pltpu.roll(t, LANES - HEAD_DIM // 2, 1),
                            pltpu.roll(t, HEAD_DIM // 2, 1))
        return t * cos + partner * sin

    a = ATTN_WIDTH
    groups = a // LANES
    for c in range(groups):
        qkv[c] = rope(proj[:, c * LANES:(c + 1) * LANES]) * scale
        qkv[groups + c] = rope(proj[:, a + c * LANES:a + (c + 1) * LANES])
        qkv[2 * groups + c] = proj[:, 2 * a + c * LANES:2 * a + (c + 1) * LANES]
    u_ref[...] = proj[:, 3 * a:3 * a + POOL_WIDTH]
    qm_ref[...] = (proj[:, 3 * a + POOL_WIDTH:] * scale).astype(BF16)

    for n, (r1, r4, r16) in enumerate(((q1, q4, q16), (k1, k4, k16), (v1, v4, v16))):
        for c in range(groups):
            g = n * groups + c
            cols = slice(c * LANES, (c + 1) * LANES)
            r1[0, 0, :, cols] = qkv[g].astype(BF16)
            for a in range(4):
                rows = qkv[g, pl.ds(a, tm // 4, stride=4), :]
                r4[0, a, :, cols] = rows.astype(BF16)
                cls4[g * 4 + a] = rows
            for a in range(4):
                for b in range(4):
                    r16[0, a + 4 * b, :, cols] = cls4[g * 4 + a, pl.ds(b, tm // 16, stride=4), :].astype(BF16)


def _in_proj(x2, g, w_bf, cos, sin, batch, seq):
    t, d = x2.shape
    tm = 512
    tiles_per_seq = seq // tm
    cols = w_bf.shape[1]
    row = lambda i: (i, 0)
    fixed = lambda i: (0, 0)
    cls = lambda i: (i // tiles_per_seq, 0, i % tiles_per_seq, 0)
    cls_specs, cls_shapes = [], []
    for dil in DILATIONS:
        for _ in range(3):
            cls_specs.append(pl.BlockSpec((1, dil, tm // dil, ATTN_WIDTH), cls))
            cls_shapes.append(jax.ShapeDtypeStruct((batch, dil, seq // dil, ATTN_WIDTH), BF16))
    outs = pl.pallas_call(
        functools.partial(_in_proj_kernel, tm=tm),
        grid=(t // tm,),
        in_specs=[pl.BlockSpec((tm, d), row),
                  pl.BlockSpec((1, d), fixed),
                  pl.BlockSpec((d, cols), fixed),
                  pl.BlockSpec((tm, LANES), lambda i: (i % tiles_per_seq, 0)),
                  pl.BlockSpec((tm, LANES), lambda i: (i % tiles_per_seq, 0))],
        out_specs=cls_specs + [pl.BlockSpec((tm, POOL_WIDTH), row), pl.BlockSpec((tm, MEM_WIDTH), row)],
        out_shape=cls_shapes + [jax.ShapeDtypeStruct((t, POOL_WIDTH), F32),
                                jax.ShapeDtypeStruct((t, MEM_WIDTH), BF16)],
        scratch_shapes=[pltpu.VMEM((3 * ATTN_WIDTH // LANES, tm, LANES), F32),
                        pltpu.VMEM((4 * 3 * ATTN_WIDTH // LANES, tm // 4, LANES), F32)],
        compiler_params=_params(("parallel",)),
        name="in_proj",
    )(x2, g, w_bf, cos, sin)
    qkv = [outs[3 * n:3 * n + 3] for n in range(len(DILATIONS))]
    return qkv, outs[-2], outs[-1]


def _head_pair_attention(q, k, v, valid):
    lane = lax.broadcasted_iota(jnp.int32, q.shape, 1)
    outs, lses = [], []
    for half in range(2):
        mine = (lane // HEAD_DIM) == half
        s = lax.dot_general(jnp.where(mine, q, jnp.zeros_like(q)), k, _NT,
                            preferred_element_type=F32)
        if valid is not None:
            s = jnp.where(valid, s, NEG_INF)
        m = jnp.max(s, axis=-1, keepdims=True)
        p = jnp.exp(s - m)
        den = jnp.sum(p, axis=-1, keepdims=True)
        pv = jnp.dot(p.astype(BF16), v, preferred_element_type=F32)
        outs.append(pv / den)
        lses.append(m + jnp.log(den))
    return jnp.where((lane // HEAD_DIM) == 0, outs[0], outs[1]), lses


def _attn_kernel(q_ref, kp_ref, kc_ref, kn_ref, vp_ref, vc_ref, vn_ref, o_ref, lse_ref, *, tq, sub, length):
    j = pl.program_id(2)
    nk = sub + 2 * BAND
    row = lax.broadcasted_iota(jnp.int32, (sub, nk), 0)
    col = lax.broadcasted_iota(jnp.int32, (sub, nk), 1)
    in_band = jnp.abs(col - BAND - row) <= BAND
    lane = lax.broadcasted_iota(jnp.int32, (sub, LANES), 1)
    for c in range(ATTN_WIDTH // LANES):
        sl = slice(c * LANES, (c + 1) * LANES)
        k = jnp.concatenate([kp_ref[0, 0, :, sl], kc_ref[0, 0, :, sl], kn_ref[0, 0, :, sl]], axis=0)
        v = jnp.concatenate([vp_ref[0, 0, :, sl], vc_ref[0, 0, :, sl], vn_ref[0, 0, :, sl]], axis=0)
        for s in range(tq // sub):
            rows = slice(s * sub, (s + 1) * sub)
            key = j * tq + s * sub - BAND + col
            valid = in_band & (key >= 0) & (key < length)
            o, lses = _head_pair_attention(q_ref[0, 0, rows, sl], k[s * sub:s * sub + nk], v[s * sub:s * sub + nk],
                                           valid)
            o_ref[0, 0, rows, sl] = o
            lse_tile = jnp.where(lane == 2 * c, lses[0], lses[1])
            if c == 0:
                lse_ref[0, 0, rows, :] = lse_tile
            else:
                lse_ref[0, 0, rows, :] = jnp.where((lane // 2) == c, lse_tile, lse_ref[0, 0, rows, :])


def _dilated_attention(q, k, v):
    batch, dilation, length, _ = q.shape
    tq = 512
    per = tq // BAND
    n_band_blocks = length // BAND
    cur = lambda b, r, j: (b, r, j, 0)
    prev = lambda b, r, j: (b, r, jnp.maximum(j * per - 1, 0), 0)
    nxt = lambda b, r, j: (b, r, jnp.minimum((j + 1) * per, n_band_blocks - 1), 0)
    big = lambda imap: pl.BlockSpec((1, 1, tq, ATTN_WIDTH), imap)
    halo = lambda imap: pl.BlockSpec((1, 1, BAND, ATTN_WIDTH), imap)
    return pl.pallas_call(
        functools.partial(_attn_kernel, tq=tq, sub=128, length=length),
        grid=(batch, dilation, length // tq),
        in_specs=[big(cur), halo(prev), big(cur), halo(nxt), halo(prev), big(cur), halo(nxt)],
        out_specs=[pl.BlockSpec((1, 1, tq, ATTN_WIDTH), cur), pl.BlockSpec((1, 1, tq, LANES), cur)],
        out_shape=[jax.ShapeDtypeStruct((batch, dilation, length, ATTN_WIDTH), F32),
                   jax.ShapeDtypeStruct((batch, dilation, length, LANES), F32)],
        compiler_params=_params(("parallel", "parallel", "parallel")),
        name=f"dilated_attn_d{dilation}",
    )(q, k, k, k, v, v, v)


def _mem_kv_kernel(m_ref, g_ref, w_ref, kv_ref):
    h = _rms(m_ref[...], g_ref[...]).astype(BF16)
    kv_ref[...] = jnp.dot(h, w_ref[...], preferred_element_type=F32).astype(BF16)


def _mem_kv(mem2, g, w_bf):
    n, d = mem2.shape
    cols = w_bf.shape[1]
    tm = 256
    return pl.pallas_call(
        _mem_kv_kernel,
        grid=(n // tm,),
        in_specs=[pl.BlockSpec((tm, d), lambda i: (i, 0)),
                  pl.BlockSpec((1, d), lambda i: (0, 0)),
                  pl.BlockSpec((d, cols), lambda i: (0, 0))],
        out_specs=pl.BlockSpec((tm, cols), lambda i: (i, 0)),
        out_shape=jax.ShapeDtypeStruct((n, cols), BF16),
        compiler_params=_params(("parallel",)),
        name="mem_kv",
    )(mem2, g, w_bf)


def _mix_out_kernel(o1_ref, o4_ref, o16_ref, l1_ref, l4_ref, l16_ref, up_ref, u_ref, un_ref, qm_ref, kv_ref,
                    x_ref, pw_ref, ps_ref, gg_ref, wo_ref, n2_ref, rwh_ref, rwl_ref, rb_ref,
                    x1_ref, h2_ref, idx_ref, gate_ref, rank_ref, cnt_ref, base_ref,
                    o4_s, o16_s, l4_s, l16_s, cls4, carry, *, tm, seq):
    i = pl.program_id(0)
    tiles_per_seq = seq // tm
    pos0 = (i % tiles_per_seq) * tm

    for src4, src16, dst4, dst16 in ((o4_ref, o16_ref, o4_s, o16_s), (l4_ref, l16_ref, l4_s, l16_s)):
        for c in range(dst4.shape[0]):
            cols = slice(c * LANES, (c + 1) * LANES)
            for a in range(4):
                dst4[c, pl.ds(a, tm // 4, stride=4), :] = src4[0, a, :, cols]
                for b in range(4):
                    cls4[a, pl.ds(b, tm // 16, stride=4), :] = src16[0, a + 4 * b, :, cols]
            for a in range(4):
                dst16[c, pl.ds(a, tm // 4, stride=4), :] = cls4[a]

    l1, l2, l3 = l1_ref[0, 0], l4_s[0], l16_s[0]
    lm = jnp.maximum(jnp.maximum(l1, l2), l3)
    e1, e2, e3 = jnp.exp(l1 - lm), jnp.exp(l2 - lm), jnp.exp(l3 - lm)
    es = e1 + e2 + e3
    w1, w2, w3 = e1 / es, e2 / es, e3 / es
    ya = []
    for c in range(ATTN_WIDTH // LANES):
        sl = slice(c * LANES, (c + 1) * LANES)
        lane = lax.broadcasted_iota(jnp.int32, (tm, LANES), 1)
        lo = lane < HEAD_DIM

        def per_head(w):
            return jnp.where(lo, w[:, 2 * c:2 * c + 1], w[:, 2 * c + 1:2 * c + 2])

        ya.append(per_head(w1) * o1_ref[0, 0, :, sl] + per_head(w2) * o4_s[c] + per_head(w3) * o16_s[c])
    ya = jnp.concatenate(ya, axis=1)

    u = u_ref[...]
    before = jnp.where(pos0 > 0, up_ref[...], 0.0)
    after = jnp.where(pos0 + tm < seq, un_ref[...], 0.0)
    ext = jnp.concatenate([before, u, after], axis=0)
    n_ext = tm + 2 * POOL_HALO
    sums = []
    acc = ext
    shift = 1
    for w in POOL_WINDOWS:
        if w == 2:
            acc = pltpu.roll(ext, 1, 0) + ext
        else:
            acc = pltpu.roll(acc, shift, 0) + pltpu.roll(acc, n_ext - shift, 0)
            shift *= 2
        sums.append(acc[POOL_HALO:POOL_HALO + tm])
    pos = pos0 + lax.broadcasted_iota(jnp.int32, (tm, 1), 0)
    grp = lax.broadcasted_iota(jnp.int32, (tm, POOL_WIDTH), 1) // POOL_GROUP
    mean = jnp.zeros((tm, POOL_WIDTH), F32)
    for g, w in enumerate(POOL_WINDOWS):
        cnt = jnp.minimum(pos + (w - 1 - w // 2), seq - 1) + 1 - jnp.maximum(pos - w // 2, 0)
        mean = jnp.where(grp == g, sums[g] / cnt.astype(F32), mean)
    d = (mean - u).astype(BF16)
    yp = jnp.dot(d, pw_ref[...], preferred_element_type=F32) * ps_ref[...]

    ym = []
    for c in range(MEM_WIDTH // LANES):
        sl = slice(c * LANES, (c + 1) * LANES)
        o, _ = _head_pair_attention(qm_ref[:, sl], kv_ref[0, :, sl],
                                    kv_ref[0, :, MEM_WIDTH + c * LANES:MEM_WIDTH + (c + 1) * LANES], None)
        ym.append(o)
    ym = jnp.concatenate(ym, axis=1)

    gg = gg_ref[...]
    a, p = ATTN_WIDTH, POOL_WIDTH
    y = jnp.concatenate([_rms(ya, gg[:, :a]), _rms(yp, gg[:, a:a + p]), _rms(ym, gg[:, a + p:])], axis=1)
    x1 = x_ref[...] + jnp.dot(y.astype(BF16), wo_ref[...], preferred_element_type=F32)
    x1_ref[...] = x1

    h2 = _rms(x1, n2_ref[...])
    _to_row_tiles(h2_ref, h2)

    hi = h2.astype(BF16)
    lo = (h2 - hi.astype(F32)).astype(BF16)
    logits = (lax.dot_general(rwh_ref[...], hi, _NT, preferred_element_type=F32)
              + lax.dot_general(rwh_ref[...], lo, _NT, preferred_element_type=F32)
              + lax.dot_general(rwl_ref[...], hi, _NT, preferred_element_type=F32)) + rb_ref[...]
    eidx = lax.broadcasted_iota(jnp.int32, logits.shape, 0)
    krow = lax.broadcasted_iota(jnp.int32, (TOP_K, tm), 0)
    vals = jnp.zeros((TOP_K, tm), F32)
    idxs = jnp.zeros((TOP_K, tm), jnp.int32)
    work = logits
    args = []
    for kk in range(TOP_K):
        best = jnp.max(work, axis=0, keepdims=True)
        arg = jnp.min(jnp.where(work == best, eidx, N_EXPERTS), axis=0, keepdims=True)
        vals = jnp.where(krow == kk, best, vals)
        idxs = jnp.where(krow == kk, arg, idxs)
        work = jnp.where(eidx == arg, -jnp.inf, work)
        args.append(arg)
    ex = jnp.exp(vals - vals[0:1])
    gate_ref[...] = ex / jnp.sum(ex, axis=0, keepdims=True)
    idx_ref[...] = idxs

    @pl.when(i == 0)
    def _():
        carry[...] = jnp.zeros_like(carry)

    chosen = (work == -jnp.inf).astype(BF16)
    earlier = (lax.broadcasted_iota(jnp.int32, (tm, tm), 0)
               < lax.broadcasted_iota(jnp.int32, (tm, tm), 1)).astype(BF16)
    before_me = jnp.dot(chosen, earlier, preferred_element_type=F32) + carry[...]
    ranks = jnp.zeros((TOP_K, tm), F32)
    for kk in range(TOP_K):
        ranks = jnp.where(krow == kk, jnp.sum(jnp.where(eidx == args[kk], before_me, 0.0), axis=0, keepdims=True),
                          ranks)
    rank_ref[...] = ranks.astype(jnp.int32)
    for part in range(tm // COMBINE_TILE):
        base_ref[part] = before_me[:, part * COMBINE_TILE:part * COMBINE_TILE + 1].astype(jnp.int32)
    carry[...] = carry[...] + jnp.sum(chosen.astype(F32), axis=1, keepdims=True)
    cnt_ref[...] = carry[...].astype(jnp.int32)


def _mix_out(attn, u, qm, kv, x2, pw_bd, ps, gg, wo_bf, n2, rw_hi, rw_lo, rb, batch, seq):
    t, d = x2.shape
    tm = ROUTER_TILE
    hp = tm // POOL_HALO
    n_halo = t // POOL_HALO
    tiles_per_seq = seq // tm
    row = lambda i: (i, 0)
    fixed = lambda i: (0, 0)
    cls = lambda i: (i // tiles_per_seq, 0, i % tiles_per_seq, 0)
    rowspec = lambda w: pl.BlockSpec((tm, w), row)
    clsspec = lambda dil, w: pl.BlockSpec((1, dil, tm // dil, w), cls)
    tok_cols = lambda i: (0, i)
    (o1, l1), (o4, l4), (o16, l16) = attn
    return pl.pallas_call(
        functools.partial(_mix_out_kernel, tm=tm, seq=seq),
        grid=(t // tm,),
        in_specs=[clsspec(1, ATTN_WIDTH), clsspec(4, ATTN_WIDTH), clsspec(16, ATTN_WIDTH),
                  clsspec(1, LANES), clsspec(4, LANES), clsspec(16, LANES),
                  pl.BlockSpec((POOL_HALO, POOL_WIDTH), lambda i: (jnp.maximum(i * hp - 1, 0), 0)),
                  rowspec(POOL_WIDTH),
                  pl.BlockSpec((POOL_HALO, POOL_WIDTH), lambda i: (jnp.minimum((i + 1) * hp, n_halo - 1), 0)),
                  rowspec(MEM_WIDTH),
                  pl.BlockSpec((1, kv.shape[1], kv.shape[2]), lambda i: (i // tiles_per_seq, 0, 0)),
                  rowspec(d),
                  pl.BlockSpec(pw_bd.shape, fixed),
                  pl.BlockSpec(ps.shape, fixed),
                  pl.BlockSpec(gg.shape, fixed),
                  pl.BlockSpec(wo_bf.shape, fixed),
                  pl.BlockSpec(n2.shape, fixed),
                  pl.BlockSpec(rw_hi.shape, fixed),
                  pl.BlockSpec(rw_lo.shape, fixed),
                  pl.BlockSpec(rb.shape, fixed)],
        out_specs=[rowspec(d),
                   pl.BlockSpec((tm * ROW_TILE, LANES), row),
                   pl.BlockSpec((TOP_K, tm), tok_cols),
                   pl.BlockSpec((TOP_K, tm), tok_cols),
                   pl.BlockSpec((TOP_K, tm), tok_cols),
                   pl.BlockSpec((N_EXPERTS, 1), fixed),
                   pl.BlockSpec((tm // COMBINE_TILE, N_EXPERTS, 1), lambda i: (i, 0, 0))],
        out_shape=[jax.ShapeDtypeStruct((t, d), F32),
                   jax.ShapeDtypeStruct((t * ROW_TILE, LANES), F32),
                   jax.ShapeDtypeStruct((TOP_K, t), jnp.int32),
                   jax.ShapeDtypeStruct((TOP_K, t), F32),
                   jax.ShapeDtypeStruct((TOP_K, t), jnp.int32),
                   jax.ShapeDtypeStruct((N_EXPERTS, 1), jnp.int32),
                   jax.ShapeDtypeStruct((t // COMBINE_TILE, N_EXPERTS, 1), jnp.int32)],
        scratch_shapes=[pltpu.VMEM((ATTN_WIDTH // LANES, tm, LANES), F32),
                        pltpu.VMEM((ATTN_WIDTH // LANES, tm, LANES), F32),
                        pltpu.VMEM((1, tm, LANES), F32), pltpu.VMEM((1, tm, LANES), F32),
                        pltpu.VMEM((4, tm // 4, LANES), F32),
                        pltpu.VMEM((N_EXPERTS, 1), F32)],
        compiler_params=_params(("arbitrary",)),
        name="mix_out_router",
    )(o1, o4, o16, l1, l4, l16, u, u, u, qm, kv, x2, pw_bd, ps, gg, wo_bf, n2, rw_hi, rw_lo, rb)


def _dispatch_kernel(dest_ref, pend_ref, padded_ref, nb_ref, h_ref, xs_hbm, zeros, sem, *, tm, n_tok, n_blocks):
    i = pl.program_id(0)
    blk = MOE_BLOCK * ROW_TILE

    @pl.when(i == 0)
    def _():
        zeros[...] = jnp.zeros_like(zeros)

        def clear(block):
            return pltpu.make_async_copy(zeros, xs_hbm.at[pl.ds(pl.multiple_of(block * blk, blk), blk), :], sem)

        def for_each_cleared_block(fn):
            for e in range(N_EXPERTS):
                @pl.when(padded_ref[e] > 0)
                def _():
                    fn(clear(pend_ref[e] // MOE_BLOCK - 1))

                @pl.when(nb_ref[0] + e < n_blocks)
                def _():
                    fn(clear(nb_ref[0] + e))

        for_each_cleared_block(lambda c: c.start())
        for_each_cleared_block(lambda c: c.wait())

    def start(r, c):
        src = h_ref.at[pl.ds(pl.multiple_of(r * ROW_TILE, ROW_TILE), ROW_TILE), :]
        for kk in range(TOP_K):
            slot = dest_ref[kk * n_tok + i * tm + r]
            pltpu.async_copy(src, xs_hbm.at[pl.ds(pl.multiple_of(slot * ROW_TILE, ROW_TILE), ROW_TILE), :],
                             sem, priority=kk % 2)
        return c

    lax.fori_loop(0, tm, start, 0, unroll=4)
    n = tm * TOP_K * ROW_TILE
    pltpu.make_async_copy(xs_hbm.at[pl.ds(0, n), :], xs_hbm.at[pl.ds(0, n), :], sem).wait()


def _dispatch(dest, pad_end, padded, n_used, h2_tiles, n_slots):
    n_tok = h2_tiles.shape[0] // ROW_TILE
    tm = 512
    return pl.pallas_call(
        functools.partial(_dispatch_kernel, tm=tm, n_tok=n_tok, n_blocks=n_slots // MOE_BLOCK),
        grid_spec=pltpu.PrefetchScalarGridSpec(
            num_scalar_prefetch=4,
            grid=(n_tok // tm,),
            in_specs=[pl.BlockSpec((tm * ROW_TILE, LANES), lambda i, *_: (i, 0))],
            out_specs=pl.BlockSpec(memory_space=pl.ANY),
            scratch_shapes=[pltpu.VMEM((MOE_BLOCK * ROW_TILE, LANES), F32), pltpu.SemaphoreType.DMA(())]),
        out_shape=jax.ShapeDtypeStruct((n_slots * ROW_TILE, LANES), F32),
        compiler_params=_params(("arbitrary",)),
        name="moe_dispatch",
    )(dest, pad_end, padded, n_used, h2_tiles)


EXPERT_GROUP = 2


WEIGHT_DMA_QUEUE = 1


def _expert_kernel(first_ref, nblk_ref, nb_ref, xs_hbm, wgu_hbm, bgu_ref, wdn_hbm, bdn_ref, y_hbm,
                   wgu_f32, wdn_f32, wgu_bf, wdn_bf, xbuf, ybuf, xtail, ytail, zeros, sem_w, sem_in, sem_out,
                   *, layer, n_blocks):
    e = pl.program_id(0)
    first = first_ref[e]
    n = nblk_ref[e]
    rows = MOE_BLOCK * ROW_TILE
    n_groups = n // EXPERT_GROUP
    tail = first + n_groups * EXPERT_GROUP

    def load(block, count, dst, sem):
        return pltpu.make_async_copy(xs_hbm.at[pl.ds(pl.multiple_of(block * rows, rows), count * rows), :], dst, sem)

    def store(block, count, src, sem):
        return pltpu.make_async_copy(src, y_hbm.at[pl.ds(pl.multiple_of(block * MOE_BLOCK, MOE_BLOCK),
                                                         count * MOE_BLOCK), :], sem)

    def group_load(j, s):
        return load(first + j * EXPERT_GROUP, EXPERT_GROUP, xbuf.at[s], sem_in.at[s])

    def group_store(j, s):
        return store(first + j * EXPERT_GROUP, EXPERT_GROUP, ybuf.at[s], sem_out.at[s])

    def ffn(x_tiles, m):
        x = _from_row_tiles(x_tiles, 0, m).astype(BF16)
        gu = jnp.dot(x, wgu_bf[...], preferred_element_type=F32) + bgu_ref[0, 0]
        gate = jnp.minimum(gu[:, :D_EXPERT], SWIGLU_LIMIT)
        up = jnp.clip(gu[:, D_EXPERT:], -SWIGLU_LIMIT, SWIGLU_LIMIT)
        act = (up + 1.0) * gate * jax.nn.sigmoid(SWIGLU_ALPHA * gate)
        return (jnp.dot(act.astype(BF16), wdn_bf[...], preferred_element_type=F32) + bdn_ref[0, 0]).astype(BF16)

    def weights(ex, fn):
        s = ex % 2
        fn(wgu_hbm.at[layer, ex], wgu_f32.at[s], sem_w.at[0, s])
        fn(wdn_hbm.at[layer, ex], wdn_f32.at[s], sem_w.at[1, s])

    def w_start(src, dst, sem):
        pltpu.async_copy(src, dst, sem, priority=WEIGHT_DMA_QUEUE)

    def w_wait(src, dst, sem):
        pltpu.make_async_copy(src, dst, sem).wait()

    @pl.when(e == 0)
    def _():
        weights(0, w_start)

    has_tail = n_groups * EXPERT_GROUP < n

    @pl.when(n_groups > 0)
    def _():
        group_load(0, 0).start()

    @pl.when(has_tail)
    def _():
        load(tail, 1, xtail, sem_in.at[2]).start()

    @pl.when(e + 1 < N_EXPERTS)
    def _():
        weights(e + 1, w_start)

    weights(e, w_wait)

    @pl.when(n > 0)
    def _():
        wgu_bf[...] = wgu_f32[e % 2].astype(BF16)
        wdn_bf[...] = wdn_f32[e % 2].astype(BF16)

        def group(j, carry):
            s = j % 2
            group_load(j, s).wait()

            @pl.when(j + 1 < n_groups)
            def _():
                group_load(j + 1, 1 - s).start()

            y = ffn(xbuf.at[s], EXPERT_GROUP * MOE_BLOCK)

            @pl.when(j >= 2)
            def _():
                group_store(j - 2, s).wait()

            ybuf[s] = y
            group_store(j, s).start()
            return carry

        lax.fori_loop(0, n_groups, group, 0)

        @pl.when(has_tail)
        def _():
            load(tail, 1, xtail, sem_in.at[2]).wait()
            ytail[...] = ffn(xtail, MOE_BLOCK)
            store(tail, 1, ytail, sem_out.at[2]).start()

        @pl.when(n_groups >= 2)
        def _():
            group_store(n_groups - 2, n_groups % 2).wait()

        @pl.when(n_groups >= 1)
        def _():
            group_store(n_groups - 1, (n_groups - 1) % 2).wait()

        @pl.when(has_tail)
        def _():
            store(tail, 1, ytail, sem_out.at[2]).wait()

    @pl.when(e == N_EXPERTS - 1)
    def _():
        zeros[...] = jnp.zeros_like(zeros)

        def clear(k):
            blk = pl.multiple_of((nb_ref[0] + k) * MOE_BLOCK, MOE_BLOCK)
            return pltpu.make_async_copy(zeros, y_hbm.at[pl.ds(blk, MOE_BLOCK), :], sem_out.at[0])

        for k in range(N_EXPERTS):
            @pl.when(nb_ref[0] + k < n_blocks)
            def _():
                clear(k).start()
        for k in range(N_EXPERTS):
            @pl.when(nb_ref[0] + k < n_blocks)
            def _():
                clear(k).wait()


def _experts(layer, first_block, n_block, n_used, xs_tiles, w_gu, b_gu, w_down, b_down):
    n_slots = xs_tiles.shape[0] // ROW_TILE
    d = w_gu.shape[2]
    de2 = w_gu.shape[3]
    by_expert = lambda e, *_: (layer, e, 0, 0)
    return pl.pallas_call(
        functools.partial(_expert_kernel, layer=layer, n_blocks=n_slots // MOE_BLOCK),
        grid_spec=pltpu.PrefetchScalarGridSpec(
            num_scalar_prefetch=3,
            grid=(N_EXPERTS,),
            in_specs=[pl.BlockSpec(memory_space=pl.ANY),
                      pl.BlockSpec(memory_space=pl.ANY),
                      pl.BlockSpec((1, 1, 1, de2), by_expert),
                      pl.BlockSpec(memory_space=pl.ANY),
                      pl.BlockSpec((1, 1, 1, d), by_expert)],
            out_specs=pl.BlockSpec(memory_space=pl.ANY),
            scratch_shapes=[pltpu.VMEM((2, d, de2), F32), pltpu.VMEM((2, D_EXPERT, d), F32),
                            pltpu.VMEM((d, de2), BF16), pltpu.VMEM((D_EXPERT, d), BF16),
                            pltpu.VMEM((2, EXPERT_GROUP * MOE_BLOCK * ROW_TILE, LANES), F32),
                            pltpu.VMEM((2, EXPERT_GROUP * MOE_BLOCK, d), BF16),
                            pltpu.VMEM((MOE_BLOCK * ROW_TILE, LANES), F32),
                            pltpu.VMEM((MOE_BLOCK, d), BF16),
                            pltpu.VMEM((MOE_BLOCK, d), BF16),
                            pltpu.SemaphoreType.DMA((2, 2)),
                            pltpu.SemaphoreType.DMA((3,)), pltpu.SemaphoreType.DMA((3,))]),
        out_shape=jax.ShapeDtypeStruct((n_slots, d), BF16),
        compiler_params=_params(("arbitrary",)),
        name="moe_experts",
    )(first_block, n_block, n_used, xs_tiles, w_gu, b_gu.reshape(b_gu.shape[0], N_EXPERTS, 1, de2), w_down,
      b_down.reshape(b_down.shape[0], N_EXPERTS, 1, d))


CHUNK = 16
COMBINE_ROWS = COMBINE_TILE * TOP_K + 2 * N_EXPERTS * CHUNK
COMBINE_CHUNKS = COMBINE_ROWS // CHUNK


def _combine_kernel(src_ref, y_hbm, x1_ref, col_ref, gate_ref, g_ref, out_ref, ybuf, sems,
                    *, tm, n_tiles, final):
    i = pl.program_id(0)
    slot = i % 2

    def fetch(tile, buf):
        for c in range(COMBINE_CHUNKS):
            src = src_ref[tile * COMBINE_CHUNKS + c]
            pltpu.make_async_copy(y_hbm.at[pl.ds(pl.multiple_of(src, CHUNK), CHUNK), :],
                                  ybuf.at[buf, pl.ds(c * CHUNK, CHUNK), :], sems.at[buf]).start()

    def fetched(buf):
        return pltpu.make_async_copy(y_hbm.at[pl.ds(0, COMBINE_ROWS), :], ybuf.at[buf], sems.at[buf])

    @pl.when(i == 0)
    def _():
        fetch(0, 0)

    fetch(jnp.minimum(i + 1, n_tiles - 1), 1 - slot)

    col = col_ref[...]
    gates = gate_ref[...]
    col_id = lax.broadcasted_iota(jnp.int32, (tm, COMBINE_ROWS), 1)
    g = jnp.zeros((tm, COMBINE_ROWS), F32)
    for kk in range(TOP_K):
        g = jnp.where(col_id == col[:, kk:kk + 1], gates[:, kk:kk + 1], g)

    fetched(slot).wait()
    x = x1_ref[...] + jnp.dot(g.astype(BF16), ybuf[slot], preferred_element_type=F32)
    out_ref[...] = _rms(x, g_ref[...]) if final else x

    @pl.when(i == n_tiles - 1)
    def _():
        fetched(1 - slot).wait()


def _combine(chunk_src, y_slots, x1, col_tk, gates_tk, final_g, final):
    t, d = x1.shape
    tm = COMBINE_TILE
    n_tiles = t // tm
    tok = lambda i, *_: (i, 0)
    return pl.pallas_call(
        functools.partial(_combine_kernel, tm=tm, n_tiles=n_tiles, final=final),
        grid_spec=pltpu.PrefetchScalarGridSpec(
            num_scalar_prefetch=1,
            grid=(n_tiles,),
            in_specs=[pl.BlockSpec(memory_space=pl.ANY),
                      pl.BlockSpec((tm, d), tok),
                      pl.BlockSpec((tm, TOP_K), tok),
                      pl.BlockSpec((tm, TOP_K), tok),
                      pl.BlockSpec((1, d), lambda i, *_: (0, 0))],
            out_specs=pl.BlockSpec((tm, d), tok),
            scratch_shapes=[pltpu.VMEM((2, COMBINE_ROWS, d), BF16), pltpu.SemaphoreType.DMA((2,))]),
        out_shape=jax.ShapeDtypeStruct((t, d), F32),
        compiler_params=_params(("arbitrary",)),
        name="moe_combine",
    )(chunk_src, y_slots, x1, col_tk, gates_tk, final_g)


def _slot_layout(top_idx, rank, counts, tile_base):
    counts = counts.reshape(-1)
    padded = (counts + MOE_BLOCK - 1) // MOE_BLOCK * MOE_BLOCK
    pad_end = jnp.cumsum(padded).astype(jnp.int32)
    pad_start = pad_end - padded
    experts = jnp.arange(N_EXPERTS, dtype=jnp.int32)
    start_of = jnp.sum(jnp.where(top_idx[..., None] == experts, pad_start, 0), axis=-1)
    dest = (start_of + rank).astype(jnp.int32)
    n_used = pad_end[-1] // MOE_BLOCK
    base = tile_base.reshape(-1, N_EXPERTS)
    run_start = pad_start[None, :] + base
    run_len = jnp.concatenate([base[1:], counts[None, :]], axis=0) - base
    seg_a = run_start // CHUNK * CHUNK
    seg_nch = jnp.where(run_len > 0, (run_start + run_len - seg_a + CHUNK - 1) // CHUNK, 0)
    buf_row0 = (jnp.cumsum(seg_nch, axis=1) - seg_nch) * CHUNK
    shift = jnp.repeat(buf_row0 - seg_a, COMBINE_TILE, axis=0)
    col = dest + jnp.sum(jnp.where(top_idx[..., None] == experts, shift[None], 0), axis=-1)
    chunk_end = jnp.cumsum(seg_nch, axis=1)
    c = jnp.arange(COMBINE_CHUNKS, dtype=jnp.int32)
    owner = c[None, :, None] >= chunk_end[:, None, :]
    owner_e = jnp.minimum(jnp.sum(owner.astype(jnp.int32), axis=-1), N_EXPERTS - 1)
    pick = owner_e[..., None] == experts
    first_chunk = jnp.sum(jnp.where(pick, (chunk_end - seg_nch)[:, None, :], 0), axis=-1)
    run_a = jnp.sum(jnp.where(pick, seg_a[:, None, :], 0), axis=-1)
    total = chunk_end[:, -1]
    chunk_src = jnp.where(c[None, :] < total[:, None], run_a + (c[None, :] - first_chunk) * CHUNK, 0)
    flat = lambda a: a.reshape(-1).astype(jnp.int32)
    return (dest, pad_end, padded.astype(jnp.int32), n_used.reshape(1).astype(jnp.int32),
            flat(chunk_src), col.astype(jnp.int32))


def _rope_tables(seq):
    half = HEAD_DIM // 2
    inv_freq = ROPE_THETA ** (-jnp.arange(half, dtype=F32) / half)
    ang = jnp.arange(seq, dtype=F32)[:, None] * inv_freq[None, :]
    cos, sin = jnp.cos(ang), jnp.sin(ang)
    reps = LANES // HEAD_DIM
    cos_l = jnp.tile(jnp.concatenate([cos, cos], axis=1), (1, reps))
    sin_l = jnp.tile(jnp.concatenate([-sin, sin], axis=1), (1, reps))
    return cos_l, sin_l


def kernel(x, mem, norm1_g, w_in, pool_w, pool_scale, mem_norm_g, w_mem_kv, grp_norm_g, w_out, norm2_g,
           router_w, router_b, w_gu, b_gu, w_down, b_down, final_g):
    batch, seq, d = x.shape
    depth = w_in.shape[0]
    n_mem = mem.shape[1]
    t = batch * seq
    n_blocks = t * TOP_K // MOE_BLOCK + N_EXPERTS
    cos_l, sin_l = _rope_tables(seq)
    x2 = x.reshape(t, d)
    mem2 = mem.reshape(batch * n_mem, d)
    row = lambda a: a.reshape(1, -1)
    for l in range(depth):
        qkv, u, qm = _in_proj(x2, row(norm1_g[l]), w_in[l].astype(BF16), cos_l, sin_l, batch, seq)
        attn = [_dilated_attention(*qkv[n]) for n in range(len(DILATIONS))]
        kv = _mem_kv(mem2, row(mem_norm_g[l]), w_mem_kv[l].astype(BF16)).reshape(batch, n_mem, 2 * MEM_WIDTH)
        pw_bd = jax.scipy.linalg.block_diag(*[pool_w[l, g] for g in range(len(POOL_WINDOWS))]).astype(BF16)
        rw_t = router_w[l].T
        rw_hi = rw_t.astype(BF16)
        rw_lo = (rw_t - rw_hi.astype(F32)).astype(BF16)
        x1, h2_tiles, top_idx, gates, rank, counts, tile_base = _mix_out(
            attn, u, qm, kv, x2, pw_bd, row(pool_scale[l]), row(grp_norm_g[l]), w_out[l].astype(BF16),
            row(norm2_g[l]), rw_hi, rw_lo, router_b[l].reshape(-1, 1), batch, seq)
        dest, pad_end, padded, n_used, chunk_src, col = _slot_layout(
            top_idx, rank, counts, tile_base)
        xs_tiles = _dispatch(dest.reshape(-1), pad_end, padded, n_used, h2_tiles, n_blocks * MOE_BLOCK)
        y_slots = _experts(l, (pad_end - padded) // MOE_BLOCK, padded // MOE_BLOCK, n_used, xs_tiles,
                           w_gu, b_gu, w_down, b_down)
        x2 = _combine(chunk_src, y_slots, x1, col.T, gates.T, row(final_g),
                      final=(l == depth - 1))
    return x2.reshape(batch, seq, d)
```

```python
import functools

import jax
import jax.numpy as jnp
from jax import lax
from jax.experimental import pallas as pl
from jax.experimental.pallas import tpu as pltpu

D_MODEL = 1024
HEAD_DIM = 64
ATTN_WIDTH = 512
DILATIONS = (1, 4, 16)
BAND = 64
ROPE_THETA = 10000.0
POOL_WINDOWS = (2, 4, 8, 16)
POOL_WIDTH = 256
POOL_GROUP = 64
POOL_HALO = 8
MEM_WIDTH = 256
N_EXPERTS = 32
TOP_K = 4
D_EXPERT = 1024
SWIGLU_ALPHA = 1.702
SWIGLU_LIMIT = 7.0
MOE_BLOCK = 256
ROUTER_TILE = 512
COMBINE_TILE = 256
NORM_EPS = 1e-5
NEG_INF = -1e30
LANES = 128
SUBLANES = 8
ROW_TILE = D_MODEL // LANES

F32 = jnp.float32
BF16 = jnp.bfloat16
VMEM_LIMIT = 56 * 1024 * 1024

_NT = (((1,), (1,)), ((), ()))


def _params(sem, vmem=VMEM_LIMIT):
    return pltpu.CompilerParams(dimension_semantics=sem, vmem_limit_bytes=vmem)


def _rms(x, g):
    return x * lax.rsqrt(jnp.mean(x * x, axis=-1, keepdims=True) + NORM_EPS) * g


def _to_row_tiles(ref, val):
    m = val.shape[0]
    for s in range(ROW_TILE):
        ref[pl.ds(s, m, stride=ROW_TILE), :] = val[:, s * LANES:(s + 1) * LANES]


def _from_row_tiles(ref, start, m):
    return jnp.concatenate([ref[pl.ds(start * ROW_TILE + s, m, stride=ROW_TILE), :] for s in range(ROW_TILE)],
                           axis=1)


def _in_proj_kernel(x_ref, g_ref, w_ref, cos_ref, sin_ref, q1, k1, v1, q4, k4, v4, q16, k16, v16, u_ref, qm_ref,
                    qkv, cls4, *, tm):
    h = _rms(x_ref[...], g_ref[...]).astype(BF16)
    proj = jnp.dot(h, w_ref[...], preferred_element_type=F32)
    cos = cos_ref[...]
    sin = sin_ref[...]
    lane = lax.broadcasted_iota(jnp.int32, cos.shape, 1)
    first_half = (lane % HEAD_DIM) < (HEAD_DIM // 2)
    scale = HEAD_DIM ** -0.5

    def rope(t):
        partner = jnp.where(first_half, pltpu.roll(t, LANES - HEAD_DIM // 2, 1),
                            pltpu.roll(t, HEAD_DIM // 2, 1))
        return t * cos + partner * sin

    a = ATTN_WIDTH
    groups = a // LANES
    for c in range(groups):
        qkv[c] = rope(proj[:, c * LANES:(c + 1) * LANES]) * scale
        qkv[groups + c] = rope(proj[:, a + c * LANES:a + (c + 1) * LANES])
        qkv[2 * groups + c] = proj[:, 2 * a + c * LANES:2 * a + (c + 1) * LANES]
    u_ref[...] = proj[:, 3 * a:3 * a + POOL_WIDTH]
    qm_ref[...] = (proj[:, 3 * a + POOL_WIDTH:] * scale).astype(BF16)

    for n, (r1, r4, r16) in enumerate(((q1, q4, q16), (k1, k4, k16), (v1, v4, v16))):
        for c in range(groups):
            g = n * groups + c
            cols = slice(c * LANES, (c + 1) * LANES)
            r1[0, 0, :, cols] = qkv[g].astype(BF16)
            for a in range(4):
                rows = qkv[g, pl.ds(a, tm // 4, stride=4), :]
                r4[0, a, :, cols] = rows.astype(BF16)
                cls4[g * 4 + a] = rows
            for a in range(4):
                for b in range(4):
                    r16[0, a + 4 * b, :, cols] = cls4[g * 4 + a, pl.ds(b, tm // 16, stride=4), :].astype(BF16)


def _in_proj(x2, g, w_bf, cos, sin, batch, seq):
    t, d = x2.shape
    tm = 512
    tiles_per_seq = seq // tm
    cols = w_bf.shape[1]
    row = lambda i: (i, 0)
    fixed = lambda i: (0, 0)
    cls = lambda i: (i // tiles_per_seq, 0, i % tiles_per_seq, 0)
    cls_specs, cls_shapes = [], []
    for dil in DILATIONS:
        for _ in range(3):
            cls_specs.append(pl.BlockSpec((1, dil, tm // dil, ATTN_WIDTH), cls))
            cls_shapes.append(jax.ShapeDtypeStruct((batch, dil, seq // dil, ATTN_WIDTH), BF16))
    outs = pl.pallas_call(
        functools.partial(_in_proj_kernel, tm=tm),
        grid=(t // tm,),
        in_specs=[pl.BlockSpec((tm, d), row),
                  pl.BlockSpec((1, d), fixed),
                  pl.BlockSpec((d, cols), fixed),
                  pl.BlockSpec((tm, LANES), lambda i: (i % tiles_per_seq, 0)),
                  pl.BlockSpec((tm, LANES), lambda i: (i % tiles_per_seq, 0))],
        out_specs=cls_specs + [pl.BlockSpec((tm, POOL_WIDTH), row), pl.BlockSpec((tm, MEM_WIDTH), row)],
        out_shape=cls_shapes + [jax.ShapeDtypeStruct((t, POOL_WIDTH), F32),
                                jax.ShapeDtypeStruct((t, MEM_WIDTH), BF16)],
        scratch_shapes=[pltpu.VMEM((3 * ATTN_WIDTH // LANES, tm, LANES), F32),
                        pltpu.VMEM((4 * 3 * ATTN_WIDTH // LANES, tm // 4, LANES), F32)],
        compiler_params=_params(("parallel",)),
        name="in_proj",
    )(x2, g, w_bf, cos, sin)
    qkv = [outs[3 * n:3 * n + 3] for n in range(len(DILATIONS))]
    return qkv, outs[-2], outs[-1]


def _head_pair_attention(q, k, v, valid):
    lane = lax.broadcasted_iota(jnp.int32, q.shape, 1)
    outs, lses = [], []
    for half in range(2):
        mine = (lane // HEAD_DIM) == half
        s = lax.dot_general(jnp.where(mine, q, jnp.zeros_like(q)), k, _NT,
                            preferred_element_type=F32)
        if valid is not None:
            s = jnp.where(valid, s, NEG_INF)
        m = jnp.max(s, axis=-1, keepdims=True)
        p = jnp.exp(s - m)
        den = jnp.sum(p, axis=-1, keepdims=True)
        pv = jnp.dot(p.astype(BF16), v, preferred_element_type=F32)
        outs.append(pv / den)
        lses.append(m + jnp.log(den))
    return jnp.where((lane // HEAD_DIM) == 0, outs[0], outs[1]), lses


def _attn_kernel(q_ref, kp_ref, kc_ref, kn_ref, vp_ref, vc_ref, vn_ref, o_ref, lse_ref, *, tq, sub, length):
    j = pl.program_id(2)
    nk = sub + 2 * BAND
    row = lax.broadcasted_iota(jnp.int32, (sub, nk), 0)
    col = lax.broadcasted_iota(jnp.int32, (sub, nk), 1)
    in_band = jnp.abs(col - BAND - row) <= BAND
    lane = lax.broadcasted_iota(jnp.int32, (sub, LANES), 1)
    for c in range(ATTN_WIDTH // LANES):
        sl = slice(c * LANES, (c + 1) * LANES)
        k = jnp.concatenate([kp_ref[0, 0, :, sl], kc_ref[0, 0, :, sl], kn_ref[0, 0, :, sl]], axis=0)
        v = jnp.concatenate([vp_ref[0, 0, :, sl], vc_ref[0, 0, :, sl], vn_ref[0, 0, :, sl]], axis=0)
        for s in range(tq // sub):
            rows = slice(s * sub, (s + 1) * sub)
            key = j * tq + s * sub - BAND + col
            valid = in_band & (key >= 0) & (key < length)
            o, lses = _head_pair_attention(q_ref[0, 0, rows, sl], k[s * sub:s * sub + nk], v[s * sub:s * sub + nk],
                                           valid)
            o_ref[0, 0, rows, sl] = o
            lse_tile = jnp.where(lane == 2 * c, lses[0], lses[1])
            if c == 0:
                lse_ref[0, 0, rows, :] = lse_tile
            else:
                lse_ref[0, 0, rows, :] = jnp.where((lane // 2) == c, lse_tile, lse_ref[0, 0, rows, :])


def _dilated_attention(q, k, v):
    batch, dilation, length, _ = q.shape
    tq = 512
    per = tq // BAND
    n_band_blocks = length // BAND
    cur = lambda b, r, j: (b, r, j, 0)
    prev = lambda b, r, j: (b, r, jnp.maximum(j * per - 1, 0), 0)
    nxt = lambda b, r, j: (b, r, jnp.minimum((j + 1) * per, n_band_blocks - 1), 0)
    big = lambda imap: pl.BlockSpec((1, 1, tq, ATTN_WIDTH), imap)
    halo = lambda imap: pl.BlockSpec((1, 1, BAND, ATTN_WIDTH), imap)
    return pl.pallas_call(
        functools.partial(_attn_kernel, tq=tq, sub=128, length=length),
        grid=(batch, dilation, length // tq),
        in_specs=[big(cur), halo(prev), big(cur), halo(nxt), halo(prev), big(cur), halo(nxt)],
        out_specs=[pl.BlockSpec((1, 1, tq, ATTN_WIDTH), cur), pl.BlockSpec((1, 1, tq, LANES), cur)],
        out_shape=[jax.ShapeDtypeStruct((batch, dilation, length, ATTN_WIDTH), F32),
                   jax.ShapeDtypeStruct((batch, dilation, length, LANES), F32)],
        compiler_params=_params(("parallel", "parallel", "parallel")),
        name=f"dilated_attn_d{dilation}",
    )(q, k, k, k, v, v, v)


def _mem_kv_kernel(m_ref, g_ref, w_ref, kv_ref):
    h = _rms(m_ref[...], g_ref[...]).astype(BF16)
    kv_ref[...] = jnp.dot(h, w_ref[...], preferred_element_type=F32).astype(BF16)


def _mem_kv(mem2, g, w_bf):
    n, d = mem2.shape
    cols = w_bf.shape[1]
    tm = 256
    return pl.pallas_call(
        _mem_kv_kernel,
        grid=(n // tm,),
        in_specs=[pl.BlockSpec((tm, d), lambda i: (i, 0)),
                  pl.BlockSpec((1, d), lambda i: (0, 0)),
                  pl.BlockSpec((d, cols), lambda i: (0, 0))],
        out_specs=pl.BlockSpec((tm, cols), lambda i: (i, 0)),
        out_shape=jax.ShapeDtypeStruct((n, cols), BF16),
        compiler_params=_params(("parallel",)),
        name="mem_kv",
    )(mem2, g, w_bf)


def _mix_out_kernel(o1_ref, o4_ref, o16_ref, l1_ref, l4_ref, l16_ref, up_ref, u_ref, un_ref, qm_ref, kv_ref,
                    x_ref, pw_ref, ps_ref, gg_ref, wo_ref, n2_ref, rwh_ref, rwl_ref, rb_ref,
                    x1_ref, h2_ref, idx_ref, gate_ref, rank_ref, cnt_ref, base_ref,
                    o4_s, o16_s, l4_s, l16_s, cls4, carry, *, tm, seq):
    i = pl.program_id(0)
    tiles_per_seq = seq // tm
    pos0 = (i % tiles_per_seq) * tm

    for src4, src16, dst4, dst16 in ((o4_ref, o16_ref, o4_s, o16_s), (l4_ref, l16_ref, l4_s, l16_s)):
        for c in range(dst4.shape[0]):
            cols = slice(c * LANES, (c + 1) * LANES)
            for a in range(4):
                dst4[c, pl.ds(a, tm // 4, stride=4), :] = src4[0, a, :, cols]
                for b in range(4):
                    cls4[a, pl.ds(b, tm // 16, stride=4), :] = src16[0, a + 4 * b, :, cols]
            for a in range(4):
                dst16[c, pl.ds(a, tm // 4, stride=4), :] = cls4[a]

    l1, l2, l3 = l1_ref[0, 0], l4_s[0], l16_s[0]
    lm = jnp.maximum(jnp.maximum(l1, l2), l3)
    e1, e2, e3 = jnp.exp(l1 - lm), jnp.exp(l2 - lm), jnp.exp(l3 - lm)
    es = e1 + e2 + e3
    w1, w2, w3 = e1 / es, e2 / es, e3 / es
    ya = []
    for c in range(ATTN_WIDTH // LANES):
        sl = slice(c * LANES, (c + 1) * LANES)
        lane = lax.broadcasted_iota(jnp.int32, (tm, LANES), 1)
        lo = lane < HEAD_DIM

        def per_head(w):
            return jnp.where(lo, w[:, 2 * c:2 * c + 1], w[:, 2 * c + 1:2 * c + 2])

        ya.append(per_head(w1) * o1_ref[0, 0, :, sl] + per_head(w2) * o4_s[c] + per_head(w3) * o16_s[c])
    ya = jnp.concatenate(ya, axis=1)

    u = u_ref[...]
    before = jnp.where(pos0 > 0, up_ref[...], 0.0)
    after = jnp.where(pos0 + tm < seq, un_ref[...], 0.0)
    ext = jnp.concatenate([before, u, after], axis=0)
    n_ext = tm + 2 * POOL_HALO
    sums = []
    acc = ext
    shift = 1
    for w in POOL_WINDOWS:
        if w == 2:
            acc = pltpu.roll(ext, 1, 0) + ext
        else:
            acc = pltpu.roll(acc, shift, 0) + pltpu.roll(acc, n_ext - shift, 0)
            shift *= 2
        sums.append(acc[POOL_HALO:POOL_HALO + tm])
    pos = pos0 + lax.broadcasted_iota(jnp.int32, (tm, 1), 0)
    grp = lax.broadcasted_iota(jnp.int32, (tm, POOL_WIDTH), 1) // POOL_GROUP
    mean = jnp.zeros((tm, POOL_WIDTH), F32)
    for g, w in enumerate(POOL_WINDOWS):
        cnt = jnp.minimum(pos + (w - 1 - w // 2), seq - 1) + 1 - jnp.maximum(pos - w // 2, 0)
        mean = jnp.where(grp == g, sums[g] / cnt.astype(F32), mean)
    d = (mean - u).astype(BF16)
    yp = jnp.dot(d, pw_ref[...], preferred_element_type=F32) * ps_ref[...]

    ym = []
    for c in range(MEM_WIDTH // LANES):
        sl = slice(c * LANES, (c + 1) * LANES)
        o, _ = _head_pair_attention(qm_ref[:, sl], kv_ref[0, :, sl],
                                    kv_ref[0, :, MEM_WIDTH + c * LANES:MEM_WIDTH + (c + 1) * LANES], None)
        ym.append(o)
    ym = jnp.concatenate(ym, axis=1)

    gg = gg_ref[...]
    a, p = ATTN_WIDTH, POOL_WIDTH
    y = jnp.concatenate([_rms(ya, gg[:, :a]), _rms(yp, gg[:, a:a + p]), _rms(ym, gg[:, a + p:])], axis=1)
    x1 = x_ref[...] + jnp.dot(y.astype(BF16), wo_ref[...], preferred_element_type=F32)
    x1_ref[...] = x1

    h2 = _rms(x1, n2_ref[...])
    _to_row_tiles(h2_ref, h2)

    hi = h2.astype(BF16)
    lo = (h2 - hi.astype(F32)).astype(BF16)
    logits = (lax.dot_general(rwh_ref[...], hi, _NT, preferred_element_type=F32)
              + lax.dot_general(rwh_ref[...], lo, _NT, preferred_element_type=F32)
              + lax.dot_general(rwl_ref[...], hi, _NT, preferred_element_type=F32)) + rb_ref[...]
    eidx = lax.broadcasted_iota(jnp.int32, logits.shape, 0)
    krow = lax.broadcasted_iota(jnp.int32, (TOP_K, tm), 0)
    vals = jnp.zeros((TOP_K, tm), F32)
    idxs = jnp.zeros((TOP_K, tm), jnp.int32)
    work = logits
    args = []
    for kk in range(TOP_K):
        best = jnp.max(work, axis=0, keepdims=True)
        arg = jnp.min(jnp.where(work == best, eidx, N_EXPERTS), axis=0, keepdims=True)
        vals = jnp.where(krow == kk, best, vals)
        idxs = jnp.where(krow == kk, arg, idxs)
        work = jnp.where(eidx == arg, -jnp.inf, work)
        args.append(arg)
    ex = jnp.exp(vals - vals[0:1])
    gate_ref[...] = ex / jnp.sum(ex, axis=0, keepdims=True)
    idx_ref[...] = idxs

    @pl.when(i == 0)
    def _():
        carry[...] = jnp.zeros_like(carry)

    chosen = (work == -jnp.inf).astype(BF16)
    earlier = (lax.broadcasted_iota(jnp.int32, (tm, tm), 0)
               < lax.broadcasted_iota(jnp.int32, (tm, tm), 1)).astype(BF16)
    before_me = jnp.dot(chosen, earlier, preferred_element_type=F32) + carry[...]
    ranks = jnp.zeros((TOP_K, tm), F32)
    for kk in range(TOP_K):
        ranks = jnp.where(krow == kk, jnp.sum(jnp.where(eidx == args[kk], before_me, 0.0), axis=0, keepdims=True),
                          ranks)
    rank_ref[...] = ranks.astype(jnp.int32)
    for part in range(tm // COMBINE_TILE):
        base_ref[part] = before_me[:, part * COMBINE_TILE:part * COMBINE_TILE + 1].astype(jnp.int32)
    carry[...] = carry[...] + jnp.sum(chosen.astype(F32), axis=1, keepdims=True)
    cnt_ref[...] = carry[...].astype(jnp.int32)


def _mix_out(attn, u, qm, kv, x2, pw_bd, ps, gg, wo_bf, n2, rw_hi, rw_lo, rb, batch, seq):
    t, d = x2.shape
    tm = ROUTER_TILE
    hp = tm // POOL_HALO
    n_halo = t // POOL_HALO
    tiles_per_seq = seq // tm
    row = lambda i: (i, 0)
    fixed = lambda i: (0, 0)
    cls = lambda i: (i // tiles_per_seq, 0, i % tiles_per_seq, 0)
    rowspec = lambda w: pl.BlockSpec((tm, w), row)
    clsspec = lambda dil, w: pl.BlockSpec((1, dil, tm // dil, w), cls)
    tok_cols = lambda i: (0, i)
    (o1, l1), (o4, l4), (o16, l16) = attn
    return pl.pallas_call(
        functools.partial(_mix_out_kernel, tm=tm, seq=seq),
        grid=(t // tm,),
        in_specs=[clsspec(1, ATTN_WIDTH), clsspec(4, ATTN_WIDTH), clsspec(16, ATTN_WIDTH),
                  clsspec(1, LANES), clsspec(4, LANES), clsspec(16, LANES),
                  pl.BlockSpec((POOL_HALO, POOL_WIDTH), lambda i: (jnp.maximum(i * hp - 1, 0), 0)),
                  rowspec(POOL_WIDTH),
                  pl.BlockSpec((POOL_HALO, POOL_WIDTH), lambda i: (jnp.minimum((i + 1) * hp, n_halo - 1), 0)),
                  rowspec(MEM_WIDTH),
                  pl.BlockSpec((1, kv.shape[1], kv.shape[2]), lambda i: (i // tiles_per_seq, 0, 0)),
                  rowspec(d),
                  pl.BlockSpec(pw_bd.shape, fixed),
                  pl.BlockSpec(ps.shape, fixed),
                  pl.BlockSpec(gg.shape, fixed),
                  pl.BlockSpec(wo_bf.shape, fixed),
                  pl.BlockSpec(n2.shape, fixed),
                  pl.BlockSpec(rw_hi.shape, fixed),
                  pl.BlockSpec(rw_lo.shape, fixed),
                  pl.BlockSpec(rb.shape, fixed)],
        out_specs=[rowspec(d),
                   pl.BlockSpec((tm * ROW_TILE, LANES), row),
                   pl.BlockSpec((TOP_K, tm), tok_cols),
                   pl.BlockSpec((TOP_K, tm), tok_cols),
                   pl.BlockSpec((TOP_K, tm), tok_cols),
                   pl.BlockSpec((N_EXPERTS, 1), fixed),
                   pl.BlockSpec((tm // COMBINE_TILE, N_EXPERTS, 1), lambda i: (i, 0, 0))],
        out_shape=[jax.ShapeDtypeStruct((t, d), F32),
                   jax.ShapeDtypeStruct((t * ROW_TILE, LANES), F32),
                   jax.ShapeDtypeStruct((TOP_K, t), jnp.int32),
                   jax.ShapeDtypeStruct((TOP_K, t), F32),
                   jax.ShapeDtypeStruct((TOP_K, t), jnp.int32),
                   jax.ShapeDtypeStruct((N_EXPERTS, 1), jnp.int32),
                   jax.ShapeDtypeStruct((t // COMBINE_TILE, N_EXPERTS, 1), jnp.int32)],
        scratch_shapes=[pltpu.VMEM((ATTN_WIDTH // LANES, tm, LANES), F32),
                        pltpu.VMEM((ATTN_WIDTH // LANES, tm, LANES), F32),
                        pltpu.VMEM((1, tm, LANES), F32), pltpu.VMEM((1, tm, LANES), F32),
                        pltpu.VMEM((4, tm // 4, LANES), F32),
                        pltpu.VMEM((N_EXPERTS, 1), F32)],
        compiler_params=_params(("arbitrary",)),
        name="mix_out_router",
    )(o1, o4, o16, l1, l4, l16, u, u, u, qm, kv, x2, pw_bd, ps, gg, wo_bf, n2, rw_hi, rw_lo, rb)


def _dispatch_kernel(dest_ref, pend_ref, padded_ref, nb_ref, h_ref, xs_hbm, zeros, sem, *, tm, n_tok, n_blocks):
    i = pl.program_id(0)
    blk = MOE_BLOCK * ROW_TILE

    @pl.when(i == 0)
    def _():
        zeros[...] = jnp.zeros_like(zeros)

        def clear(block):
            return pltpu.make_async_copy(zeros, xs_hbm.at[pl.ds(pl.multiple_of(block * blk, blk), blk), :], sem)

        def for_each_cleared_block(fn):
            for e in range(N_EXPERTS):
                @pl.when(padded_ref[e] > 0)
                def _():
                    fn(clear(pend_ref[e] // MOE_BLOCK - 1))

                @pl.when(nb_ref[0] + e < n_blocks)
                def _():
                    fn(clear(nb_ref[0] + e))

        for_each_cleared_block(lambda c: c.start())
        for_each_cleared_block(lambda c: c.wait())

    def start(r, c):
        src = h_ref.at[pl.ds(pl.multiple_of(r * ROW_TILE, ROW_TILE), ROW_TILE), :]
        for kk in range(TOP_K):
            slot = dest_ref[kk * n_tok + i * tm + r]
            pltpu.async_copy(src, xs_hbm.at[pl.ds(pl.multiple_of(slot * ROW_TILE, ROW_TILE), ROW_TILE), :],
                             sem, priority=kk % 2)
        return c

    lax.fori_loop(0, tm, start, 0, unroll=4)
    n = tm * TOP_K * ROW_TILE
    pltpu.make_async_copy(xs_hbm.at[pl.ds(0, n), :], xs_hbm.at[pl.ds(0, n), :], sem).wait()


def _dispatch(dest, pad_end, padded, n_used, h2_tiles, n_slots):
    n_tok = h2_tiles.shape[0] // ROW_TILE
    tm = 512
    return pl.pallas_call(
        functools.partial(_dispatch_kernel, tm=tm, n_tok=n_tok, n_blocks=n_slots // MOE_BLOCK),
        grid_spec=pltpu.PrefetchScalarGridSpec(
            num_scalar_prefetch=4,
            grid=(n_tok // tm,),
            in_specs=[pl.BlockSpec((tm * ROW_TILE, LANES), lambda i, *_: (i, 0))],
            out_specs=pl.BlockSpec(memory_space=pl.ANY),
            scratch_shapes=[pltpu.VMEM((MOE_BLOCK * ROW_TILE, LANES), F32), pltpu.SemaphoreType.DMA(())]),
        out_shape=jax.ShapeDtypeStruct((n_slots * ROW_TILE, LANES), F32),
        compiler_params=_params(("arbitrary",)),
        name="moe_dispatch",
    )(dest, pad_end, padded, n_used, h2_tiles)


EXPERT_GROUP = 2


WEIGHT_DMA_QUEUE = 1


def _expert_kernel(first_ref, nblk_ref, nb_ref, xs_hbm, wgu_hbm, bgu_ref, wdn_hbm, bdn_ref, y_hbm,
                   wgu_f32, wdn_f32, wgu_bf, wdn_bf, xbuf, ybuf, xtail, ytail, zeros, sem_w, sem_in, sem_out,
                   *, layer, n_blocks):
    e = pl.program_id(0)
    first = first_ref[e]
    n = nblk_ref[e]
    rows = MOE_BLOCK * ROW_TILE
    n_groups = n // EXPERT_GROUP
    tail = first + n_groups * EXPERT_GROUP

    def load(block, count, dst, sem):
        return pltpu.make_async_copy(xs_hbm.at[pl.ds(pl.multiple_of(block * rows, rows), count * rows), :], dst, sem)

    def store(block, count, src, sem):
        return pltpu.make_async_copy(src, y_hbm.at[pl.ds(pl.multiple_of(block * MOE_BLOCK, MOE_BLOCK),
                                                         count * MOE_BLOCK), :], sem)

    def group_load(j, s):
        return load(first + j * EXPERT_GROUP, EXPERT_GROUP, xbuf.at[s], sem_in.at[s])

    def group_store(j, s):
        return store(first + j * EXPERT_GROUP, EXPERT_GROUP, ybuf.at[s], sem_out.at[s])

    def ffn(x_tiles, m):
        x = _from_row_tiles(x_tiles, 0, m).astype(BF16)
        gu = jnp.dot(x, wgu_bf[...], preferred_element_type=F32) + bgu_ref[0, 0]
        gate = jnp.minimum(gu[:, :D_EXPERT], SWIGLU_LIMIT)
        up = jnp.clip(gu[:, D_EXPERT:], -SWIGLU_LIMIT, SWIGLU_LIMIT)
        act = (up + 1.0) * gate * jax.nn.sigmoid(SWIGLU_ALPHA * gate)
        return (jnp.dot(act.astype(BF16), wdn_bf[...], preferred_element_type=F32) + bdn_ref[0, 0]).astype(BF16)

    def weights(ex, fn):
        s = ex % 2
        fn(wgu_hbm.at[layer, ex], wgu_f32.at[s], sem_w.at[0, s])
        fn(wdn_hbm.at[layer, ex], wdn_f32.at[s], sem_w.at[1, s])

    def w_start(src, dst, sem):
        pltpu.async_copy(src, dst, sem, priority=WEIGHT_DMA_QUEUE)

    def w_wait(src, dst, sem):
        pltpu.make_async_copy(src, dst, sem).wait()

    @pl.when(e == 0)
    def _():
        weights(0, w_start)

    has_tail = n_groups * EXPERT_GROUP < n

    @pl.when(n_groups > 0)
    def _():
        group_load(0, 0).start()

    @pl.when(has_tail)
    def _():
        load(tail, 1, xtail, sem_in.at[2]).start()

    @pl.when(e + 1 < N_EXPERTS)
    def _():
        weights(e + 1, w_start)

    weights(e, w_wait)

    @pl.when(n > 0)
    def _():
        wgu_bf[...] = wgu_f32[e % 2].astype(BF16)
        wdn_bf[...] = wdn_f32[e % 2].astype(BF16)

        def group(j, carry):
            s = j % 2
            group_load(j, s).wait()

            @pl.when(j + 1 < n_groups)
            def _():
                group_load(j + 1, 1 - s).start()

            y = ffn(xbuf.at[s], EXPERT_GROUP * MOE_BLOCK)

            @pl.when(j >= 2)
            def _():
                group_store(j - 2, s).wait()

            ybuf[s] = y
            group_store(j, s).start()
            return carry

        lax.fori_loop(0, n_groups, group, 0)

        @pl.when(has_tail)
        def _():
            load(tail, 1, xtail, sem_in.at[2]).wait()
            ytail[...] = ffn(xtail, MOE_BLOCK)
            store(tail, 1, ytail, sem_out.at[2]).start()

        @pl.when(n_groups >= 2)
        def _():
            group_store(n_groups - 2, n_groups % 2).wait()

        @pl.when(n_groups >= 1)
        def _():
            group_store(n_groups - 1, (n_groups - 1) % 2).wait()

        @pl.when(has_tail)
        def _():
            store(tail, 1, ytail, sem_out.at[2]).wait()

    @pl.when(e == N_EXPERTS - 1)
    def _():
        zeros[...] = jnp.zeros_like(zeros)

        def clear(k):
            blk = pl.multiple_of((nb_ref[0] + k) * MOE_BLOCK, MOE_BLOCK)
            return pltpu.make_async_copy(zeros, y_hbm.at[pl.ds(blk, MOE_BLOCK), :], sem_out.at[0])

        for k in range(N_EXPERTS):
            @pl.when(nb_ref[0] + k < n_blocks)
            def _():
                clear(k).start()
        for k in range(N_EXPERTS):
            @pl.when(nb_ref[0] + k < n_blocks)
            def _():
                clear(k).wait()


def _experts(layer, first_block, n_block, n_used, xs_tiles, w_gu, b_gu, w_down, b_down):
    n_slots = xs_tiles.shape[0] // ROW_TILE
    d = w_gu.shape[2]
    de2 = w_gu.shape[3]
    by_expert = lambda e, *_: (layer, e, 0, 0)
    return pl.pallas_call(
        functools.partial(_expert_kernel, layer=layer, n_blocks=n_slots // MOE_BLOCK),
        grid_spec=pltpu.PrefetchScalarGridSpec(
            num_scalar_prefetch=3,
            grid=(N_EXPERTS,),
            in_specs=[pl.BlockSpec(memory_space=pl.ANY),
                      pl.BlockSpec(memory_space=pl.ANY),
                      pl.BlockSpec((1, 1, 1, de2), by_expert),
                      pl.BlockSpec(memory_space=pl.ANY),
                      pl.BlockSpec((1, 1, 1, d), by_expert)],
            out_specs=pl.BlockSpec(memory_space=pl.ANY),
            scratch_shapes=[pltpu.VMEM((2, d, de2), F32), pltpu.VMEM((2, D_EXPERT, d), F32),
                            pltpu.VMEM((d, de2), BF16), pltpu.VMEM((D_EXPERT, d), BF16),
                            pltpu.VMEM((2, EXPERT_GROUP * MOE_BLOCK * ROW_TILE, LANES), F32),
                            pltpu.VMEM((2, EXPERT_GROUP * MOE_BLOCK, d), BF16),
                            pltpu.VMEM((MOE_BLOCK * ROW_TILE, LANES), F32),
                            pltpu.VMEM((MOE_BLOCK, d), BF16),
                            pltpu.VMEM((MOE_BLOCK, d), BF16),
                            pltpu.SemaphoreType.DMA((2, 2)),
                            pltpu.SemaphoreType.DMA((3,)), pltpu.SemaphoreType.DMA((3,))]),
        out_shape=jax.ShapeDtypeStruct((n_slots, d), BF16),
        compiler_params=_params(("arbitrary",)),
        name="moe_experts",
    )(first_block, n_block, n_used, xs_tiles, w_gu, b_gu.reshape(b_gu.shape[0], N_EXPERTS, 1, de2), w_down,
      b_down.reshape(b_down.shape[0], N_EXPERTS, 1, d))


CHUNK = 16
CHUNK_BATCH = 8
COMBINE_ROWS = COMBINE_TILE * TOP_K + 2 * N_EXPERTS * CHUNK
COMBINE_CHUNKS = COMBINE_ROWS // CHUNK


def _combine_kernel(src_ref, nbatch_ref, y_hbm, x1_ref, col_ref, gate_ref, g_ref, out_ref, ybuf, sems,
                    *, tm, n_tiles, final):
    i = pl.program_id(0)
    slot = i % 2
    batch_rows = CHUNK_BATCH * CHUNK

    @pl.when(i == 0)
    def _():
        ybuf[...] = jnp.zeros_like(ybuf)

    def fetch(tile, buf):
        def batch(b, carry):
            for u in range(CHUNK_BATCH):
                c = b * CHUNK_BATCH + u
                src = src_ref[tile * COMBINE_CHUNKS + c]
                pltpu.make_async_copy(y_hbm.at[pl.ds(pl.multiple_of(src, CHUNK), CHUNK), :],
                                      ybuf.at[buf, pl.ds(pl.multiple_of(c * CHUNK, CHUNK), CHUNK), :],
                                      sems.at[buf]).start()
            return carry

        lax.fori_loop(0, nbatch_ref[tile], batch, 0)

    @pl.when(i == 0)
    def _():
        fetch(0, 0)

    @pl.when(i + 1 < n_tiles)
    def _():
        fetch(i + 1, 1 - slot)

    col = col_ref[...]
    gates = gate_ref[...]
    col_id = lax.broadcasted_iota(jnp.int32, (tm, COMBINE_ROWS), 1)
    g = jnp.zeros((tm, COMBINE_ROWS), F32)
    for kk in range(TOP_K):
        g = jnp.where(col_id == col[:, kk:kk + 1], gates[:, kk:kk + 1], g)

    def wait_batch(b, carry):
        pltpu.make_async_copy(y_hbm.at[pl.ds(0, batch_rows), :], ybuf.at[slot, pl.ds(0, batch_rows), :],
                              sems.at[slot]).wait()
        return carry

    lax.fori_loop(0, nbatch_ref[i], wait_batch, 0)
    x = x1_ref[...] + jnp.dot(g.astype(BF16), ybuf[slot], preferred_element_type=F32)
    out_ref[...] = _rms(x, g_ref[...]) if final else x


def _combine(chunk_src, tile_batches, y_slots, x1, col_tk, gates_tk, final_g, final):
    t, d = x1.shape
    tm = COMBINE_TILE
    n_tiles = t // tm
    tok = lambda i, *_: (i, 0)
    return pl.pallas_call(
        functools.partial(_combine_kernel, tm=tm, n_tiles=n_tiles, final=final),
        grid_spec=pltpu.PrefetchScalarGridSpec(
            num_scalar_prefetch=2,
            grid=(n_tiles,),
            in_specs=[pl.BlockSpec(memory_space=pl.ANY),
                      pl.BlockSpec((tm, d), tok),
                      pl.BlockSpec((tm, TOP_K), tok),
                      pl.BlockSpec((tm, TOP_K), tok),
                      pl.BlockSpec((1, d), lambda i, *_: (0, 0))],
            out_specs=pl.BlockSpec((tm, d), tok),
            scratch_shapes=[pltpu.VMEM((2, COMBINE_ROWS, d), BF16), pltpu.SemaphoreType.DMA((2,))]),
        out_shape=jax.ShapeDtypeStruct((t, d), F32),
        compiler_params=_params(("arbitrary",)),
        name="moe_combine",
    )(chunk_src, tile_batches, y_slots, x1, col_tk, gates_tk, final_g)


def _slot_layout(top_idx, rank, counts, tile_base):
    counts = counts.reshape(-1)
    padded = (counts + MOE_BLOCK - 1) // MOE_BLOCK * MOE_BLOCK
    pad_end = jnp.cumsum(padded).astype(jnp.int32)
    pad_start = pad_end - padded
    experts = jnp.arange(N_EXPERTS, dtype=jnp.int32)
    start_of = jnp.sum(jnp.where(top_idx[..., None] == experts, pad_start, 0), axis=-1)
    dest = (start_of + rank).astype(jnp.int32)
    n_used = pad_end[-1] // MOE_BLOCK
    base = tile_base.reshape(-1, N_EXPERTS)
    run_start = pad_start[None, :] + base
    run_len = jnp.concatenate([base[1:], counts[None, :]], axis=0) - base
    seg_a = run_start // CHUNK * CHUNK
    seg_nch = jnp.where(run_len > 0, (run_start + run_len - seg_a + CHUNK - 1) // CHUNK, 0)
    buf_row0 = (jnp.cumsum(seg_nch, axis=1) - seg_nch) * CHUNK
    shift = jnp.repeat(buf_row0 - seg_a, COMBINE_TILE, axis=0)
    col = dest + jnp.sum(jnp.where(top_idx[..., None] == experts, shift[None], 0), axis=-1)
    chunk_end = jnp.cumsum(seg_nch, axis=1)
    c = jnp.arange(COMBINE_CHUNKS, dtype=jnp.int32)
    owner = c[None, :, None] >= chunk_end[:, None, :]
    owner_e = jnp.minimum(jnp.sum(owner.astype(jnp.int32), axis=-1), N_EXPERTS - 1)
    pick = owner_e[..., None] == experts
    first_chunk = jnp.sum(jnp.where(pick, (chunk_end - seg_nch)[:, None, :], 0), axis=-1)
    run_a = jnp.sum(jnp.where(pick, seg_a[:, None, :], 0), axis=-1)
    total = chunk_end[:, -1]
    chunk_src = jnp.where(c[None, :] < total[:, None], run_a + (c[None, :] - first_chunk) * CHUNK, 0)
    flat = lambda a: a.reshape(-1).astype(jnp.int32)
    return (dest, pad_end, padded.astype(jnp.int32), n_used.reshape(1).astype(jnp.int32),
            flat(chunk_src), flat((total + CHUNK_BATCH - 1) // CHUNK_BATCH), col.astype(jnp.int32))


def _rope_tables(seq):
    half = HEAD_DIM // 2
    inv_freq = ROPE_THETA ** (-jnp.arange(half, dtype=F32) / half)
    ang = jnp.arange(seq, dtype=F32)[:, None] * inv_freq[None, :]
    cos, sin = jnp.cos(ang), jnp.sin(ang)
    reps = LANES // HEAD_DIM
    cos_l = jnp.tile(jnp.concatenate([cos, cos], axis=1), (1, reps))
    sin_l = jnp.tile(jnp.concatenate([-sin, sin], axis=1), (1, reps))
    return cos_l, sin_l


def kernel(x, mem, norm1_g, w_in, pool_w, pool_scale, mem_norm_g, w_mem_kv, grp_norm_g, w_out, norm2_g,
           router_w, router_b, w_gu, b_gu, w_down, b_down, final_g):
    batch, seq, d = x.shape
    depth = w_in.shape[0]
    n_mem = mem.shape[1]
    t = batch * seq
    n_blocks = t * TOP_K // MOE_BLOCK + N_EXPERTS
    cos_l, sin_l = _rope_tables(seq)
    x2 = x.reshape(t, d)
    mem2 = mem.reshape(batch * n_mem, d)
    row = lambda a: a.reshape(1, -1)
    for l in range(depth):
        qkv, u, qm = _in_proj(x2, row(norm1_g[l]), w_in[l].astype(BF16), cos_l, sin_l, batch, seq)
        attn = [_dilated_attention(*qkv[n]) for n in range(len(DILATIONS))]
        kv = _mem_kv(mem2, row(mem_norm_g[l]), w_mem_kv[l].astype(BF16)).reshape(batch, n_mem, 2 * MEM_WIDTH)
        pw_bd = jax.scipy.linalg.block_diag(*[pool_w[l, g] for g in range(len(POOL_WINDOWS))]).astype(BF16)
        rw_t = router_w[l].T
        rw_hi = rw_t.astype(BF16)
        rw_lo = (rw_t - rw_hi.astype(F32)).astype(BF16)
        x1, h2_tiles, top_idx, gates, rank, counts, tile_base = _mix_out(
            attn, u, qm, kv, x2, pw_bd, row(pool_scale[l]), row(grp_norm_g[l]), w_out[l].astype(BF16),
            row(norm2_g[l]), rw_hi, rw_lo, router_b[l].reshape(-1, 1), batch, seq)
        dest, pad_end, padded, n_used, chunk_src, tile_batches, col = _slot_layout(
            top_idx, rank, counts, tile_base)
        xs_tiles = _dispatch(dest.reshape(-1), pad_end, padded, n_used, h2_tiles, n_blocks * MOE_BLOCK)
        y_slots = _experts(l, (pad_end - padded) // MOE_BLOCK, padded // MOE_BLOCK, n_used, xs_tiles,
                           w_gu, b_gu, w_down, b_down)
        x2 = _combine(chunk_src, tile_batches, y_slots, x1, col.T, gates.T, row(final_g),
                      final=(l == depth - 1))
    return x2.reshape(batch, seq, d)
```

```python
import functools

import jax
import jax.numpy as jnp
from jax import lax
from jax.experimental import pallas as pl
from jax.experimental.pallas import tpu as pltpu

D_MODEL = 1024
HEAD_DIM = 64
ATTN_WIDTH = 512
DILATIONS = (1, 4, 16)
BAND = 64
ROPE_THETA = 10000.0
POOL_WINDOWS = (2, 4, 8, 16)
POOL_WIDTH = 256
POOL_GROUP = 64
POOL_HALO = 8
MEM_WIDTH = 256
N_EXPERTS = 32
TOP_K = 4
D_EXPERT = 1024
SWIGLU_ALPHA = 1.702
SWIGLU_LIMIT = 7.0
MOE_BLOCK = 256
IN_PROJ_TILE = 512
ATTN_TILE = 512
ATTN_SUBTILE = 128
MEM_KV_TILE = 256
ROUTER_TILE = 512
DISPATCH_TILE = 1024
COMBINE_TILE = 256
NORM_EPS = 1e-5
NEG_INF = -1e30
LANES = 128
SUBLANES = 8
ROW_TILE = D_MODEL // LANES

F32 = jnp.float32
BF16 = jnp.bfloat16
VMEM_LIMIT = 56 * 1024 * 1024

_NT = (((1,), (1,)), ((), ()))


def _params(sem, vmem=VMEM_LIMIT):
    return pltpu.CompilerParams(dimension_semantics=sem, vmem_limit_bytes=vmem)


def _rms(x, g):
    return x * lax.rsqrt(jnp.mean(x * x, axis=-1, keepdims=True) + NORM_EPS) * g


def _to_row_tiles(ref, val):
    m = val.shape[0]
    for s in range(ROW_TILE):
        ref[pl.ds(s, m, stride=ROW_TILE), :] = val[:, s * LANES:(s + 1) * LANES]


def _from_row_tiles(ref, start, m):
    return jnp.concatenate([ref[pl.ds(start * ROW_TILE + s, m, stride=ROW_TILE), :] for s in range(ROW_TILE)],
                           axis=1)


def _in_proj_kernel(x_ref, g_ref, w_ref, cos_ref, sin_ref, q1, k1, v1, q4, k4, v4, q16, k16, v16, u_ref, qm_ref,
                    qkv, cls4, *, tm):
    h = _rms(x_ref[...], g_ref[...]).astype(BF16)
    proj = jnp.dot(h, w_ref[...], preferred_element_type=F32)
    cos = cos_ref[...]
    sin = sin_ref[...]
    lane = lax.broadcasted_iota(jnp.int32, cos.shape, 1)
    first_half = (lane % HEAD_DIM) < (HEAD_DIM // 2)
    scale = HEAD_DIM ** -0.5

    def rope(t):
        partner = jnp.where(first_half, pltpu.roll(t, LANES - HEAD_DIM // 2, 1),
                            pltpu.roll(t, HEAD_DIM // 2, 1))
        return t * cos + partner * sin

    a = ATTN_WIDTH
    groups = a // LANES
    for c in range(groups):
        qkv[c] = rope(proj[:, c * LANES:(c + 1) * LANES]) * scale
        qkv[groups + c] = rope(proj[:, a + c * LANES:a + (c + 1) * LANES])
        qkv[2 * groups + c] = proj[:, 2 * a + c * LANES:2 * a + (c + 1) * LANES]
    u_ref[...] = proj[:, 3 * a:3 * a + POOL_WIDTH]
    qm_ref[...] = (proj[:, 3 * a + POOL_WIDTH:] * scale).astype(BF16)

    for n, (r1, r4, r16) in enumerate(((q1, q4, q16), (k1, k4, k16), (v1, v4, v16))):
        for c in range(groups):
            g = n * groups + c
            cols = slice(c * LANES, (c + 1) * LANES)
            r1[0, 0, :, cols] = qkv[g].astype(BF16)
            for a in range(4):
                rows = qkv[g, pl.ds(a, tm // 4, stride=4), :]
                r4[0, a, :, cols] = rows.astype(BF16)
                cls4[g * 4 + a] = rows
            for a in range(4):
                for b in range(4):
                    r16[0, a + 4 * b, :, cols] = cls4[g * 4 + a, pl.ds(b, tm // 16, stride=4), :].astype(BF16)


def _in_proj(x2, g, w_bf, cos, sin, batch, seq):
    t, d = x2.shape
    tm = IN_PROJ_TILE
    tiles_per_seq = seq // tm
    cols = w_bf.shape[1]
    row = lambda i: (i, 0)
    fixed = lambda i: (0, 0)
    cls = lambda i: (i // tiles_per_seq, 0, i % tiles_per_seq, 0)
    cls_specs, cls_shapes = [], []
    for dil in DILATIONS:
        for _ in range(3):
            cls_specs.append(pl.BlockSpec((1, dil, tm // dil, ATTN_WIDTH), cls))
            cls_shapes.append(jax.ShapeDtypeStruct((batch, dil, seq // dil, ATTN_WIDTH), BF16))
    outs = pl.pallas_call(
        functools.partial(_in_proj_kernel, tm=tm),
        grid=(t // tm,),
        in_specs=[pl.BlockSpec((tm, d), row),
                  pl.BlockSpec((1, d), fixed),
                  pl.BlockSpec((d, cols), fixed),
                  pl.BlockSpec((tm, LANES), lambda i: (i % tiles_per_seq, 0)),
                  pl.BlockSpec((tm, LANES), lambda i: (i % tiles_per_seq, 0))],
        out_specs=cls_specs + [pl.BlockSpec((tm, POOL_WIDTH), row), pl.BlockSpec((tm, MEM_WIDTH), row)],
        out_shape=cls_shapes + [jax.ShapeDtypeStruct((t, POOL_WIDTH), F32),
                                jax.ShapeDtypeStruct((t, MEM_WIDTH), BF16)],
        scratch_shapes=[pltpu.VMEM((3 * ATTN_WIDTH // LANES, tm, LANES), F32),
                        pltpu.VMEM((4 * 3 * ATTN_WIDTH // LANES, tm // 4, LANES), F32)],
        compiler_params=_params(("parallel",)),
        name="in_proj",
    )(x2, g, w_bf, cos, sin)
    qkv = [outs[3 * n:3 * n + 3] for n in range(len(DILATIONS))]
    return qkv, outs[-2], outs[-1]


def _head_pair_attention(q, k, v, valid):
    lane = lax.broadcasted_iota(jnp.int32, q.shape, 1)
    outs, lses = [], []
    for half in range(2):
        mine = (lane // HEAD_DIM) == half
        s = lax.dot_general(jnp.where(mine, q, jnp.zeros_like(q)), k, _NT,
                            preferred_element_type=F32)
        if valid is not None:
            s = jnp.where(valid, s, NEG_INF)
        m = jnp.max(s, axis=-1, keepdims=True)
        p = jnp.exp(s - m)
        den = jnp.sum(p, axis=-1, keepdims=True)
        pv = jnp.dot(p.astype(BF16), v, preferred_element_type=F32)
        outs.append(pv / den)
        lses.append(m + jnp.log(den))
    return jnp.where((lane // HEAD_DIM) == 0, outs[0], outs[1]), lses


def _attn_kernel(q_ref, kp_ref, kc_ref, kn_ref, vp_ref, vc_ref, vn_ref, o_ref, lse_ref, *, tq, sub, length):
    j = pl.program_id(2)
    nk = sub + 2 * BAND
    row = lax.broadcasted_iota(jnp.int32, (sub, nk), 0)
    col = lax.broadcasted_iota(jnp.int32, (sub, nk), 1)
    in_band = jnp.abs(col - BAND - row) <= BAND
    lane = lax.broadcasted_iota(jnp.int32, (sub, LANES), 1)
    for c in range(ATTN_WIDTH // LANES):
        sl = slice(c * LANES, (c + 1) * LANES)
        k = jnp.concatenate([kp_ref[0, 0, :, sl], kc_ref[0, 0, :, sl], kn_ref[0, 0, :, sl]], axis=0)
        v = jnp.concatenate([vp_ref[0, 0, :, sl], vc_ref[0, 0, :, sl], vn_ref[0, 0, :, sl]], axis=0)
        for s in range(tq // sub):
            rows = slice(s * sub, (s + 1) * sub)
            key = j * tq + s * sub - BAND + col
            valid = in_band & (key >= 0) & (key < length)
            o, lses = _head_pair_attention(q_ref[0, 0, rows, sl], k[s * sub:s * sub + nk], v[s * sub:s * sub + nk],
                                           valid)
            o_ref[0, 0, rows, sl] = o
            lse_tile = jnp.where(lane == 2 * c, lses[0], lses[1])
            if c == 0:
                lse_ref[0, 0, rows, :] = lse_tile
            else:
                lse_ref[0, 0, rows, :] = jnp.where((lane // 2) == c, lse_tile, lse_ref[0, 0, rows, :])


def _dilated_attention(q, k, v):
    batch, dilation, length, _ = q.shape
    tq = ATTN_TILE
    per = tq // BAND
    n_band_blocks = length // BAND
    cur = lambda b, r, j: (b, r, j, 0)
    prev = lambda b, r, j: (b, r, jnp.maximum(j * per - 1, 0), 0)
    nxt = lambda b, r, j: (b, r, jnp.minimum((j + 1) * per, n_band_blocks - 1), 0)
    big = lambda imap: pl.BlockSpec((1, 1, tq, ATTN_WIDTH), imap)
    halo = lambda imap: pl.BlockSpec((1, 1, BAND, ATTN_WIDTH), imap)
    return pl.pallas_call(
        functools.partial(_attn_kernel, tq=tq, sub=ATTN_SUBTILE, length=length),
        grid=(batch, dilation, length // tq),
        in_specs=[big(cur), halo(prev), big(cur), halo(nxt), halo(prev), big(cur), halo(nxt)],
        out_specs=[pl.BlockSpec((1, 1, tq, ATTN_WIDTH), cur), pl.BlockSpec((1, 1, tq, LANES), cur)],
        out_shape=[jax.ShapeDtypeStruct((batch, dilation, length, ATTN_WIDTH), F32),
                   jax.ShapeDtypeStruct((batch, dilation, length, LANES), F32)],
        compiler_params=_params(("parallel", "parallel", "parallel")),
        name=f"dilated_attn_d{dilation}",
    )(q, k, k, k, v, v, v)


def _mem_kv_kernel(m_ref, g_ref, w_ref, kv_ref):
    h = _rms(m_ref[...], g_ref[...]).astype(BF16)
    kv_ref[...] = jnp.dot(h, w_ref[...], preferred_element_type=F32).astype(BF16)


def _mem_kv(mem2, g, w_bf):
    n, d = mem2.shape
    cols = w_bf.shape[1]
    tm = MEM_KV_TILE
    return pl.pallas_call(
        _mem_kv_kernel,
        grid=(n // tm,),
        in_specs=[pl.BlockSpec((tm, d), lambda i: (i, 0)),
                  pl.BlockSpec((1, d), lambda i: (0, 0)),
                  pl.BlockSpec((d, cols), lambda i: (0, 0))],
        out_specs=pl.BlockSpec((tm, cols), lambda i: (i, 0)),
        out_shape=jax.ShapeDtypeStruct((n, cols), BF16),
        compiler_params=_params(("parallel",)),
        name="mem_kv",
    )(mem2, g, w_bf)


def _mix_out_kernel(o1_ref, o4_ref, o16_ref, l1_ref, l4_ref, l16_ref, up_ref, u_ref, un_ref, qm_ref, kv_ref,
                    x_ref, pw_ref, ps_ref, gg_ref, wo_ref, n2_ref, rwh_ref, rwl_ref, rb_ref,
                    x1_ref, h2_ref, idx_ref, gate_ref, rank_ref, cnt_ref, base_ref,
                    o4_s, o16_s, l4_s, l16_s, cls4, carry, *, tm, seq):
    i = pl.program_id(0)
    tiles_per_seq = seq // tm
    pos0 = (i % tiles_per_seq) * tm

    for src4, src16, dst4, dst16 in ((o4_ref, o16_ref, o4_s, o16_s), (l4_ref, l16_ref, l4_s, l16_s)):
        for c in range(dst4.shape[0]):
            cols = slice(c * LANES, (c + 1) * LANES)
            for a in range(4):
                dst4[c, pl.ds(a, tm // 4, stride=4), :] = src4[0, a, :, cols]
                for b in range(4):
                    cls4[a, pl.ds(b, tm // 16, stride=4), :] = src16[0, a + 4 * b, :, cols]
            for a in range(4):
                dst16[c, pl.ds(a, tm // 4, stride=4), :] = cls4[a]

    l1, l2, l3 = l1_ref[0, 0], l4_s[0], l16_s[0]
    lm = jnp.maximum(jnp.maximum(l1, l2), l3)
    e1, e2, e3 = jnp.exp(l1 - lm), jnp.exp(l2 - lm), jnp.exp(l3 - lm)
    es = e1 + e2 + e3
    w1, w2, w3 = e1 / es, e2 / es, e3 / es
    ya = []
    for c in range(ATTN_WIDTH // LANES):
        sl = slice(c * LANES, (c + 1) * LANES)
        lane = lax.broadcasted_iota(jnp.int32, (tm, LANES), 1)
        lo = lane < HEAD_DIM

        def per_head(w):
            return jnp.where(lo, w[:, 2 * c:2 * c + 1], w[:, 2 * c + 1:2 * c + 2])

        ya.append(per_head(w1) * o1_ref[0, 0, :, sl] + per_head(w2) * o4_s[c] + per_head(w3) * o16_s[c])
    ya = jnp.concatenate(ya, axis=1)

    u = u_ref[...]
    before = jnp.where(pos0 > 0, up_ref[...], 0.0)
    after = jnp.where(pos0 + tm < seq, un_ref[...], 0.0)
    ext = jnp.concatenate([before, u, after], axis=0)
    n_ext = tm + 2 * POOL_HALO
    sums = []
    acc = ext
    shift = 1
    for w in POOL_WINDOWS:
        if w == 2:
            acc = pltpu.roll(ext, 1, 0) + ext
        else:
            acc = pltpu.roll(acc, shift, 0) + pltpu.roll(acc, n_ext - shift, 0)
            shift *= 2
        sums.append(acc[POOL_HALO:POOL_HALO + tm])
    pos = pos0 + lax.broadcasted_iota(jnp.int32, (tm, 1), 0)
    grp = lax.broadcasted_iota(jnp.int32, (tm, POOL_WIDTH), 1) // POOL_GROUP
    mean = jnp.zeros((tm, POOL_WIDTH), F32)
    for g, w in enumerate(POOL_WINDOWS):
        cnt = jnp.minimum(pos + (w - 1 - w // 2), seq - 1) + 1 - jnp.maximum(pos - w // 2, 0)
        mean = jnp.where(grp == g, sums[g] / cnt.astype(F32), mean)
    d = (mean - u).astype(BF16)
    yp = jnp.dot(d, pw_ref[...], preferred_element_type=F32) * ps_ref[...]

    ym = []
    for c in range(MEM_WIDTH // LANES):
        sl = slice(c * LANES, (c + 1) * LANES)
        o, _ = _head_pair_attention(qm_ref[:, sl], kv_ref[0, :, sl],
                                    kv_ref[0, :, MEM_WIDTH + c * LANES:MEM_WIDTH + (c + 1) * LANES], None)
        ym.append(o)
    ym = jnp.concatenate(ym, axis=1)

    gg = gg_ref[...]
    a, p = ATTN_WIDTH, POOL_WIDTH
    y = jnp.concatenate([_rms(ya, gg[:, :a]), _rms(yp, gg[:, a:a + p]), _rms(ym, gg[:, a + p:])], axis=1)
    x1 = x_ref[...] + jnp.dot(y.astype(BF16), wo_ref[...], preferred_element_type=F32)
    x1_ref[...] = x1

    h2 = _rms(x1, n2_ref[...])
    _to_row_tiles(h2_ref, h2)

    hi = h2.astype(BF16)
    lo = (h2 - hi.astype(F32)).astype(BF16)
    logits = (lax.dot_general(rwh_ref[...], hi, _NT, preferred_element_type=F32)
              + lax.dot_general(rwh_ref[...], lo, _NT, preferred_element_type=F32)
              + lax.dot_general(rwl_ref[...], hi, _NT, preferred_element_type=F32)) + rb_ref[...]
    eidx = lax.broadcasted_iota(jnp.int32, logits.shape, 0)
    krow = lax.broadcasted_iota(jnp.int32, (TOP_K, tm), 0)
    vals = jnp.zeros((TOP_K, tm), F32)
    idxs = jnp.zeros((TOP_K, tm), jnp.int32)
    work = logits
    args = []
    for kk in range(TOP_K):
        best = jnp.max(work, axis=0, keepdims=True)
        arg = jnp.min(jnp.where(work == best, eidx, N_EXPERTS), axis=0, keepdims=True)
        vals = jnp.where(krow == kk, best, vals)
        idxs = jnp.where(krow == kk, arg, idxs)
        work = jnp.where(eidx == arg, -jnp.inf, work)
        args.append(arg)
    ex = jnp.exp(vals - vals[0:1])
    gate_ref[...] = ex / jnp.sum(ex, axis=0, keepdims=True)
    idx_ref[...] = idxs

    @pl.when(i == 0)
    def _():
        carry[...] = jnp.zeros_like(carry)

    chosen = (work == -jnp.inf).astype(BF16)
    earlier = (lax.broadcasted_iota(jnp.int32, (tm, tm), 0)
               < lax.broadcasted_iota(jnp.int32, (tm, tm), 1)).astype(BF16)
    before_me = jnp.dot(chosen, earlier, preferred_element_type=F32) + carry[...]
    ranks = jnp.zeros((TOP_K, tm), F32)
    for kk in range(TOP_K):
        ranks = jnp.where(krow == kk, jnp.sum(jnp.where(eidx == args[kk], before_me, 0.0), axis=0, keepdims=True),
                          ranks)
    rank_ref[...] = ranks.astype(jnp.int32)
    for part in range(tm // COMBINE_TILE):
        base_ref[part] = before_me[:, part * COMBINE_TILE:part * COMBINE_TILE + 1].astype(jnp.int32)
    carry[...] = carry[...] + jnp.sum(chosen.astype(F32), axis=1, keepdims=True)
    cnt_ref[...] = carry[...].astype(jnp.int32)


def _mix_out(attn, u, qm, kv, x2, pw_bd, ps, gg, wo_bf, n2, rw_hi, rw_lo, rb, batch, seq):
    t, d = x2.shape
    tm = ROUTER_TILE
    hp = tm // POOL_HALO
    n_halo = t // POOL_HALO
    tiles_per_seq = seq // tm
    row = lambda i: (i, 0)
    fixed = lambda i: (0, 0)
    cls = lambda i: (i // tiles_per_seq, 0, i % tiles_per_seq, 0)
    rowspec = lambda w: pl.BlockSpec((tm, w), row)
    clsspec = lambda dil, w: pl.BlockSpec((1, dil, tm // dil, w), cls)
    tok_cols = lambda i: (0, i)
    (o1, l1), (o4, l4), (o16, l16) = attn
    return pl.pallas_call(
        functools.partial(_mix_out_kernel, tm=tm, seq=seq),
        grid=(t // tm,),
        in_specs=[clsspec(1, ATTN_WIDTH), clsspec(4, ATTN_WIDTH), clsspec(16, ATTN_WIDTH),
                  clsspec(1, LANES), clsspec(4, LANES), clsspec(16, LANES),
                  pl.BlockSpec((POOL_HALO, POOL_WIDTH), lambda i: (jnp.maximum(i * hp - 1, 0), 0)),
                  rowspec(POOL_WIDTH),
                  pl.BlockSpec((POOL_HALO, POOL_WIDTH), lambda i: (jnp.minimum((i + 1) * hp, n_halo - 1), 0)),
                  rowspec(MEM_WIDTH),
                  pl.BlockSpec((1, kv.shape[1], kv.shape[2]), lambda i: (i // tiles_per_seq, 0, 0)),
                  rowspec(d),
                  pl.BlockSpec(pw_bd.shape, fixed),
                  pl.BlockSpec(ps.shape, fixed),
                  pl.BlockSpec(gg.shape, fixed),
                  pl.BlockSpec(wo_bf.shape, fixed),
                  pl.BlockSpec(n2.shape, fixed),
                  pl.BlockSpec(rw_hi.shape, fixed),
                  pl.BlockSpec(rw_lo.shape, fixed),
                  pl.BlockSpec(rb.shape, fixed)],
        out_specs=[rowspec(d),
                   pl.BlockSpec((tm * ROW_TILE, LANES), row),
                   pl.BlockSpec((TOP_K, tm), tok_cols),
                   pl.BlockSpec((TOP_K, tm), tok_cols),
                   pl.BlockSpec((TOP_K, tm), tok_cols),
                   pl.BlockSpec((N_EXPERTS, 1), fixed),
                   pl.BlockSpec((tm // COMBINE_TILE, N_EXPERTS, 1), lambda i: (i, 0, 0))],
        out_shape=[jax.ShapeDtypeStruct((t, d), F32),
                   jax.ShapeDtypeStruct((t * ROW_TILE, LANES), F32),
                   jax.ShapeDtypeStruct((TOP_K, t), jnp.int32),
                   jax.ShapeDtypeStruct((TOP_K, t), F32),
                   jax.ShapeDtypeStruct((TOP_K, t), jnp.int32),
                   jax.ShapeDtypeStruct((N_EXPERTS, 1), jnp.int32),
                   jax.ShapeDtypeStruct((t // COMBINE_TILE, N_EXPERTS, 1), jnp.int32)],
        scratch_shapes=[pltpu.VMEM((ATTN_WIDTH // LANES, tm, LANES), F32),
                        pltpu.VMEM((ATTN_WIDTH // LANES, tm, LANES), F32),
                        pltpu.VMEM((1, tm, LANES), F32), pltpu.VMEM((1, tm, LANES), F32),
                        pltpu.VMEM((4, tm // 4, LANES), F32),
                        pltpu.VMEM((N_EXPERTS, 1), F32)],
        compiler_params=_params(("arbitrary",)),
        name="mix_out_router",
    )(o1, o4, o16, l1, l4, l16, u, u, u, qm, kv, x2, pw_bd, ps, gg, wo_bf, n2, rw_hi, rw_lo, rb)


def _dispatch_kernel(dest_ref, pend_ref, padded_ref, nb_ref, h_ref, xs_hbm, zeros, sem, *, tm, n_tok, n_blocks):
    i = pl.program_id(0)
    blk = MOE_BLOCK * ROW_TILE

    @pl.when(i == 0)
    def _():
        zeros[...] = jnp.zeros_like(zeros)

        def clear(block):
            return pltpu.make_async_copy(zeros, xs_hbm.at[pl.ds(pl.multiple_of(block * blk, blk), blk), :], sem)

        def for_each_cleared_block(fn):
            for e in range(N_EXPERTS):
                @pl.when(padded_ref[e] > 0)
                def _():
                    fn(clear(pend_ref[e] // MOE_BLOCK - 1))

                @pl.when(nb_ref[0] + e < n_blocks)
                def _():
                    fn(clear(nb_ref[0] + e))

        for_each_cleared_block(lambda c: c.start())
        for_each_cleared_block(lambda c: c.wait())

    def start(r, c):
        src = h_ref.at[pl.ds(pl.multiple_of(r * ROW_TILE, ROW_TILE), ROW_TILE), :]
        for kk in range(TOP_K):
            slot = dest_ref[kk * n_tok + i * tm + r]
            pltpu.async_copy(src, xs_hbm.at[pl.ds(pl.multiple_of(slot * ROW_TILE, ROW_TILE), ROW_TILE), :],
                             sem, priority=kk % 2)
        return c

    lax.fori_loop(0, tm, start, 0, unroll=4)
    n = tm * TOP_K * ROW_TILE
    pltpu.make_async_copy(xs_hbm.at[pl.ds(0, n), :], xs_hbm.at[pl.ds(0, n), :], sem).wait()


def _dispatch(dest, pad_end, padded, n_used, h2_tiles, n_slots):
    n_tok = h2_tiles.shape[0] // ROW_TILE
    tm = DISPATCH_TILE
    return pl.pallas_call(
        functools.partial(_dispatch_kernel, tm=tm, n_tok=n_tok, n_blocks=n_slots // MOE_BLOCK),
        grid_spec=pltpu.PrefetchScalarGridSpec(
            num_scalar_prefetch=4,
            grid=(n_tok // tm,),
            in_specs=[pl.BlockSpec((tm * ROW_TILE, LANES), lambda i, *_: (i, 0))],
            out_specs=pl.BlockSpec(memory_space=pl.ANY),
            scratch_shapes=[pltpu.VMEM((MOE_BLOCK * ROW_TILE, LANES), F32), pltpu.SemaphoreType.DMA(())]),
        out_shape=jax.ShapeDtypeStruct((n_slots * ROW_TILE, LANES), F32),
        compiler_params=_params(("arbitrary",)),
        name="moe_dispatch",
    )(dest, pad_end, padded, n_used, h2_tiles)


EXPERT_GROUP = 2


WEIGHT_DMA_QUEUE = 1


def _expert_kernel(first_ref, nblk_ref, nb_ref, xs_hbm, wgu_hbm, bgu_ref, wdn_hbm, bdn_ref, y_hbm,
                   wgu_f32, wdn_f32, wgu_bf, wdn_bf, xbuf, ybuf, xtail, ytail, zeros, sem_w, sem_in, sem_out,
                   *, layer, n_blocks):
    e = pl.program_id(0)
    first = first_ref[e]
    n = nblk_ref[e]
    rows = MOE_BLOCK * ROW_TILE
    n_groups = n // EXPERT_GROUP
    tail = first + n_groups * EXPERT_GROUP

    def load(block, count, dst, sem):
        return pltpu.make_async_copy(xs_hbm.at[pl.ds(pl.multiple_of(block * rows, rows), count * rows), :], dst, sem)

    def store(block, count, src, sem):
        return pltpu.make_async_copy(src, y_hbm.at[pl.ds(pl.multiple_of(block * MOE_BLOCK, MOE_BLOCK),
                                                         count * MOE_BLOCK), :], sem)

    def group_load(j, s):
        return load(first + j * EXPERT_GROUP, EXPERT_GROUP, xbuf.at[s], sem_in.at[s])

    def group_store(j, s):
        return store(first + j * EXPERT_GROUP, EXPERT_GROUP, ybuf.at[s], sem_out.at[s])

    def ffn(x_tiles, m):
        x = _from_row_tiles(x_tiles, 0, m).astype(BF16)
        gu = jnp.dot(x, wgu_bf[...], preferred_element_type=F32) + bgu_ref[0, 0]
        gate = jnp.minimum(gu[:, :D_EXPERT], SWIGLU_LIMIT)
        up = jnp.clip(gu[:, D_EXPERT:], -SWIGLU_LIMIT, SWIGLU_LIMIT)
        act = (up + 1.0) * gate * jax.nn.sigmoid(SWIGLU_ALPHA * gate)
        return (jnp.dot(act.astype(BF16), wdn_bf[...], preferred_element_type=F32) + bdn_ref[0, 0]).astype(BF16)

    def weights(ex, fn):
        s = ex % 2
        fn(wgu_hbm.at[layer, ex], wgu_f32.at[s], sem_w.at[0, s])
        fn(wdn_hbm.at[layer, ex], wdn_f32.at[s], sem_w.at[1, s])

    def w_start(src, dst, sem):
        pltpu.async_copy(src, dst, sem, priority=WEIGHT_DMA_QUEUE)

    def w_wait(src, dst, sem):
        pltpu.make_async_copy(src, dst, sem).wait()

    @pl.when(e == 0)
    def _():
        weights(0, w_start)

    has_tail = n_groups * EXPERT_GROUP < n

    @pl.when(n_groups > 0)
    def _():
        group_load(0, 0).start()

    @pl.when(has_tail)
    def _():
        load(tail, 1, xtail, sem_in.at[2]).start()

    @pl.when(e + 1 < N_EXPERTS)
    def _():
        weights(e + 1, w_start)

    weights(e, w_wait)

    @pl.when(n > 0)
    def _():
        wgu_bf[...] = wgu_f32[e % 2].astype(BF16)
        wdn_bf[...] = wdn_f32[e % 2].astype(BF16)

        def group(j, carry):
            s = j % 2
            group_load(j, s).wait()

            @pl.when(j + 1 < n_groups)
            def _():
                group_load(j + 1, 1 - s).start()

            y = ffn(xbuf.at[s], EXPERT_GROUP * MOE_BLOCK)

            @pl.when(j >= 2)
            def _():
                group_store(j - 2, s).wait()

            ybuf[s] = y
            group_store(j, s).start()
            return carry

        lax.fori_loop(0, n_groups, group, 0)

        @pl.when(has_tail)
        def _():
            load(tail, 1, xtail, sem_in.at[2]).wait()
            ytail[...] = ffn(xtail, MOE_BLOCK)
            store(tail, 1, ytail, sem_out.at[2]).start()

        @pl.when(n_groups >= 2)
        def _():
            group_store(n_groups - 2, n_groups % 2).wait()

        @pl.when(n_groups >= 1)
        def _():
            group_store(n_groups - 1, (n_groups - 1) % 2).wait()

        @pl.when(has_tail)
        def _():
            store(tail, 1, ytail, sem_out.at[2]).wait()

    @pl.when(e == N_EXPERTS - 1)
    def _():
        zeros[...] = jnp.zeros_like(zeros)

        def clear(k):
            blk = pl.multiple_of((nb_ref[0] + k) * MOE_BLOCK, MOE_BLOCK)
            return pltpu.make_async_copy(zeros, y_hbm.at[pl.ds(blk, MOE_BLOCK), :], sem_out.at[0])

        for k in range(N_EXPERTS):
            @pl.when(nb_ref[0] + k < n_blocks)
            def _():
                clear(k).start()
        for k in range(N_EXPERTS):
            @pl.when(nb_ref[0] + k < n_blocks)
            def _():
                clear(k).wait()


def _experts(layer, first_block, n_block, n_used, xs_tiles, w_gu, b_gu, w_down, b_down):
    n_slots = xs_tiles.shape[0] // ROW_TILE
    d = w_gu.shape[2]
    de2 = w_gu.shape[3]
    by_expert = lambda e, *_: (layer, e, 0, 0)
    return pl.pallas_call(
        functools.partial(_expert_kernel, layer=layer, n_blocks=n_slots // MOE_BLOCK),
        grid_spec=pltpu.PrefetchScalarGridSpec(
            num_scalar_prefetch=3,
            grid=(N_EXPERTS,),
            in_specs=[pl.BlockSpec(memory_space=pl.ANY),
                      pl.BlockSpec(memory_space=pl.ANY),
                      pl.BlockSpec((1, 1, 1, de2), by_expert),
                      pl.BlockSpec(memory_space=pl.ANY),
                      pl.BlockSpec((1, 1, 1, d), by_expert)],
            out_specs=pl.BlockSpec(memory_space=pl.ANY),
            scratch_shapes=[pltpu.VMEM((2, d, de2), F32), pltpu.VMEM((2, D_EXPERT, d), F32),
                            pltpu.VMEM((d, de2), BF16), pltpu.VMEM((D_EXPERT, d), BF16),
                            pltpu.VMEM((2, EXPERT_GROUP * MOE_BLOCK * ROW_TILE, LANES), F32),
                            pltpu.VMEM((2, EXPERT_GROUP * MOE_BLOCK, d), BF16),
                            pltpu.VMEM((MOE_BLOCK * ROW_TILE, LANES), F32),
                            pltpu.VMEM((MOE_BLOCK, d), BF16),
                            pltpu.VMEM((MOE_BLOCK, d), BF16),
                            pltpu.SemaphoreType.DMA((2, 2)),
                            pltpu.SemaphoreType.DMA((3,)), pltpu.SemaphoreType.DMA((3,))]),
        out_shape=jax.ShapeDtypeStruct((n_slots, d), BF16),
        compiler_params=_params(("arbitrary",)),
        name="moe_experts",
    )(first_block, n_block, n_used, xs_tiles, w_gu, b_gu.reshape(b_gu.shape[0], N_EXPERTS, 1, de2), w_down,
      b_down.reshape(b_down.shape[0], N_EXPERTS, 1, d))


CHUNK = 16
CHUNK_BATCH = 16
COMBINE_ROWS = COMBINE_TILE * TOP_K + 2 * N_EXPERTS * CHUNK
COMBINE_CHUNKS = COMBINE_ROWS // CHUNK


def _combine_kernel(src_ref, nbatch_ref, y_hbm, x1_ref, col_ref, gate_ref, g_ref, out_ref, ybuf, sems,
                    *, tm, n_tiles, final):
    i = pl.program_id(0)
    slot = i % 2
    batch_rows = CHUNK_BATCH * CHUNK

    @pl.when(i == 0)
    def _():
        ybuf[...] = jnp.zeros_like(ybuf)

    def fetch(tile, buf):
        def batch(b, carry):
            for u in range(CHUNK_BATCH):
                c = b * CHUNK_BATCH + u
                src = src_ref[tile * COMBINE_CHUNKS + c]
                pltpu.make_async_copy(y_hbm.at[pl.ds(pl.multiple_of(src, CHUNK), CHUNK), :],
                                      ybuf.at[buf, pl.ds(pl.multiple_of(c * CHUNK, CHUNK), CHUNK), :],
                                      sems.at[buf]).start()
            return carry

        lax.fori_loop(0, nbatch_ref[tile], batch, 0)

    @pl.when(i == 0)
    def _():
        fetch(0, 0)

    @pl.when(i + 1 < n_tiles)
    def _():
        fetch(i + 1, 1 - slot)

    col = col_ref[...]
    gates = gate_ref[...]
    col_id = lax.broadcasted_iota(jnp.int32, (tm, COMBINE_ROWS), 1)
    g = jnp.zeros((tm, COMBINE_ROWS), F32)
    for kk in range(TOP_K):
        g = jnp.where(col_id == col[:, kk:kk + 1], gates[:, kk:kk + 1], g)

    def wait_batch(b, carry):
        pltpu.make_async_copy(y_hbm.at[pl.ds(0, batch_rows), :], ybuf.at[slot, pl.ds(0, batch_rows), :],
                              sems.at[slot]).wait()
        return carry

    lax.fori_loop(0, nbatch_ref[i], wait_batch, 0)
    x = x1_ref[...] + jnp.dot(g.astype(BF16), ybuf[slot], preferred_element_type=F32)
    out_ref[...] = _rms(x, g_ref[...]) if final else x


def _combine(chunk_src, tile_batches, y_slots, x1, col_tk, gates_tk, final_g, final):
    t, d = x1.shape
    tm = COMBINE_TILE
    n_tiles = t // tm
    tok = lambda i, *_: (i, 0)
    return pl.pallas_call(
        functools.partial(_combine_kernel, tm=tm, n_tiles=n_tiles, final=final),
        grid_spec=pltpu.PrefetchScalarGridSpec(
            num_scalar_prefetch=2,
            grid=(n_tiles,),
            in_specs=[pl.BlockSpec(memory_space=pl.ANY),
                      pl.BlockSpec((tm, d), tok),
                      pl.BlockSpec((tm, TOP_K), tok),
                      pl.BlockSpec((tm, TOP_K), tok),
                      pl.BlockSpec((1, d), lambda i, *_: (0, 0))],
            out_specs=pl.BlockSpec((tm, d), tok),
            scratch_shapes=[pltpu.VMEM((2, COMBINE_ROWS, d), BF16), pltpu.SemaphoreType.DMA((2,))]),
        out_shape=jax.ShapeDtypeStruct((t, d), F32),
        compiler_params=_params(("arbitrary",)),
        name="moe_combine",
    )(chunk_src, tile_batches, y_slots, x1, col_tk, gates_tk, final_g)


def _slot_layout(top_idx, rank, counts, tile_base):
    counts = counts.reshape(-1)
    padded = (counts + MOE_BLOCK - 1) // MOE_BLOCK * MOE_BLOCK
    pad_end = jnp.cumsum(padded).astype(jnp.int32)
    pad_start = pad_end - padded
    experts = jnp.arange(N_EXPERTS, dtype=jnp.int32)
    start_of = jnp.sum(jnp.where(top_idx[..., None] == experts, pad_start, 0), axis=-1)
    dest = (start_of + rank).astype(jnp.int32)
    n_used = pad_end[-1] // MOE_BLOCK
    base = tile_base.reshape(-1, N_EXPERTS)
    run_start = pad_start[None, :] + base
    run_len = jnp.concatenate([base[1:], counts[None, :]], axis=0) - base
    seg_a = run_start // CHUNK * CHUNK
    seg_nch = jnp.where(run_len > 0, (run_start + run_len - seg_a + CHUNK - 1) // CHUNK, 0)
    buf_row0 = (jnp.cumsum(seg_nch, axis=1) - seg_nch) * CHUNK
    shift = jnp.repeat(buf_row0 - seg_a, COMBINE_TILE, axis=0)
    col = dest + jnp.sum(jnp.where(top_idx[..., None] == experts, shift[None], 0), axis=-1)
    chunk_end = jnp.cumsum(seg_nch, axis=1)
    c = jnp.arange(COMBINE_CHUNKS, dtype=jnp.int32)
    owner = c[None, :, None] >= chunk_end[:, None, :]
    owner_e = jnp.minimum(jnp.sum(owner.astype(jnp.int32), axis=-1), N_EXPERTS - 1)
    pick = owner_e[..., None] == experts
    first_chunk = jnp.sum(jnp.where(pick, (chunk_end - seg_nch)[:, None, :], 0), axis=-1)
    run_a = jnp.sum(jnp.where(pick, seg_a[:, None, :], 0), axis=-1)
    total = chunk_end[:, -1]
    chunk_src = jnp.where(c[None, :] < total[:, None], run_a + (c[None, :] - first_chunk) * CHUNK, 0)
    flat = lambda a: a.reshape(-1).astype(jnp.int32)
    return (dest, pad_end, padded.astype(jnp.int32), n_used.reshape(1).astype(jnp.int32),
            flat(chunk_src), flat((total + CHUNK_BATCH - 1) // CHUNK_BATCH), col.astype(jnp.int32))


def _rope_tables(seq):
    half = HEAD_DIM // 2
    inv_freq = ROPE_THETA ** (-jnp.arange(half, dtype=F32) / half)
    ang = jnp.arange(seq, dtype=F32)[:, None] * inv_freq[None, :]
    cos, sin = jnp.cos(ang), jnp.sin(ang)
    reps = LANES // HEAD_DIM
    cos_l = jnp.tile(jnp.concatenate([cos, cos], axis=1), (1, reps))
    sin_l = jnp.tile(jnp.concatenate([-sin, sin], axis=1), (1, reps))
    return cos_l, sin_l


def kernel(x, mem, norm1_g, w_in, pool_w, pool_scale, mem_norm_g, w_mem_kv, grp_norm_g, w_out, norm2_g,
           router_w, router_b, w_gu, b_gu, w_down, b_down, final_g):
    batch, seq, d = x.shape
    depth = w_in.shape[0]
    n_mem = mem.shape[1]
    t = batch * seq
    n_blocks = t * TOP_K // MOE_BLOCK + N_EXPERTS
    cos_l, sin_l = _rope_tables(seq)
    x2 = x.reshape(t, d)
    mem2 = mem.reshape(batch * n_mem, d)
    row = lambda a: a.reshape(1, -1)
    for l in range(depth):
        qkv, u, qm = _in_proj(x2, row(norm1_g[l]), w_in[l].astype(BF16), cos_l, sin_l, batch, seq)
        attn = [_dilated_attention(*qkv[n]) for n in range(len(DILATIONS))]
        kv = _mem_kv(mem2, row(mem_norm_g[l]), w_mem_kv[l].astype(BF16)).reshape(batch, n_mem, 2 * MEM_WIDTH)
        pw_bd = jax.scipy.linalg.block_diag(*[pool_w[l, g] for g in range(len(POOL_WINDOWS))]).astype(BF16)
        rw_t = router_w[l].T
        rw_hi = rw_t.astype(BF16)
        rw_lo = (rw_t - rw_hi.astype(F32)).astype(BF16)
        x1, h2_tiles, top_idx, gates, rank, counts, tile_base = _mix_out(
            attn, u, qm, kv, x2, pw_bd, row(pool_scale[l]), row(grp_norm_g[l]), w_out[l].astype(BF16),
            row(norm2_g[l]), rw_hi, rw_lo, router_b[l].reshape(-1, 1), batch, seq)
        dest, pad_end, padded, n_used, chunk_src, tile_batches, col = _slot_layout(
            top_idx, rank, counts, tile_base)
        xs_tiles = _dispatch(dest.reshape(-1), pad_end, padded, n_used, h2_tiles, n_blocks * MOE_BLOCK)
        y_slots = _experts(l, (pad_end - padded) // MOE_BLOCK, padded // MOE_BLOCK, n_used, xs_tiles,
                           w_gu, b_gu, w_down, b_down)
        x2 = _combine(chunk_src, tile_batches, y_slots, x1, col.T, gates.T, row(final_g),
                      final=(l == depth - 1))
    return x2.reshape(batch, seq, d)
```

```python
import functools

import jax
import jax.numpy as jnp
from jax import lax
from jax.experimental import pallas as pl
from jax.experimental.pallas import tpu as pltpu

D_MODEL = 1024
HEAD_DIM = 64
ATTN_WIDTH = 512
DILATIONS = (1, 4, 16)
BAND = 64
ROPE_THETA = 10000.0
POOL_WINDOWS = (2, 4, 8, 16)
POOL_WIDTH = 256
POOL_GROUP = 64
POOL_HALO = 8
MEM_WIDTH = 256
N_EXPERTS = 32
TOP_K = 4
D_EXPERT = 1024
SWIGLU_ALPHA = 1.702
SWIGLU_LIMIT = 7.0
MOE_BLOCK = 256
IN_PROJ_TILE = 512
ATTN_TILE = 512
ATTN_SUBTILE = 128
MEM_KV_TILE = 256
ROUTER_TILE = 512
DISPATCH_TILE = 1024
COMBINE_TILE = 256
NORM_EPS = 1e-5
NEG_INF = -1e30
LANES = 128
SUBLANES = 8
ROW_TILE = D_MODEL // LANES

F32 = jnp.float32
BF16 = jnp.bfloat16
VMEM_LIMIT = 56 * 1024 * 1024

_NT = (((1,), (1,)), ((), ()))


def _params(sem, vmem=VMEM_LIMIT):
    return pltpu.CompilerParams(dimension_semantics=sem, vmem_limit_bytes=vmem)


def _rms(x, g):
    return x * lax.rsqrt(jnp.mean(x * x, axis=-1, keepdims=True) + NORM_EPS) * g


def _to_row_tiles(ref, val):
    m = val.shape[0]
    for s in range(ROW_TILE):
        ref[pl.ds(s, m, stride=ROW_TILE), :] = val[:, s * LANES:(s + 1) * LANES]


def _from_row_tiles(ref, start, m):
    return jnp.concatenate([ref[pl.ds(start * ROW_TILE + s, m, stride=ROW_TILE), :] for s in range(ROW_TILE)],
                           axis=1)


def _in_proj_kernel(x_ref, g_ref, w_ref, cos_ref, sin_ref, q1, k1, v1, q4, k4, v4, q16, k16, v16, u_ref, qm_ref,
                    qkv, cls4, *, tm):
    h = _rms(x_ref[...], g_ref[...]).astype(BF16)
    proj = jnp.dot(h, w_ref[...], preferred_element_type=F32)
    cos = cos_ref[...]
    sin = sin_ref[...]
    lane = lax.broadcasted_iota(jnp.int32, cos.shape, 1)
    first_half = (lane % HEAD_DIM) < (HEAD_DIM // 2)
    scale = HEAD_DIM ** -0.5

    def rope(t):
        partner = jnp.where(first_half, pltpu.roll(t, LANES - HEAD_DIM // 2, 1),
                            pltpu.roll(t, HEAD_DIM // 2, 1))
        return t * cos + partner * sin

    a = ATTN_WIDTH
    groups = a // LANES
    for c in range(groups):
        qkv[c] = rope(proj[:, c * LANES:(c + 1) * LANES]) * scale
        qkv[groups + c] = rope(proj[:, a + c * LANES:a + (c + 1) * LANES])
        qkv[2 * groups + c] = proj[:, 2 * a + c * LANES:2 * a + (c + 1) * LANES]
    u_ref[...] = proj[:, 3 * a:3 * a + POOL_WIDTH]
    qm_ref[...] = (proj[:, 3 * a + POOL_WIDTH:] * scale).astype(BF16)

    for n, (r1, r4, r16) in enumerate(((q1, q4, q16), (k1, k4, k16), (v1, v4, v16))):
        for c in range(groups):
            g = n * groups + c
            cols = slice(c * LANES, (c + 1) * LANES)
            r1[0, 0, :, cols] = qkv[g].astype(BF16)
            for a in range(4):
                rows = qkv[g, pl.ds(a, tm // 4, stride=4), :]
                r4[0, a, :, cols] = rows.astype(BF16)
                cls4[g * 4 + a] = rows
            for a in range(4):
                for b in range(4):
                    r16[0, a + 4 * b, :, cols] = cls4[g * 4 + a, pl.ds(b, tm // 16, stride=4), :].astype(BF16)


def _in_proj(x2, g, w_bf, cos, sin, batch, seq):
    t, d = x2.shape
    tm = IN_PROJ_TILE
    tiles_per_seq = seq // tm
    cols = w_bf.shape[1]
    row = lambda i: (i, 0)
    fixed = lambda i: (0, 0)
    cls = lambda i: (i // tiles_per_seq, 0, i % tiles_per_seq, 0)
    cls_specs, cls_shapes = [], []
    for dil in DILATIONS:
        for _ in range(3):
            cls_specs.append(pl.BlockSpec((1, dil, tm // dil, ATTN_WIDTH), cls))
            cls_shapes.append(jax.ShapeDtypeStruct((batch, dil, seq // dil, ATTN_WIDTH), BF16))
    outs = pl.pallas_call(
        functools.partial(_in_proj_kernel, tm=tm),
        grid=(t // tm,),
        in_specs=[pl.BlockSpec((tm, d), row),
                  pl.BlockSpec((1, d), fixed),
                  pl.BlockSpec((d, cols), fixed),
                  pl.BlockSpec((tm, LANES), lambda i: (i % tiles_per_seq, 0)),
                  pl.BlockSpec((tm, LANES), lambda i: (i % tiles_per_seq, 0))],
        out_specs=cls_specs + [pl.BlockSpec((tm, POOL_WIDTH), row), pl.BlockSpec((tm, MEM_WIDTH), row)],
        out_shape=cls_shapes + [jax.ShapeDtypeStruct((t, POOL_WIDTH), F32),
                                jax.ShapeDtypeStruct((t, MEM_WIDTH), BF16)],
        scratch_shapes=[pltpu.VMEM((3 * ATTN_WIDTH // LANES, tm, LANES), F32),
                        pltpu.VMEM((4 * 3 * ATTN_WIDTH // LANES, tm // 4, LANES), F32)],
        compiler_params=_params(("parallel",)),
        name="in_proj",
    )(x2, g, w_bf, cos, sin)
    qkv = [outs[3 * n:3 * n + 3] for n in range(len(DILATIONS))]
    return qkv, outs[-2], outs[-1]


def _head_pair_attention(q, k, v, valid):
    lane = lax.broadcasted_iota(jnp.int32, q.shape, 1)
    outs, lses = [], []
    for half in range(2):
        mine = (lane // HEAD_DIM) == half
        s = lax.dot_general(jnp.where(mine, q, jnp.zeros_like(q)), k, _NT,
                            preferred_element_type=F32)
        if valid is not None:
            s = jnp.where(valid, s, NEG_INF)
        m = jnp.max(s, axis=-1, keepdims=True)
        p = jnp.exp(s - m)
        den = jnp.sum(p, axis=-1, keepdims=True)
        pv = jnp.dot(p.astype(BF16), v, preferred_element_type=F32)
        outs.append(pv / den)
        lses.append(m + jnp.log(den))
    return jnp.where((lane // HEAD_DIM) == 0, outs[0], outs[1]), lses


def _attn_kernel(q_ref, kp_ref, kc_ref, kn_ref, vp_ref, vc_ref, vn_ref, o_ref, lse_ref, *, tq, sub, length):
    j = pl.program_id(2)
    nk = sub + 2 * BAND
    row = lax.broadcasted_iota(jnp.int32, (sub, nk), 0)
    col = lax.broadcasted_iota(jnp.int32, (sub, nk), 1)
    in_band = jnp.abs(col - BAND - row) <= BAND
    lane = lax.broadcasted_iota(jnp.int32, (sub, LANES), 1)
    for c in range(ATTN_WIDTH // LANES):
        sl = slice(c * LANES, (c + 1) * LANES)
        k = jnp.concatenate([kp_ref[0, 0, :, sl], kc_ref[0, 0, :, sl], kn_ref[0, 0, :, sl]], axis=0)
        v = jnp.concatenate([vp_ref[0, 0, :, sl], vc_ref[0, 0, :, sl], vn_ref[0, 0, :, sl]], axis=0)
        for s in range(tq // sub):
            rows = slice(s * sub, (s + 1) * sub)
            key = j * tq + s * sub - BAND + col
            valid = in_band & (key >= 0) & (key < length)
            o, lses = _head_pair_attention(q_ref[0, 0, rows, sl], k[s * sub:s * sub + nk], v[s * sub:s * sub + nk],
                                           valid)
            o_ref[0, 0, rows, sl] = o
            lse_tile = jnp.where(lane == 2 * c, lses[0], lses[1])
            if c == 0:
                lse_ref[0, 0, rows, :] = lse_tile
            else:
                lse_ref[0, 0, rows, :] = jnp.where((lane // 2) == c, lse_tile, lse_ref[0, 0, rows, :])


def _dilated_attention(q, k, v):
    batch, dilation, length, _ = q.shape
    tq = ATTN_TILE
    per = tq // BAND
    n_band_blocks = length // BAND
    cur = lambda b, r, j: (b, r, j, 0)
    prev = lambda b, r, j: (b, r, jnp.maximum(j * per - 1, 0), 0)
    nxt = lambda b, r, j: (b, r, jnp.minimum((j + 1) * per, n_band_blocks - 1), 0)
    big = lambda imap: pl.BlockSpec((1, 1, tq, ATTN_WIDTH), imap)
    halo = lambda imap: pl.BlockSpec((1, 1, BAND, ATTN_WIDTH), imap)
    return pl.pallas_call(
        functools.partial(_attn_kernel, tq=tq, sub=ATTN_SUBTILE, length=length),
        grid=(batch, dilation, length // tq),
        in_specs=[big(cur), halo(prev), big(cur), halo(nxt), halo(prev), big(cur), halo(nxt)],
        out_specs=[pl.BlockSpec((1, 1, tq, ATTN_WIDTH), cur), pl.BlockSpec((1, 1, tq, LANES), cur)],
        out_shape=[jax.ShapeDtypeStruct((batch, dilation, length, ATTN_WIDTH), F32),
                   jax.ShapeDtypeStruct((batch, dilation, length, LANES), F32)],
        compiler_params=_params(("parallel", "parallel", "parallel")),
        name=f"dilated_attn_d{dilation}",
    )(q, k, k, k, v, v, v)


def _mem_kv_kernel(m_ref, g_ref, w_ref, kv_ref):
    h = _rms(m_ref[...], g_ref[...]).astype(BF16)
    kv_ref[...] = jnp.dot(h, w_ref[...], preferred_element_type=F32).astype(BF16)


def _mem_kv(mem2, g, w_bf):
    n, d = mem2.shape
    cols = w_bf.shape[1]
    tm = MEM_KV_TILE
    return pl.pallas_call(
        _mem_kv_kernel,
        grid=(n // tm,),
        in_specs=[pl.BlockSpec((tm, d), lambda i: (i, 0)),
                  pl.BlockSpec((1, d), lambda i: (0, 0)),
                  pl.BlockSpec((d, cols), lambda i: (0, 0))],
        out_specs=pl.BlockSpec((tm, cols), lambda i: (i, 0)),
        out_shape=jax.ShapeDtypeStruct((n, cols), BF16),
        compiler_params=_params(("parallel",)),
        name="mem_kv",
    )(mem2, g, w_bf)


def _mix_out_kernel(o1_ref, o4_ref, o16_ref, l1_ref, l4_ref, l16_ref, up_ref, u_ref, un_ref, pinv_ref, qm_ref, kv_ref,
                    x_ref, pw_ref, ps_ref, gg_ref, wo_ref, n2_ref, rwh_ref, rwl_ref, rb_ref,
                    x1_ref, h2_ref, idx_ref, gate_ref, rank_ref, cnt_ref, base_ref,
                    o4_s, o16_s, l4_s, l16_s, cls4, carry, *, tm, seq):
    i = pl.program_id(0)
    tiles_per_seq = seq // tm
    pos0 = (i % tiles_per_seq) * tm

    for src4, src16, dst4, dst16 in ((o4_ref, o16_ref, o4_s, o16_s), (l4_ref, l16_ref, l4_s, l16_s)):
        for c in range(dst4.shape[0]):
            cols = slice(c * LANES, (c + 1) * LANES)
            for a in range(4):
                dst4[c, pl.ds(a, tm // 4, stride=4), :] = src4[0, a, :, cols]
                for b in range(4):
                    cls4[a, pl.ds(b, tm // 16, stride=4), :] = src16[0, a + 4 * b, :, cols]
            for a in range(4):
                dst16[c, pl.ds(a, tm // 4, stride=4), :] = cls4[a]

    l1, l2, l3 = l1_ref[0, 0], l4_s[0], l16_s[0]
    lm = jnp.maximum(jnp.maximum(l1, l2), l3)
    e1, e2, e3 = jnp.exp(l1 - lm), jnp.exp(l2 - lm), jnp.exp(l3 - lm)
    es = e1 + e2 + e3
    spread = (lax.broadcasted_iota(jnp.int32, (LANES, ATTN_WIDTH), 1) // HEAD_DIM
              == lax.broadcasted_iota(jnp.int32, (LANES, ATTN_WIDTH), 0)).astype(BF16)

    def per_head(e):
        w = e / es
        hi = w.astype(BF16)
        lo = (w - hi.astype(F32)).astype(BF16)
        return (jnp.dot(hi, spread, preferred_element_type=F32)
                + jnp.dot(lo, spread, preferred_element_type=F32))

    w1, w2, w3 = per_head(e1), per_head(e2), per_head(e3)
    ya = []
    for c in range(ATTN_WIDTH // LANES):
        sl = slice(c * LANES, (c + 1) * LANES)
        ya.append(w1[:, sl] * o1_ref[0, 0, :, sl] + w2[:, sl] * o4_s[c] + w3[:, sl] * o16_s[c])
    ya = jnp.concatenate(ya, axis=1)

    u = u_ref[...]
    before = jnp.where(pos0 > 0, up_ref[...], 0.0)
    after = jnp.where(pos0 + tm < seq, un_ref[...], 0.0)
    ext = jnp.concatenate([before, u, after], axis=0)
    n_ext = tm + 2 * POOL_HALO

    def doubled(acc, shift):
        return pltpu.roll(acc, shift, 0) + pltpu.roll(acc, n_ext - shift, 0)

    first_group = lax.broadcasted_iota(jnp.int32, (n_ext, LANES), 1) < POOL_GROUP
    halves = []
    for half in range(POOL_WIDTH // LANES):
        e = ext[:, half * LANES:(half + 1) * LANES]
        s2 = pltpu.roll(e, 1, 0) + e
        s4 = doubled(s2, 1)
        if half == 0:
            halves.append(jnp.where(first_group, s2, s4))
        else:
            s8 = doubled(s4, 2)
            halves.append(jnp.where(first_group, s8, doubled(s8, 4)))
    window_sum = jnp.concatenate(halves, axis=1)[POOL_HALO:POOL_HALO + tm]
    d = (window_sum * pinv_ref[...] - u).astype(BF16)
    yp = jnp.dot(d, pw_ref[...], preferred_element_type=F32) * ps_ref[...]

    ym = []
    for c in range(MEM_WIDTH // LANES):
        sl = slice(c * LANES, (c + 1) * LANES)
        o, _ = _head_pair_attention(qm_ref[:, sl], kv_ref[0, :, sl],
                                    kv_ref[0, :, MEM_WIDTH + c * LANES:MEM_WIDTH + (c + 1) * LANES], None)
        ym.append(o)
    ym = jnp.concatenate(ym, axis=1)

    gg = gg_ref[...]
    a, p = ATTN_WIDTH, POOL_WIDTH
    y = jnp.concatenate([_rms(ya, gg[:, :a]), _rms(yp, gg[:, a:a + p]), _rms(ym, gg[:, a + p:])], axis=1)
    x1 = x_ref[...] + jnp.dot(y.astype(BF16), wo_ref[...], preferred_element_type=F32)
    x1_ref[...] = x1

    h2 = _rms(x1, n2_ref[...])
    _to_row_tiles(h2_ref, h2)

    hi = h2.astype(BF16)
    lo = (h2 - hi.astype(F32)).astype(BF16)
    logits = (lax.dot_general(rwh_ref[...], hi, _NT, preferred_element_type=F32)
              + lax.dot_general(rwh_ref[...], lo, _NT, preferred_element_type=F32)
              + lax.dot_general(rwl_ref[...], hi, _NT, preferred_element_type=F32)) + rb_ref[...]
    eidx = lax.broadcasted_iota(jnp.int32, logits.shape, 0)
    krow = lax.broadcasted_iota(jnp.int32, (TOP_K, tm), 0)
    vals = jnp.zeros((TOP_K, tm), F32)
    idxs = jnp.zeros((TOP_K, tm), jnp.int32)
    work = logits
    args = []
    for kk in range(TOP_K):
        best = jnp.max(work, axis=0, keepdims=True)
        arg = jnp.min(jnp.where(work == best, eidx, N_EXPERTS), axis=0, keepdims=True)
        vals = jnp.where(krow == kk, best, vals)
        idxs = jnp.where(krow == kk, arg, idxs)
        work = jnp.where(eidx == arg, -jnp.inf, work)
        args.append(arg)
    ex = jnp.exp(vals - vals[0:1])
    gate_ref[...] = ex / jnp.sum(ex, axis=0, keepdims=True)
    idx_ref[...] = idxs

    @pl.when(i == 0)
    def _():
        carry[...] = jnp.zeros_like(carry)

    chosen = (work == -jnp.inf).astype(BF16)
    earlier = (lax.broadcasted_iota(jnp.int32, (tm, tm), 0)
               < lax.broadcasted_iota(jnp.int32, (tm, tm), 1)).astype(BF16)
    before_me = jnp.dot(chosen, earlier, preferred_element_type=F32) + carry[...]
    ranks = jnp.zeros((TOP_K, tm), F32)
    for kk in range(TOP_K):
        ranks = jnp.where(krow == kk, jnp.sum(jnp.where(eidx == args[kk], before_me, 0.0), axis=0, keepdims=True),
                          ranks)
    rank_ref[...] = ranks.astype(jnp.int32)
    for part in range(tm // COMBINE_TILE):
        base_ref[part] = before_me[:, part * COMBINE_TILE:part * COMBINE_TILE + 1].astype(jnp.int32)
    carry[...] = carry[...] + jnp.sum(chosen.astype(F32), axis=1, keepdims=True)
    cnt_ref[...] = carry[...].astype(jnp.int32)


def _mix_out(attn, u, pool_inv, qm, kv, x2, pw_bd, ps, gg, wo_bf, n2, rw_hi, rw_lo, rb, batch, seq):
    t, d = x2.shape
    tm = ROUTER_TILE
    hp = tm // POOL_HALO
    n_halo = t // POOL_HALO
    tiles_per_seq = seq // tm
    row = lambda i: (i, 0)
    fixed = lambda i: (0, 0)
    cls = lambda i: (i // tiles_per_seq, 0, i % tiles_per_seq, 0)
    rowspec = lambda w: pl.BlockSpec((tm, w), row)
    clsspec = lambda dil, w: pl.BlockSpec((1, dil, tm // dil, w), cls)
    tok_cols = lambda i: (0, i)
    (o1, l1), (o4, l4), (o16, l16) = attn
    return pl.pallas_call(
        functools.partial(_mix_out_kernel, tm=tm, seq=seq),
        grid=(t // tm,),
        in_specs=[clsspec(1, ATTN_WIDTH), clsspec(4, ATTN_WIDTH), clsspec(16, ATTN_WIDTH),
                  clsspec(1, LANES), clsspec(4, LANES), clsspec(16, LANES),
                  pl.BlockSpec((POOL_HALO, POOL_WIDTH), lambda i: (jnp.maximum(i * hp - 1, 0), 0)),
                  rowspec(POOL_WIDTH),
                  pl.BlockSpec((POOL_HALO, POOL_WIDTH), lambda i: (jnp.minimum((i + 1) * hp, n_halo - 1), 0)),
                  pl.BlockSpec((tm, POOL_WIDTH), lambda i: (i % tiles_per_seq, 0)),
                  rowspec(MEM_WIDTH),
                  pl.BlockSpec((1, kv.shape[1], kv.shape[2]), lambda i: (i // tiles_per_seq, 0, 0)),
                  rowspec(d),
                  pl.BlockSpec(pw_bd.shape, fixed),
                  pl.BlockSpec(ps.shape, fixed),
                  pl.BlockSpec(gg.shape, fixed),
                  pl.BlockSpec(wo_bf.shape, fixed),
                  pl.BlockSpec(n2.shape, fixed),
                  pl.BlockSpec(rw_hi.shape, fixed),
                  pl.BlockSpec(rw_lo.shape, fixed),
                  pl.BlockSpec(rb.shape, fixed)],
        out_specs=[rowspec(d),
                   pl.BlockSpec((tm * ROW_TILE, LANES), row),
                   pl.BlockSpec((TOP_K, tm), tok_cols),
                   pl.BlockSpec((TOP_K, tm), tok_cols),
                   pl.BlockSpec((TOP_K, tm), tok_cols),
                   pl.BlockSpec((N_EXPERTS, 1), fixed),
                   pl.BlockSpec((tm // COMBINE_TILE, N_EXPERTS, 1), lambda i: (i, 0, 0))],
        out_shape=[jax.ShapeDtypeStruct((t, d), F32),
                   jax.ShapeDtypeStruct((t * ROW_TILE, LANES), F32),
                   jax.ShapeDtypeStruct((TOP_K, t), jnp.int32),
                   jax.ShapeDtypeStruct((TOP_K, t), F32),
                   jax.ShapeDtypeStruct((TOP_K, t), jnp.int32),
                   jax.ShapeDtypeStruct((N_EXPERTS, 1), jnp.int32),
                   jax.ShapeDtypeStruct((t // COMBINE_TILE, N_EXPERTS, 1), jnp.int32)],
        scratch_shapes=[pltpu.VMEM((ATTN_WIDTH // LANES, tm, LANES), F32),
                        pltpu.VMEM((ATTN_WIDTH // LANES, tm, LANES), F32),
                        pltpu.VMEM((1, tm, LANES), F32), pltpu.VMEM((1, tm, LANES), F32),
                        pltpu.VMEM((4, tm // 4, LANES), F32),
                        pltpu.VMEM((N_EXPERTS, 1), F32)],
        compiler_params=_params(("arbitrary",)),
        name="mix_out_router",
    )(o1, o4, o16, l1, l4, l16, u, u, u, pool_inv, qm, kv, x2, pw_bd, ps, gg, wo_bf, n2, rw_hi, rw_lo, rb)


def _dispatch_kernel(dest_ref, pend_ref, padded_ref, nb_ref, h_ref, xs_hbm, zeros, sem, *, tm, n_tok, n_blocks):
    i = pl.program_id(0)
    blk = MOE_BLOCK * ROW_TILE

    @pl.when(i == 0)
    def _():
        zeros[...] = jnp.zeros_like(zeros)

        def clear(block):
            return pltpu.make_async_copy(zeros, xs_hbm.at[pl.ds(pl.multiple_of(block * blk, blk), blk), :], sem)

        def for_each_cleared_block(fn):
            for e in range(N_EXPERTS):
                @pl.when(padded_ref[e] > 0)
                def _():
                    fn(clear(pend_ref[e] // MOE_BLOCK - 1))

                @pl.when(nb_ref[0] + e < n_blocks)
                def _():
                    fn(clear(nb_ref[0] + e))

        for_each_cleared_block(lambda c: c.start())
        for_each_cleared_block(lambda c: c.wait())

    def start(r, c):
        src = h_ref.at[pl.ds(pl.multiple_of(r * ROW_TILE, ROW_TILE), ROW_TILE), :]
        for kk in range(TOP_K):
            slot = dest_ref[kk * n_tok + i * tm + r]
            pltpu.async_copy(src, xs_hbm.at[pl.ds(pl.multiple_of(slot * ROW_TILE, ROW_TILE), ROW_TILE), :],
                             sem, priority=kk % 2)
        return c

    lax.fori_loop(0, tm, start, 0, unroll=4)
    n = tm * TOP_K * ROW_TILE
    pltpu.make_async_copy(xs_hbm.at[pl.ds(0, n), :], xs_hbm.at[pl.ds(0, n), :], sem).wait()


def _dispatch(dest, pad_end, padded, n_used, h2_tiles, n_slots):
    n_tok = h2_tiles.shape[0] // ROW_TILE
    tm = DISPATCH_TILE
    return pl.pallas_call(
        functools.partial(_dispatch_kernel, tm=tm, n_tok=n_tok, n_blocks=n_slots // MOE_BLOCK),
        grid_spec=pltpu.PrefetchScalarGridSpec(
            num_scalar_prefetch=4,
            grid=(n_tok // tm,),
            in_specs=[pl.BlockSpec((tm * ROW_TILE, LANES), lambda i, *_: (i, 0))],
            out_specs=pl.BlockSpec(memory_space=pl.ANY),
            scratch_shapes=[pltpu.VMEM((MOE_BLOCK * ROW_TILE, LANES), F32), pltpu.SemaphoreType.DMA(())]),
        out_shape=jax.ShapeDtypeStruct((n_slots * ROW_TILE, LANES), F32),
        compiler_params=_params(("arbitrary",)),
        name="moe_dispatch",
    )(dest, pad_end, padded, n_used, h2_tiles)


EXPERT_GROUP = 2


WEIGHT_DMA_QUEUE = 1


def _expert_kernel(first_ref, nblk_ref, nb_ref, xs_hbm, wgu_hbm, bgu_ref, wdn_hbm, bdn_ref, y_hbm,
                   wgu_f32, wdn_f32, wgu_bf, wdn_bf, xbuf, ybuf, xtail, ytail, zeros, sem_w, sem_in, sem_out,
                   *, layer, n_blocks):
    e = pl.program_id(0)
    first = first_ref[e]
    n = nblk_ref[e]
    rows = MOE_BLOCK * ROW_TILE
    n_groups = n // EXPERT_GROUP
    tail = first + n_groups * EXPERT_GROUP

    def load(block, count, dst, sem):
        return pltpu.make_async_copy(xs_hbm.at[pl.ds(pl.multiple_of(block * rows, rows), count * rows), :], dst, sem)

    def store(block, count, src, sem):
        return pltpu.make_async_copy(src, y_hbm.at[pl.ds(pl.multiple_of(block * MOE_BLOCK, MOE_BLOCK),
                                                         count * MOE_BLOCK), :], sem)

    def group_load(j, s):
        return load(first + j * EXPERT_GROUP, EXPERT_GROUP, xbuf.at[s], sem_in.at[s])

    def group_store(j, s):
        return store(first + j * EXPERT_GROUP, EXPERT_GROUP, ybuf.at[s], sem_out.at[s])

    def ffn(x_tiles, m):
        x = _from_row_tiles(x_tiles, 0, m).astype(BF16)
        gu = jnp.dot(x, wgu_bf[...], preferred_element_type=F32) + bgu_ref[0, 0]
        gate = jnp.minimum(gu[:, :D_EXPERT], SWIGLU_LIMIT)
        up = jnp.clip(gu[:, D_EXPERT:], -SWIGLU_LIMIT, SWIGLU_LIMIT)
        act = (up + 1.0) * gate * jax.nn.sigmoid(SWIGLU_ALPHA * gate)
        return (jnp.dot(act.astype(BF16), wdn_bf[...], preferred_element_type=F32) + bdn_ref[0, 0]).astype(BF16)

    def weights(ex, fn):
        s = ex % 2
        fn(wgu_hbm.at[layer, ex], wgu_f32.at[s], sem_w.at[0, s])
        fn(wdn_hbm.at[layer, ex], wdn_f32.at[s], sem_w.at[1, s])

    def w_start(src, dst, sem):
        pltpu.async_copy(src, dst, sem, priority=WEIGHT_DMA_QUEUE)

    def w_wait(src, dst, sem):
        pltpu.make_async_copy(src, dst, sem).wait()

    @pl.when(e == 0)
    def _():
        weights(0, w_start)

    has_tail = n_groups * EXPERT_GROUP < n

    @pl.when(n_groups > 0)
    def _():
        group_load(0, 0).start()

    @pl.when(has_tail)
    def _():
        load(tail, 1, xtail, sem_in.at[2]).start()

    @pl.when(e + 1 < N_EXPERTS)
    def _():
        weights(e + 1, w_start)

    weights(e, w_wait)

    @pl.when(n > 0)
    def _():
        wgu_bf[...] = wgu_f32[e % 2].astype(BF16)
        wdn_bf[...] = wdn_f32[e % 2].astype(BF16)

        def group(j, carry):
            s = j % 2
            group_load(j, s).wait()

            @pl.when(j + 1 < n_groups)
            def _():
                group_load(j + 1, 1 - s).start()

            y = ffn(xbuf.at[s], EXPERT_GROUP * MOE_BLOCK)

            @pl.when(j >= 2)
            def _():
                group_store(j - 2, s).wait()

            ybuf[s] = y
            group_store(j, s).start()
            return carry

        lax.fori_loop(0, n_groups, group, 0)

        @pl.when(has_tail)
        def _():
            load(tail, 1, xtail, sem_in.at[2]).wait()
            ytail[...] = ffn(xtail, MOE_BLOCK)
            store(tail, 1, ytail, sem_out.at[2]).start()

        @pl.when(n_groups >= 2)
        def _():
            group_store(n_groups - 2, n_groups % 2).wait()

        @pl.when(n_groups >= 1)
        def _():
            group_store(n_groups - 1, (n_groups - 1) % 2).wait()

        @pl.when(has_tail)
        def _():
            store(tail, 1, ytail, sem_out.at[2]).wait()

    @pl.when(e == N_EXPERTS - 1)
    def _():
        zeros[...] = jnp.zeros_like(zeros)

        def clear(k):
            blk = pl.multiple_of((nb_ref[0] + k) * MOE_BLOCK, MOE_BLOCK)
            return pltpu.make_async_copy(zeros, y_hbm.at[pl.ds(blk, MOE_BLOCK), :], sem_out.at[0])

        for k in range(N_EXPERTS):
            @pl.when(nb_ref[0] + k < n_blocks)
            def _():
                clear(k).start()
        for k in range(N_EXPERTS):
            @pl.when(nb_ref[0] + k < n_blocks)
            def _():
                clear(k).wait()


def _experts(layer, first_block, n_block, n_used, xs_tiles, w_gu, b_gu, w_down, b_down):
    n_slots = xs_tiles.shape[0] // ROW_TILE
    d = w_gu.shape[2]
    de2 = w_gu.shape[3]
    by_expert = lambda e, *_: (layer, e, 0, 0)
    return pl.pallas_call(
        functools.partial(_expert_kernel, layer=layer, n_blocks=n_slots // MOE_BLOCK),
        grid_spec=pltpu.PrefetchScalarGridSpec(
            num_scalar_prefetch=3,
            grid=(N_EXPERTS,),
            in_specs=[pl.BlockSpec(memory_space=pl.ANY),
                      pl.BlockSpec(memory_space=pl.ANY),
                      pl.BlockSpec((1, 1, 1, de2), by_expert),
                      pl.BlockSpec(memory_space=pl.ANY),
                      pl.BlockSpec((1, 1, 1, d), by_expert)],
            out_specs=pl.BlockSpec(memory_space=pl.ANY),
            scratch_shapes=[pltpu.VMEM((2, d, de2), F32), pltpu.VMEM((2, D_EXPERT, d), F32),
                            pltpu.VMEM((d, de2), BF16), pltpu.VMEM((D_EXPERT, d), BF16),
                            pltpu.VMEM((2, EXPERT_GROUP * MOE_BLOCK * ROW_TILE, LANES), F32),
                            pltpu.VMEM((2, EXPERT_GROUP * MOE_BLOCK, d), BF16),
                            pltpu.VMEM((MOE_BLOCK * ROW_TILE, LANES), F32),
                            pltpu.VMEM((MOE_BLOCK, d), BF16),
                            pltpu.VMEM((MOE_BLOCK, d), BF16),
                            pltpu.SemaphoreType.DMA((2, 2)),
                            pltpu.SemaphoreType.DMA((3,)), pltpu.SemaphoreType.DMA((3,))]),
        out_shape=jax.ShapeDtypeStruct((n_slots, d), BF16),
        compiler_params=_params(("arbitrary",)),
        name="moe_experts",
    )(first_block, n_block, n_used, xs_tiles, w_gu, b_gu.reshape(b_gu.shape[0], N_EXPERTS, 1, de2), w_down,
      b_down.reshape(b_down.shape[0], N_EXPERTS, 1, d))


CHUNK = 16
CHUNK_BATCH = 16
COMBINE_ROWS = COMBINE_TILE * TOP_K + 2 * N_EXPERTS * CHUNK
COMBINE_CHUNKS = COMBINE_ROWS // CHUNK


def _combine_kernel(src_ref, nbatch_ref, y_hbm, x1_ref, col_ref, gate_ref, g_ref, out_ref, ybuf, sems,
                    *, tm, n_tiles, final):
    i = pl.program_id(0)
    slot = i % 2
    batch_rows = CHUNK_BATCH * CHUNK

    @pl.when(i == 0)
    def _():
        ybuf[...] = jnp.zeros_like(ybuf)

    def fetch(tile, buf):
        def batch(b, carry):
            for u in range(CHUNK_BATCH):
                c = b * CHUNK_BATCH + u
                src = src_ref[tile * COMBINE_CHUNKS + c]
                pltpu.make_async_copy(y_hbm.at[pl.ds(pl.multiple_of(src, CHUNK), CHUNK), :],
                                      ybuf.at[buf, pl.ds(pl.multiple_of(c * CHUNK, CHUNK), CHUNK), :],
                                      sems.at[buf]).start()
            return carry

        lax.fori_loop(0, nbatch_ref[tile], batch, 0)

    @pl.when(i == 0)
    def _():
        fetch(0, 0)

    @pl.when(i + 1 < n_tiles)
    def _():
        fetch(i + 1, 1 - slot)

    col = col_ref[...]
    gates = gate_ref[...]
    col_id = lax.broadcasted_iota(jnp.int32, (tm, COMBINE_ROWS), 1)
    g = jnp.zeros((tm, COMBINE_ROWS), F32)
    for kk in range(TOP_K):
        g = jnp.where(col_id == col[:, kk:kk + 1], gates[:, kk:kk + 1], g)

    def wait_batch(b, carry):
        pltpu.make_async_copy(y_hbm.at[pl.ds(0, batch_rows), :], ybuf.at[slot, pl.ds(0, batch_rows), :],
                              sems.at[slot]).wait()
        return carry

    lax.fori_loop(0, nbatch_ref[i], wait_batch, 0)
    x = x1_ref[...] + jnp.dot(g.astype(BF16), ybuf[slot], preferred_element_type=F32)
    out_ref[...] = _rms(x, g_ref[...]) if final else x


def _combine(chunk_src, tile_batches, y_slots, x1, col_tk, gates_tk, final_g, final):
    t, d = x1.shape
    tm = COMBINE_TILE
    n_tiles = t // tm
    tok = lambda i, *_: (i, 0)
    return pl.pallas_call(
        functools.partial(_combine_kernel, tm=tm, n_tiles=n_tiles, final=final),
        grid_spec=pltpu.PrefetchScalarGridSpec(
            num_scalar_prefetch=2,
            grid=(n_tiles,),
            in_specs=[pl.BlockSpec(memory_space=pl.ANY),
                      pl.BlockSpec((tm, d), tok),
                      pl.BlockSpec((tm, TOP_K), tok),
                      pl.BlockSpec((tm, TOP_K), tok),
                      pl.BlockSpec((1, d), lambda i, *_: (0, 0))],
            out_specs=pl.BlockSpec((tm, d), tok),
            scratch_shapes=[pltpu.VMEM((2, COMBINE_ROWS, d), BF16), pltpu.SemaphoreType.DMA((2,))]),
        out_shape=jax.ShapeDtypeStruct((t, d), F32),
        compiler_params=_params(("arbitrary",)),
        name="moe_combine",
    )(chunk_src, tile_batches, y_slots, x1, col_tk, gates_tk, final_g)


def _slot_layout(top_idx, rank, counts, tile_base):
    counts = counts.reshape(-1)
    padded = (counts + MOE_BLOCK - 1) // MOE_BLOCK * MOE_BLOCK
    pad_end = jnp.cumsum(padded).astype(jnp.int32)
    pad_start = pad_end - padded
    experts = jnp.arange(N_EXPERTS, dtype=jnp.int32)
    start_of = jnp.sum(jnp.where(top_idx[..., None] == experts, pad_start, 0), axis=-1)
    dest = (start_of + rank).astype(jnp.int32)
    n_used = pad_end[-1] // MOE_BLOCK
    base = tile_base.reshape(-1, N_EXPERTS)
    run_start = pad_start[None, :] + base
    run_len = jnp.concatenate([base[1:], counts[None, :]], axis=0) - base
    seg_a = run_start // CHUNK * CHUNK
    seg_nch = jnp.where(run_len > 0, (run_start + run_len - seg_a + CHUNK - 1) // CHUNK, 0)
    buf_row0 = (jnp.cumsum(seg_nch, axis=1) - seg_nch) * CHUNK
    shift = jnp.repeat(buf_row0 - seg_a, COMBINE_TILE, axis=0)
    col = dest + jnp.sum(jnp.where(top_idx[..., None] == experts, shift[None], 0), axis=-1)
    chunk_end = jnp.cumsum(seg_nch, axis=1)
    c = jnp.arange(COMBINE_CHUNKS, dtype=jnp.int32)
    owner = c[None, :, None] >= chunk_end[:, None, :]
    owner_e = jnp.minimum(jnp.sum(owner.astype(jnp.int32), axis=-1), N_EXPERTS - 1)
    pick = owner_e[..., None] == experts
    first_chunk = jnp.sum(jnp.where(pick, (chunk_end - seg_nch)[:, None, :], 0), axis=-1)
    run_a = jnp.sum(jnp.where(pick, seg_a[:, None, :], 0), axis=-1)
    total = chunk_end[:, -1]
    chunk_src = jnp.where(c[None, :] < total[:, None], run_a + (c[None, :] - first_chunk) * CHUNK, 0)
    flat = lambda a: a.reshape(-1).astype(jnp.int32)
    return (dest, pad_end, padded.astype(jnp.int32), n_used.reshape(1).astype(jnp.int32),
            flat(chunk_src), flat((total + CHUNK_BATCH - 1) // CHUNK_BATCH), col.astype(jnp.int32))


def _rope_tables(seq):
    half = HEAD_DIM // 2
    inv_freq = ROPE_THETA ** (-jnp.arange(half, dtype=F32) / half)
    ang = jnp.arange(seq, dtype=F32)[:, None] * inv_freq[None, :]
    cos, sin = jnp.cos(ang), jnp.sin(ang)
    reps = LANES // HEAD_DIM
    cos_l = jnp.tile(jnp.concatenate([cos, cos], axis=1), (1, reps))
    sin_l = jnp.tile(jnp.concatenate([-sin, sin], axis=1), (1, reps))
    return cos_l, sin_l


def _pool_inverse_counts(seq):
    pos = jnp.arange(seq, dtype=jnp.int32)[:, None]
    cols = []
    for w in POOL_WINDOWS:
        cnt = jnp.minimum(pos + (w - 1 - w // 2), seq - 1) + 1 - jnp.maximum(pos - w // 2, 0)
        cols.append(jnp.broadcast_to(1.0 / cnt.astype(F32), (seq, POOL_GROUP)))
    return jnp.concatenate(cols, axis=1)


def kernel(x, mem, norm1_g, w_in, pool_w, pool_scale, mem_norm_g, w_mem_kv, grp_norm_g, w_out, norm2_g,
           router_w, router_b, w_gu, b_gu, w_down, b_down, final_g):
    batch, seq, d = x.shape
    depth = w_in.shape[0]
    n_mem = mem.shape[1]
    t = batch * seq
    n_blocks = t * TOP_K // MOE_BLOCK + N_EXPERTS
    cos_l, sin_l = _rope_tables(seq)
    pool_inv = _pool_inverse_counts(seq)
    x2 = x.reshape(t, d)
    mem2 = mem.reshape(batch * n_mem, d)
    row = lambda a: a.reshape(1, -1)
    for l in range(depth):
        qkv, u, qm = _in_proj(x2, row(norm1_g[l]), w_in[l].astype(BF16), cos_l, sin_l, batch, seq)
        attn = [_dilated_attention(*qkv[n]) for n in range(len(DILATIONS))]
        kv = _mem_kv(mem2, row(mem_norm_g[l]), w_mem_kv[l].astype(BF16)).reshape(batch, n_mem, 2 * MEM_WIDTH)
        pw_bd = jax.scipy.linalg.block_diag(*[pool_w[l, g] for g in range(len(POOL_WINDOWS))]).astype(BF16)
        rw_t = router_w[l].T
        rw_hi = rw_t.astype(BF16)
        rw_lo = (rw_t - rw_hi.astype(F32)).astype(BF16)
        x1, h2_tiles, top_idx, gates, rank, counts, tile_base = _mix_out(
            attn, u, pool_inv, qm, kv, x2, pw_bd, row(pool_scale[l]), row(grp_norm_g[l]), w_out[l].astype(BF16),
            row(norm2_g[l]), rw_hi, rw_lo, router_b[l].reshape(-1, 1), batch, seq)
        dest, pad_end, padded, n_used, chunk_src, tile_batches, col = _slot_layout(
            top_idx, rank, counts, tile_base)
        xs_tiles = _dispatch(dest.reshape(-1), pad_end, padded, n_used, h2_tiles, n_blocks * MOE_BLOCK)
        y_slots = _experts(l, (pad_end - padded) // MOE_BLOCK, padded // MOE_BLOCK, n_used, xs_tiles,
                           w_gu, b_gu, w_down, b_down)
        x2 = _combine(chunk_src, tile_batches, y_slots, x1, col.T, gates.T, row(final_g),
                      final=(l == depth - 1))
    return x2.reshape(batch, seq, d)
```

```python
import functools

import jax
import jax.numpy as jnp
from jax import lax
from jax.experimental import pallas as pl
from jax.experimental.pallas import tpu as pltpu

D_MODEL = 1024
HEAD_DIM = 64
ATTN_WIDTH = 512
DILATIONS = (1, 4, 16)
BAND = 64
ROPE_THETA = 10000.0
POOL_WINDOWS = (2, 4, 8, 16)
POOL_WIDTH = 256
POOL_GROUP = 64
POOL_HALO = 8
MEM_WIDTH = 256
N_EXPERTS = 32
TOP_K = 4
D_EXPERT = 1024
SWIGLU_ALPHA = 1.702
SWIGLU_LIMIT = 7.0
MOE_BLOCK = 256
IN_PROJ_TILE = 512
ATTN_TILE = 512
ATTN_SUBTILE = 128
MEM_KV_TILE = 256
ROUTER_TILE = 512
DISPATCH_TILE = 1024
COMBINE_TILE = 256
NORM_EPS = 1e-5
NEG_INF = -1e30
LOG2_E = 1.4426950408889634
LANES = 128
SUBLANES = 8
ROW_TILE = D_MODEL // LANES

F32 = jnp.float32
BF16 = jnp.bfloat16
VMEM_LIMIT = 56 * 1024 * 1024

_NT = (((1,), (1,)), ((), ()))


def _params(sem, vmem=VMEM_LIMIT):
    return pltpu.CompilerParams(dimension_semantics=sem, vmem_limit_bytes=vmem)


def _rms(x, g):
    return x * lax.rsqrt(jnp.mean(x * x, axis=-1, keepdims=True) + NORM_EPS) * g


def _to_row_tiles(ref, val):
    m = val.shape[0]
    for s in range(ROW_TILE):
        ref[pl.ds(s, m, stride=ROW_TILE), :] = val[:, s * LANES:(s + 1) * LANES]


def _from_row_tiles(ref, start, m):
    return jnp.concatenate([ref[pl.ds(start * ROW_TILE + s, m, stride=ROW_TILE), :] for s in range(ROW_TILE)],
                           axis=1)


def _in_proj_kernel(x_ref, g_ref, w_ref, cos_ref, sin_ref, q1, k1, v1, q4, k4, v4, q16, k16, v16, u_ref, qm_ref,
                    qkv, cls4, *, tm):
    h = _rms(x_ref[...], g_ref[...]).astype(BF16)
    proj = jnp.dot(h, w_ref[...], preferred_element_type=F32)
    cos = cos_ref[...]
    sin = sin_ref[...]
    lane = lax.broadcasted_iota(jnp.int32, cos.shape, 1)
    first_half = (lane % HEAD_DIM) < (HEAD_DIM // 2)
    scale = HEAD_DIM ** -0.5 * LOG2_E

    def rope(t):
        partner = jnp.where(first_half, pltpu.roll(t, LANES - HEAD_DIM // 2, 1),
                            pltpu.roll(t, HEAD_DIM // 2, 1))
        return t * cos + partner * sin

    a = ATTN_WIDTH
    groups = a // LANES
    for c in range(groups):
        qkv[c] = rope(proj[:, c * LANES:(c + 1) * LANES]) * scale
        qkv[groups + c] = rope(proj[:, a + c * LANES:a + (c + 1) * LANES])
        qkv[2 * groups + c] = proj[:, 2 * a + c * LANES:2 * a + (c + 1) * LANES]
    u_ref[...] = proj[:, 3 * a:3 * a + POOL_WIDTH]
    qm_ref[...] = (proj[:, 3 * a + POOL_WIDTH:] * scale).astype(BF16)

    for n, (r1, r4, r16) in enumerate(((q1, q4, q16), (k1, k4, k16), (v1, v4, v16))):
        for c in range(groups):
            g = n * groups + c
            cols = slice(c * LANES, (c + 1) * LANES)
            r1[0, 0, :, cols] = qkv[g].astype(BF16)
            for a in range(4):
                rows = qkv[g, pl.ds(a, tm // 4, stride=4), :]
                r4[0, a, :, cols] = rows.astype(BF16)
                cls4[g * 4 + a] = rows
            for a in range(4):
                for b in range(4):
                    r16[0, a + 4 * b, :, cols] = cls4[g * 4 + a, pl.ds(b, tm // 16, stride=4), :].astype(BF16)


def _in_proj(x2, g, w_bf, cos, sin, batch, seq):
    t, d = x2.shape
    tm = IN_PROJ_TILE
    tiles_per_seq = seq // tm
    cols = w_bf.shape[1]
    row = lambda i: (i, 0)
    fixed = lambda i: (0, 0)
    cls = lambda i: (i // tiles_per_seq, 0, i % tiles_per_seq, 0)
    cls_specs, cls_shapes = [], []
    for dil in DILATIONS:
        for _ in range(3):
            cls_specs.append(pl.BlockSpec((1, dil, tm // dil, ATTN_WIDTH), cls))
            cls_shapes.append(jax.ShapeDtypeStruct((batch, dil, seq // dil, ATTN_WIDTH), BF16))
    outs = pl.pallas_call(
        functools.partial(_in_proj_kernel, tm=tm),
        grid=(t // tm,),
        in_specs=[pl.BlockSpec((tm, d), row),
                  pl.BlockSpec((1, d), fixed),
                  pl.BlockSpec((d, cols), fixed),
                  pl.BlockSpec((tm, LANES), lambda i: (i % tiles_per_seq, 0)),
                  pl.BlockSpec((tm, LANES), lambda i: (i % tiles_per_seq, 0))],
        out_specs=cls_specs + [pl.BlockSpec((tm, POOL_WIDTH), row), pl.BlockSpec((tm, MEM_WIDTH), row)],
        out_shape=cls_shapes + [jax.ShapeDtypeStruct((t, POOL_WIDTH), F32),
                                jax.ShapeDtypeStruct((t, MEM_WIDTH), BF16)],
        scratch_shapes=[pltpu.VMEM((3 * ATTN_WIDTH // LANES, tm, LANES), F32),
                        pltpu.VMEM((4 * 3 * ATTN_WIDTH // LANES, tm // 4, LANES), F32)],
        compiler_params=_params(("parallel",)),
        name="in_proj",
    )(x2, g, w_bf, cos, sin)
    qkv = [outs[3 * n:3 * n + 3] for n in range(len(DILATIONS))]
    return qkv, outs[-2], outs[-1]


def _head_pair_attention(q, k, v, valid):
    lane = lax.broadcasted_iota(jnp.int32, q.shape, 1)
    outs, lses = [], []
    for half in range(2):
        mine = (lane // HEAD_DIM) == half
        s = lax.dot_general(jnp.where(mine, q, jnp.zeros_like(q)), k, _NT,
                            preferred_element_type=F32)
        if valid is not None:
            s = jnp.where(valid, s, NEG_INF)
        m = jnp.max(s, axis=-1, keepdims=True)
        p = jnp.exp2(s - m)
        den = jnp.sum(p, axis=-1, keepdims=True)
        pv = jnp.dot(p.astype(BF16), v, preferred_element_type=F32)
        outs.append(pv / den)
        lses.append(m + jnp.log2(den))
    return jnp.where((lane // HEAD_DIM) == 0, outs[0], outs[1]), lses


def _attn_kernel(q_ref, kp_ref, kc_ref, kn_ref, vp_ref, vc_ref, vn_ref, o_ref, lse_ref, *, tq, sub, length):
    j = pl.program_id(2)
    nk = sub + 2 * BAND
    row = lax.broadcasted_iota(jnp.int32, (sub, nk), 0)
    col = lax.broadcasted_iota(jnp.int32, (sub, nk), 1)
    in_band = jnp.abs(col - BAND - row) <= BAND
    lane = lax.broadcasted_iota(jnp.int32, (sub, LANES), 1)
    for c in range(ATTN_WIDTH // LANES):
        sl = slice(c * LANES, (c + 1) * LANES)
        k = jnp.concatenate([kp_ref[0, 0, :, sl], kc_ref[0, 0, :, sl], kn_ref[0, 0, :, sl]], axis=0)
        v = jnp.concatenate([vp_ref[0, 0, :, sl], vc_ref[0, 0, :, sl], vn_ref[0, 0, :, sl]], axis=0)
        for s in range(tq // sub):
            rows = slice(s * sub, (s + 1) * sub)
            key = j * tq + s * sub - BAND + col
            valid = in_band & (key >= 0) & (key < length)
            o, lses = _head_pair_attention(q_ref[0, 0, rows, sl], k[s * sub:s * sub + nk], v[s * sub:s * sub + nk],
                                           valid)
            o_ref[0, 0, rows, sl] = o
            lse_tile = jnp.where(lane == 2 * c, lses[0], lses[1])
            if c == 0:
                lse_ref[0, 0, rows, :] = lse_tile
            else:
                lse_ref[0, 0, rows, :] = jnp.where((lane // 2) == c, lse_tile, lse_ref[0, 0, rows, :])


def _dilated_attention(q, k, v):
    batch, dilation, length, _ = q.shape
    tq = ATTN_TILE
    per = tq // BAND
    n_band_blocks = length // BAND
    cur = lambda b, r, j: (b, r, j, 0)
    prev = lambda b, r, j: (b, r, jnp.maximum(j * per - 1, 0), 0)
    nxt = lambda b, r, j: (b, r, jnp.minimum((j + 1) * per, n_band_blocks - 1), 0)
    big = lambda imap: pl.BlockSpec((1, 1, tq, ATTN_WIDTH), imap)
    halo = lambda imap: pl.BlockSpec((1, 1, BAND, ATTN_WIDTH), imap)
    return pl.pallas_call(
        functools.partial(_attn_kernel, tq=tq, sub=ATTN_SUBTILE, length=length),
        grid=(batch, dilation, length // tq),
        in_specs=[big(cur), halo(prev), big(cur), halo(nxt), halo(prev), big(cur), halo(nxt)],
        out_specs=[pl.BlockSpec((1, 1, tq, ATTN_WIDTH), cur), pl.BlockSpec((1, 1, tq, LANES), cur)],
        out_shape=[jax.ShapeDtypeStruct((batch, dilation, length, ATTN_WIDTH), F32),
                   jax.ShapeDtypeStruct((batch, dilation, length, LANES), F32)],
        compiler_params=_params(("parallel", "parallel", "parallel")),
        name=f"dilated_attn_d{dilation}",
    )(q, k, k, k, v, v, v)


def _mem_kv_kernel(m_ref, g_ref, w_ref, kv_ref):
    h = _rms(m_ref[...], g_ref[...]).astype(BF16)
    kv_ref[...] = jnp.dot(h, w_ref[...], preferred_element_type=F32).astype(BF16)


def _mem_kv(mem2, g, w_bf):
    n, d = mem2.shape
    cols = w_bf.shape[1]
    tm = MEM_KV_TILE
    return pl.pallas_call(
        _mem_kv_kernel,
        grid=(n // tm,),
        in_specs=[pl.BlockSpec((tm, d), lambda i: (i, 0)),
                  pl.BlockSpec((1, d), lambda i: (0, 0)),
                  pl.BlockSpec((d, cols), lambda i: (0, 0))],
        out_specs=pl.BlockSpec((tm, cols), lambda i: (i, 0)),
        out_shape=jax.ShapeDtypeStruct((n, cols), BF16),
        compiler_params=_params(("parallel",)),
        name="mem_kv",
    )(mem2, g, w_bf)


def _mix_out_kernel(o1_ref, o4_ref, o16_ref, l1_ref, l4_ref, l16_ref, up_ref, u_ref, un_ref, pinv_ref, qm_ref, kv_ref,
                    x_ref, pw_ref, ps_ref, gg_ref, wo_ref, n2_ref, rwh_ref, rwl_ref, rb_ref,
                    x1_ref, h2_ref, idx_ref, gate_ref, rank_ref, cnt_ref, base_ref,
                    o4_s, o16_s, l4_s, l16_s, cls4, carry, *, tm, seq):
    i = pl.program_id(0)
    tiles_per_seq = seq // tm
    pos0 = (i % tiles_per_seq) * tm

    for src4, src16, dst4, dst16 in ((o4_ref, o16_ref, o4_s, o16_s), (l4_ref, l16_ref, l4_s, l16_s)):
        for c in range(dst4.shape[0]):
            cols = slice(c * LANES, (c + 1) * LANES)
            for a in range(4):
                dst4[c, pl.ds(a, tm // 4, stride=4), :] = src4[0, a, :, cols]
                for b in range(4):
                    cls4[a, pl.ds(b, tm // 16, stride=4), :] = src16[0, a + 4 * b, :, cols]
            for a in range(4):
                dst16[c, pl.ds(a, tm // 4, stride=4), :] = cls4[a]

    l1, l2, l3 = l1_ref[0, 0], l4_s[0], l16_s[0]
    lm = jnp.maximum(jnp.maximum(l1, l2), l3)
    e1, e2, e3 = jnp.exp2(l1 - lm), jnp.exp2(l2 - lm), jnp.exp2(l3 - lm)
    es = e1 + e2 + e3
    spread = (lax.broadcasted_iota(jnp.int32, (LANES, ATTN_WIDTH), 1) // HEAD_DIM
              == lax.broadcasted_iota(jnp.int32, (LANES, ATTN_WIDTH), 0)).astype(BF16)

    def per_head(e):
        w = e / es
        hi = w.astype(BF16)
        lo = (w - hi.astype(F32)).astype(BF16)
        return (jnp.dot(hi, spread, preferred_element_type=F32)
                + jnp.dot(lo, spread, preferred_element_type=F32))

    w1, w2, w3 = per_head(e1), per_head(e2), per_head(e3)
    ya = []
    for c in range(ATTN_WIDTH // LANES):
        sl = slice(c * LANES, (c + 1) * LANES)
        ya.append(w1[:, sl] * o1_ref[0, 0, :, sl] + w2[:, sl] * o4_s[c] + w3[:, sl] * o16_s[c])
    ya = jnp.concatenate(ya, axis=1)

    u = u_ref[...]
    before = jnp.where(pos0 > 0, up_ref[...], 0.0)
    after = jnp.where(pos0 + tm < seq, un_ref[...], 0.0)
    ext = jnp.concatenate([before, u, after], axis=0)
    n_ext = tm + 2 * POOL_HALO

    def doubled(acc, shift):
        return pltpu.roll(acc, shift, 0) + pltpu.roll(acc, n_ext - shift, 0)

    first_group = lax.broadcasted_iota(jnp.int32, (n_ext, LANES), 1) < POOL_GROUP
    halves = []
    for half in range(POOL_WIDTH // LANES):
        e = ext[:, half * LANES:(half + 1) * LANES]
        s2 = pltpu.roll(e, 1, 0) + e
        s4 = doubled(s2, 1)
        if half == 0:
            halves.append(jnp.where(first_group, s2, s4))
        else:
            s8 = doubled(s4, 2)
            halves.append(jnp.where(first_group, s8, doubled(s8, 4)))
    window_sum = jnp.concatenate(halves, axis=1)[POOL_HALO:POOL_HALO + tm]
    d = (window_sum * pinv_ref[...] - u).astype(BF16)
    yp = jnp.dot(d, pw_ref[...], preferred_element_type=F32) * ps_ref[...]

    ym = []
    for c in range(MEM_WIDTH // LANES):
        sl = slice(c * LANES, (c + 1) * LANES)
        o, _ = _head_pair_attention(qm_ref[:, sl], kv_ref[0, :, sl],
                                    kv_ref[0, :, MEM_WIDTH + c * LANES:MEM_WIDTH + (c + 1) * LANES], None)
        ym.append(o)
    ym = jnp.concatenate(ym, axis=1)

    gg = gg_ref[...]
    a, p = ATTN_WIDTH, POOL_WIDTH
    y = jnp.concatenate([_rms(ya, gg[:, :a]), _rms(yp, gg[:, a:a + p]), _rms(ym, gg[:, a + p:])], axis=1)
    x1 = x_ref[...] + jnp.dot(y.astype(BF16), wo_ref[...], preferred_element_type=F32)
    x1_ref[...] = x1

    h2 = _rms(x1, n2_ref[...])
    _to_row_tiles(h2_ref, h2)

    hi = h2.astype(BF16)
    lo = (h2 - hi.astype(F32)).astype(BF16)
    logits = (lax.dot_general(rwh_ref[...], hi, _NT, preferred_element_type=F32)
              + lax.dot_general(rwh_ref[...], lo, _NT, preferred_element_type=F32)
              + lax.dot_general(rwl_ref[...], hi, _NT, preferred_element_type=F32)) + rb_ref[...]
    eidx = lax.broadcasted_iota(jnp.int32, logits.shape, 0)
    krow = lax.broadcasted_iota(jnp.int32, (TOP_K, tm), 0)
    vals = jnp.zeros((TOP_K, tm), F32)
    idxs = jnp.zeros((TOP_K, tm), jnp.int32)
    work = logits
    args = []
    for kk in range(TOP_K):
        best = jnp.max(work, axis=0, keepdims=True)
        arg = jnp.min(jnp.where(work == best, eidx, N_EXPERTS), axis=0, keepdims=True)
        vals = jnp.where(krow == kk, best, vals)
        idxs = jnp.where(krow == kk, arg, idxs)
        work = jnp.where(eidx == arg, -jnp.inf, work)
        args.append(arg)
    ex = jnp.exp(vals - vals[0:1])
    gate_ref[...] = ex / jnp.sum(ex, axis=0, keepdims=True)
    idx_ref[...] = idxs

    @pl.when(i == 0)
    def _():
        carry[...] = jnp.zeros_like(carry)

    chosen = (work == -jnp.inf).astype(BF16)
    earlier = (lax.broadcasted_iota(jnp.int32, (tm, tm), 0)
               < lax.broadcasted_iota(jnp.int32, (tm, tm), 1)).astype(BF16)
    before_me = jnp.dot(chosen, earlier, preferred_element_type=F32) + carry[...]
    ranks = jnp.zeros((TOP_K, tm), F32)
    for kk in range(TOP_K):
        ranks = jnp.where(krow == kk, jnp.sum(jnp.where(eidx == args[kk], before_me, 0.0), axis=0, keepdims=True),
                          ranks)
    rank_ref[...] = ranks.astype(jnp.int32)
    for part in range(tm // COMBINE_TILE):
        base_ref[part] = before_me[:, part * COMBINE_TILE:part * COMBINE_TILE + 1].astype(jnp.int32)
    carry[...] = carry[...] + jnp.sum(chosen.astype(F32), axis=1, keepdims=True)
    cnt_ref[...] = carry[...].astype(jnp.int32)


def _mix_out(attn, u, pool_inv, qm, kv, x2, pw_bd, ps, gg, wo_bf, n2, rw_hi, rw_lo, rb, batch, seq):
    t, d = x2.shape
    tm = ROUTER_TILE
    hp = tm // POOL_HALO
    n_halo = t // POOL_HALO
    tiles_per_seq = seq // tm
    row = lambda i: (i, 0)
    fixed = lambda i: (0, 0)
    cls = lambda i: (i // tiles_per_seq, 0, i % tiles_per_seq, 0)
    rowspec = lambda w: pl.BlockSpec((tm, w), row)
    clsspec = lambda dil, w: pl.BlockSpec((1, dil, tm // dil, w), cls)
    tok_cols = lambda i: (0, i)
    (o1, l1), (o4, l4), (o16, l16) = attn
    return pl.pallas_call(
        functools.partial(_mix_out_kernel, tm=tm, seq=seq),
        grid=(t // tm,),
        in_specs=[clsspec(1, ATTN_WIDTH), clsspec(4, ATTN_WIDTH), clsspec(16, ATTN_WIDTH),
                  clsspec(1, LANES), clsspec(4, LANES), clsspec(16, LANES),
                  pl.BlockSpec((POOL_HALO, POOL_WIDTH), lambda i: (jnp.maximum(i * hp - 1, 0), 0)),
                  rowspec(POOL_WIDTH),
                  pl.BlockSpec((POOL_HALO, POOL_WIDTH), lambda i: (jnp.minimum((i + 1) * hp, n_halo - 1), 0)),
                  pl.BlockSpec((tm, POOL_WIDTH), lambda i: (i % tiles_per_seq, 0)),
                  rowspec(MEM_WIDTH),
                  pl.BlockSpec((1, kv.shape[1], kv.shape[2]), lambda i: (i // tiles_per_seq, 0, 0)),
                  rowspec(d),
                  pl.BlockSpec(pw_bd.shape, fixed),
                  pl.BlockSpec(ps.shape, fixed),
                  pl.BlockSpec(gg.shape, fixed),
                  pl.BlockSpec(wo_bf.shape, fixed),
                  pl.BlockSpec(n2.shape, fixed),
                  pl.BlockSpec(rw_hi.shape, fixed),
                  pl.BlockSpec(rw_lo.shape, fixed),
                  pl.BlockSpec(rb.shape, fixed)],
        out_specs=[rowspec(d),
                   pl.BlockSpec((tm * ROW_TILE, LANES), row),
                   pl.BlockSpec((TOP_K, tm), tok_cols),
                   pl.BlockSpec((TOP_K, tm), tok_cols),
                   pl.BlockSpec((TOP_K, tm), tok_cols),
                   pl.BlockSpec((N_EXPERTS, 1), fixed),
                   pl.BlockSpec((tm // COMBINE_TILE, N_EXPERTS, 1), lambda i: (i, 0, 0))],
        out_shape=[jax.ShapeDtypeStruct((t, d), F32),
                   jax.ShapeDtypeStruct((t * ROW_TILE, LANES), F32),
                   jax.ShapeDtypeStruct((TOP_K, t), jnp.int32),
                   jax.ShapeDtypeStruct((TOP_K, t), F32),
                   jax.ShapeDtypeStruct((TOP_K, t), jnp.int32),
                   jax.ShapeDtypeStruct((N_EXPERTS, 1), jnp.int32),
                   jax.ShapeDtypeStruct((t // COMBINE_TILE, N_EXPERTS, 1), jnp.int32)],
        scratch_shapes=[pltpu.VMEM((ATTN_WIDTH // LANES, tm, LANES), F32),
                        pltpu.VMEM((ATTN_WIDTH // LANES, tm, LANES), F32),
                        pltpu.VMEM((1, tm, LANES), F32), pltpu.VMEM((1, tm, LANES), F32),
                        pltpu.VMEM((4, tm // 4, LANES), F32),
                        pltpu.VMEM((N_EXPERTS, 1), F32)],
        compiler_params=_params(("arbitrary",)),
        name="mix_out_router",
    )(o1, o4, o16, l1, l4, l16, u, u, u, pool_inv, qm, kv, x2, pw_bd, ps, gg, wo_bf, n2, rw_hi, rw_lo, rb)


def _dispatch_kernel(dest_ref, pend_ref, padded_ref, nb_ref, h_ref, xs_hbm, zeros, sem, *, tm, n_tok, n_blocks):
    i = pl.program_id(0)
    blk = MOE_BLOCK * ROW_TILE

    @pl.when(i == 0)
    def _():
        zeros[...] = jnp.zeros_like(zeros)

        def clear(block):
            return pltpu.make_async_copy(zeros, xs_hbm.at[pl.ds(pl.multiple_of(block * blk, blk), blk), :], sem)

        def for_each_cleared_block(fn):
            for e in range(N_EXPERTS):
                @pl.when(padded_ref[e] > 0)
                def _():
                    fn(clear(pend_ref[e] // MOE_BLOCK - 1))

                @pl.when(nb_ref[0] + e < n_blocks)
                def _():
                    fn(clear(nb_ref[0] + e))

        for_each_cleared_block(lambda c: c.start())
        for_each_cleared_block(lambda c: c.wait())

    def start(r, c):
        src = h_ref.at[pl.ds(pl.multiple_of(r * ROW_TILE, ROW_TILE), ROW_TILE), :]
        for kk in range(TOP_K):
            slot = dest_ref[kk * n_tok + i * tm + r]
            pltpu.async_copy(src, xs_hbm.at[pl.ds(pl.multiple_of(slot * ROW_TILE, ROW_TILE), ROW_TILE), :],
                             sem, priority=kk % 2)
        return c

    lax.fori_loop(0, tm, start, 0, unroll=4)
    n = tm * TOP_K * ROW_TILE
    pltpu.make_async_copy(xs_hbm.at[pl.ds(0, n), :], xs_hbm.at[pl.ds(0, n), :], sem).wait()


def _dispatch(dest, pad_end, padded, n_used, h2_tiles, n_slots):
    n_tok = h2_tiles.shape[0] // ROW_TILE
    tm = DISPATCH_TILE
    return pl.pallas_call(
        functools.partial(_dispatch_kernel, tm=tm, n_tok=n_tok, n_blocks=n_slots // MOE_BLOCK),
        grid_spec=pltpu.PrefetchScalarGridSpec(
            num_scalar_prefetch=4,
            grid=(n_tok // tm,),
            in_specs=[pl.BlockSpec((tm * ROW_TILE, LANES), lambda i, *_: (i, 0))],
            out_specs=pl.BlockSpec(memory_space=pl.ANY),
            scratch_shapes=[pltpu.VMEM((MOE_BLOCK * ROW_TILE, LANES), F32), pltpu.SemaphoreType.DMA(())]),
        out_shape=jax.ShapeDtypeStruct((n_slots * ROW_TILE, LANES), F32),
        compiler_params=_params(("arbitrary",)),
        name="moe_dispatch",
    )(dest, pad_end, padded, n_used, h2_tiles)


EXPERT_GROUP = 2


WEIGHT_DMA_QUEUE = 1


def _expert_kernel(first_ref, nblk_ref, nb_ref, xs_hbm, wgu_hbm, bgu_ref, wdn_hbm, bdn_ref, y_hbm,
                   wgu_f32, wdn_f32, wgu_bf, wdn_bf, xbuf, ybuf, xtail, ytail, zeros, sem_w, sem_in, sem_out,
                   *, layer, n_blocks):
    e = pl.program_id(0)
    first = first_ref[e]
    n = nblk_ref[e]
    rows = MOE_BLOCK * ROW_TILE
    n_groups = n // EXPERT_GROUP
    tail = first + n_groups * EXPERT_GROUP

    def load(block, count, dst, sem):
        return pltpu.make_async_copy(xs_hbm.at[pl.ds(pl.multiple_of(block * rows, rows), count * rows), :], dst, sem)

    def store(block, count, src, sem):
        return pltpu.make_async_copy(src, y_hbm.at[pl.ds(pl.multiple_of(block * MOE_BLOCK, MOE_BLOCK),
                                                         count * MOE_BLOCK), :], sem)

    def group_load(j, s):
        return load(first + j * EXPERT_GROUP, EXPERT_GROUP, xbuf.at[s], sem_in.at[s])

    def group_store(j, s):
        return store(first + j * EXPERT_GROUP, EXPERT_GROUP, ybuf.at[s], sem_out.at[s])

    def ffn(x_tiles, m):
        x = _from_row_tiles(x_tiles, 0, m).astype(BF16)
        gu = jnp.dot(x, wgu_bf[...], preferred_element_type=F32) + bgu_ref[0, 0]
        gate = jnp.minimum(gu[:, :D_EXPERT], SWIGLU_LIMIT)
        up = jnp.clip(gu[:, D_EXPERT:], -SWIGLU_LIMIT, SWIGLU_LIMIT)
        act = (up + 1.0) * gate * jax.nn.sigmoid(SWIGLU_ALPHA * gate)
        return (jnp.dot(act.astype(BF16), wdn_bf[...], preferred_element_type=F32) + bdn_ref[0, 0]).astype(BF16)

    def weights(ex, fn):
        s = ex % 2
        fn(wgu_hbm.at[layer, ex], wgu_f32.at[s], sem_w.at[0, s])
        fn(wdn_hbm.at[layer, ex], wdn_f32.at[s], sem_w.at[1, s])

    def w_start(src, dst, sem):
        pltpu.async_copy(src, dst, sem, priority=WEIGHT_DMA_QUEUE)

    def w_wait(src, dst, sem):
        pltpu.make_async_copy(src, dst, sem).wait()

    @pl.when(e == 0)
    def _():
        weights(0, w_start)

    has_tail = n_groups * EXPERT_GROUP < n

    @pl.when(n_groups > 0)
    def _():
        group_load(0, 0).start()

    @pl.when(has_tail)
    def _():
        load(tail, 1, xtail, sem_in.at[2]).start()

    @pl.when(e + 1 < N_EXPERTS)
    def _():
        weights(e + 1, w_start)

    weights(e, w_wait)

    @pl.when(n > 0)
    def _():
        wgu_bf[...] = wgu_f32[e % 2].astype(BF16)
        wdn_bf[...] = wdn_f32[e % 2].astype(BF16)

        def group(j, carry):
            s = j % 2
            group_load(j, s).wait()

            @pl.when(j + 1 < n_groups)
            def _():
                group_load(j + 1, 1 - s).start()

            y = ffn(xbuf.at[s], EXPERT_GROUP * MOE_BLOCK)

            @pl.when(j >= 2)
            def _():
                group_store(j - 2, s).wait()

            ybuf[s] = y
            group_store(j, s).start()
            return carry

        lax.fori_loop(0, n_groups, group, 0)

        @pl.when(has_tail)
        def _():
            load(tail, 1, xtail, sem_in.at[2]).wait()
            ytail[...] = ffn(xtail, MOE_BLOCK)
            store(tail, 1, ytail, sem_out.at[2]).start()

        @pl.when(n_groups >= 2)
        def _():
            group_store(n_groups - 2, n_groups % 2).wait()

        @pl.when(n_groups >= 1)
        def _():
            group_store(n_groups - 1, (n_groups - 1) % 2).wait()

        @pl.when(has_tail)
        def _():
            store(tail, 1, ytail, sem_out.at[2]).wait()

    @pl.when(e == N_EXPERTS - 1)
    def _():
        zeros[...] = jnp.zeros_like(zeros)

        def clear(k):
            blk = pl.multiple_of((nb_ref[0] + k) * MOE_BLOCK, MOE_BLOCK)
            return pltpu.make_async_copy(zeros, y_hbm.at[pl.ds(blk, MOE_BLOCK), :], sem_out.at[0])

        for k in range(N_EXPERTS):
            @pl.when(nb_ref[0] + k < n_blocks)
            def _():
                clear(k).start()
        for k in range(N_EXPERTS):
            @pl.when(nb_ref[0] + k < n_blocks)
            def _():
                clear(k).wait()


def _experts(layer, first_block, n_block, n_used, xs_tiles, w_gu, b_gu, w_down, b_down):
    n_slots = xs_tiles.shape[0] // ROW_TILE
    d = w_gu.shape[2]
    de2 = w_gu.shape[3]
    by_expert = lambda e, *_: (layer, e, 0, 0)
    return pl.pallas_call(
        functools.partial(_expert_kernel, layer=layer, n_blocks=n_slots // MOE_BLOCK),
        grid_spec=pltpu.PrefetchScalarGridSpec(
            num_scalar_prefetch=3,
            grid=(N_EXPERTS,),
            in_specs=[pl.BlockSpec(memory_space=pl.ANY),
                      pl.BlockSpec(memory_space=pl.ANY),
                      pl.BlockSpec((1, 1, 1, de2), by_expert),
                      pl.BlockSpec(memory_space=pl.ANY),
                      pl.BlockSpec((1, 1, 1, d), by_expert)],
            out_specs=pl.BlockSpec(memory_space=pl.ANY),
            scratch_shapes=[pltpu.VMEM((2, d, de2), F32), pltpu.VMEM((2, D_EXPERT, d), F32),
                            pltpu.VMEM((d, de2), BF16), pltpu.VMEM((D_EXPERT, d), BF16),
                            pltpu.VMEM((2, EXPERT_GROUP * MOE_BLOCK * ROW_TILE, LANES), F32),
                            pltpu.VMEM((2, EXPERT_GROUP * MOE_BLOCK, d), BF16),
                            pltpu.VMEM((MOE_BLOCK * ROW_TILE, LANES), F32),
                            pltpu.VMEM((MOE_BLOCK, d), BF16),
                            pltpu.VMEM((MOE_BLOCK, d), BF16),
                            pltpu.SemaphoreType.DMA((2, 2)),
                            pltpu.SemaphoreType.DMA((3,)), pltpu.SemaphoreType.DMA((3,))]),
        out_shape=jax.ShapeDtypeStruct((n_slots, d), BF16),
        compiler_params=_params(("arbitrary",)),
        name="moe_experts",
    )(first_block, n_block, n_used, xs_tiles, w_gu, b_gu.reshape(b_gu.shape[0], N_EXPERTS, 1, de2), w_down,
      b_down.reshape(b_down.shape[0], N_EXPERTS, 1, d))


CHUNK = 16
CHUNK_BATCH = 16
COMBINE_ROWS = COMBINE_TILE * TOP_K + 2 * N_EXPERTS * CHUNK
COMBINE_CHUNKS = COMBINE_ROWS // CHUNK


def _combine_kernel(src_ref, nbatch_ref, y_hbm, x1_ref, col_ref, gate_ref, g_ref, out_ref, ybuf, sems,
                    *, tm, n_tiles, final):
    i = pl.program_id(0)
    slot = i % 2
    batch_rows = CHUNK_BATCH * CHUNK

    @pl.when(i == 0)
    def _():
        ybuf[...] = jnp.zeros_like(ybuf)

    def fetch(tile, buf):
        def batch(b, carry):
            for u in range(CHUNK_BATCH):
                c = b * CHUNK_BATCH + u
                src = src_ref[tile * COMBINE_CHUNKS + c]
                pltpu.make_async_copy(y_hbm.at[pl.ds(pl.multiple_of(src, CHUNK), CHUNK), :],
                                      ybuf.at[buf, pl.ds(pl.multiple_of(c * CHUNK, CHUNK), CHUNK), :],
                                      sems.at[buf]).start()
            return carry

        lax.fori_loop(0, nbatch_ref[tile], batch, 0)

    @pl.when(i == 0)
    def _():
        fetch(0, 0)

    @pl.when(i + 1 < n_tiles)
    def _():
        fetch(i + 1, 1 - slot)

    col = col_ref[...]
    gates = gate_ref[...]
    col_id = lax.broadcasted_iota(jnp.int32, (tm, COMBINE_ROWS), 1)
    g = jnp.zeros((tm, COMBINE_ROWS), F32)
    for kk in range(TOP_K):
        g = jnp.where(col_id == col[:, kk:kk + 1], gates[:, kk:kk + 1], g)

    def wait_batch(b, carry):
        pltpu.make_async_copy(y_hbm.at[pl.ds(0, batch_rows), :], ybuf.at[slot, pl.ds(0, batch_rows), :],
                              sems.at[slot]).wait()
        return carry

    lax.fori_loop(0, nbatch_ref[i], wait_batch, 0)
    x = x1_ref[...] + jnp.dot(g.astype(BF16), ybuf[slot], preferred_element_type=F32)
    out_ref[...] = _rms(x, g_ref[...]) if final else x


def _combine(chunk_src, tile_batches, y_slots, x1, col_tk, gates_tk, final_g, final):
    t, d = x1.shape
    tm = COMBINE_TILE
    n_tiles = t // tm
    tok = lambda i, *_: (i, 0)
    return pl.pallas_call(
        functools.partial(_combine_kernel, tm=tm, n_tiles=n_tiles, final=final),
        grid_spec=pltpu.PrefetchScalarGridSpec(
            num_scalar_prefetch=2,
            grid=(n_tiles,),
            in_specs=[pl.BlockSpec(memory_space=pl.ANY),
                      pl.BlockSpec((tm, d), tok),
                      pl.BlockSpec((tm, TOP_K), tok),
                      pl.BlockSpec((tm, TOP_K), tok),
                      pl.BlockSpec((1, d), lambda i, *_: (0, 0))],
            out_specs=pl.BlockSpec((tm, d), tok),
            scratch_shapes=[pltpu.VMEM((2, COMBINE_ROWS, d), BF16), pltpu.SemaphoreType.DMA((2,))]),
        out_shape=jax.ShapeDtypeStruct((t, d), F32),
        compiler_params=_params(("arbitrary",)),
        name="moe_combine",
    )(chunk_src, tile_batches, y_slots, x1, col_tk, gates_tk, final_g)


def _slot_layout(top_idx, rank, counts, tile_base):
    counts = counts.reshape(-1)
    padded = (counts + MOE_BLOCK - 1) // MOE_BLOCK * MOE_BLOCK
    pad_end = jnp.cumsum(padded).astype(jnp.int32)
    pad_start = pad_end - padded
    experts = jnp.arange(N_EXPERTS, dtype=jnp.int32)
    start_of = jnp.sum(jnp.where(top_idx[..., None] == experts, pad_start, 0), axis=-1)
    dest = (start_of + rank).astype(jnp.int32)
    n_used = pad_end[-1] // MOE_BLOCK
    base = tile_base.reshape(-1, N_EXPERTS)
    run_start = pad_start[None, :] + base
    run_len = jnp.concatenate([base[1:], counts[None, :]], axis=0) - base
    seg_a = run_start // CHUNK * CHUNK
    seg_nch = jnp.where(run_len > 0, (run_start + run_len - seg_a + CHUNK - 1) // CHUNK, 0)
    buf_row0 = (jnp.cumsum(seg_nch, axis=1) - seg_nch) * CHUNK
    shift = jnp.repeat(buf_row0 - seg_a, COMBINE_TILE, axis=0)
    col = dest + jnp.sum(jnp.where(top_idx[..., None] == experts, shift[None], 0), axis=-1)
    chunk_end = jnp.cumsum(seg_nch, axis=1)
    c = jnp.arange(COMBINE_CHUNKS, dtype=jnp.int32)
    owner = c[None, :, None] >= chunk_end[:, None, :]
    owner_e = jnp.minimum(jnp.sum(owner.astype(jnp.int32), axis=-1), N_EXPERTS - 1)
    pick = owner_e[..., None] == experts
    first_chunk = jnp.sum(jnp.where(pick, (chunk_end - seg_nch)[:, None, :], 0), axis=-1)
    run_a = jnp.sum(jnp.where(pick, seg_a[:, None, :], 0), axis=-1)
    total = chunk_end[:, -1]
    chunk_src = jnp.where(c[None, :] < total[:, None], run_a + (c[None, :] - first_chunk) * CHUNK, 0)
    flat = lambda a: a.reshape(-1).astype(jnp.int32)
    return (dest, pad_end, padded.astype(jnp.int32), n_used.reshape(1).astype(jnp.int32),
            flat(chunk_src), flat((total + CHUNK_BATCH - 1) // CHUNK_BATCH), col.astype(jnp.int32))


def _rope_tables(seq):
    half = HEAD_DIM // 2
    inv_freq = ROPE_THETA ** (-jnp.arange(half, dtype=F32) / half)
    ang = jnp.arange(seq, dtype=F32)[:, None] * inv_freq[None, :]
    cos, sin = jnp.cos(ang), jnp.sin(ang)
    reps = LANES // HEAD_DIM
    cos_l = jnp.tile(jnp.concatenate([cos, cos], axis=1), (1, reps))
    sin_l = jnp.tile(jnp.concatenate([-sin, sin], axis=1), (1, reps))
    return cos_l, sin_l


def _pool_inverse_counts(seq):
    pos = jnp.arange(seq, dtype=jnp.int32)[:, None]
    cols = []
    for w in POOL_WINDOWS:
        cnt = jnp.minimum(pos + (w - 1 - w // 2), seq - 1) + 1 - jnp.maximum(pos - w // 2, 0)
        cols.append(jnp.broadcast_to(1.0 / cnt.astype(F32), (seq, POOL_GROUP)))
    return jnp.concatenate(cols, axis=1)


def kernel(x, mem, norm1_g, w_in, pool_w, pool_scale, mem_norm_g, w_mem_kv, grp_norm_g, w_out, norm2_g,
           router_w, router_b, w_gu, b_gu, w_down, b_down, final_g):
    batch, seq, d = x.shape
    depth = w_in.shape[0]
    n_mem = mem.shape[1]
    t = batch * seq
    n_blocks = t * TOP_K // MOE_BLOCK + N_EXPERTS
    cos_l, sin_l = _rope_tables(seq)
    pool_inv = _pool_inverse_counts(seq)
    x2 = x.reshape(t, d)
    mem2 = mem.reshape(batch * n_mem, d)
    row = lambda a: a.reshape(1, -1)
    for l in range(depth):
        qkv, u, qm = _in_proj(x2, row(norm1_g[l]), w_in[l].astype(BF16), cos_l, sin_l, batch, seq)
        attn = [_dilated_attention(*qkv[n]) for n in range(len(DILATIONS))]
        kv = _mem_kv(mem2, row(mem_norm_g[l]), w_mem_kv[l].astype(BF16)).reshape(batch, n_mem, 2 * MEM_WIDTH)
        pw_bd = jax.scipy.linalg.block_diag(*[pool_w[l, g] for g in range(len(POOL_WINDOWS))]).astype(BF16)
        rw_t = router_w[l].T
        rw_hi = rw_t.astype(BF16)
        rw_lo = (rw_t - rw_hi.astype(F32)).astype(BF16)
        x1, h2_tiles, top_idx, gates, rank, counts, tile_base = _mix_out(
            attn, u, pool_inv, qm, kv, x2, pw_bd, row(pool_scale[l]), row(grp_norm_g[l]), w_out[l].astype(BF16),
            row(norm2_g[l]), rw_hi, rw_lo, router_b[l].reshape(-1, 1), batch, seq)
        dest, pad_end, padded, n_used, chunk_src, tile_batches, col = _slot_layout(
            top_idx, rank, counts, tile_base)
        xs_tiles = _dispatch(dest.reshape(-1), pad_end, padded, n_used, h2_tiles, n_blocks * MOE_BLOCK)
        y_slots = _experts(l, (pad_end - padded) // MOE_BLOCK, padded // MOE_BLOCK, n_used, xs_tiles,
                           w_gu, b_gu, w_down, b_down)
        x2 = _combine(chunk_src, tile_batches, y_slots, x1, col.T, gates.T, row(final_g),
                      final=(l == depth - 1))
    return x2.reshape(batch, seq, d)
```
